```python
import math
import jax
import jax.numpy as jnp
from jax import lax
import numpy as np

D_MODEL = 1024
BATCH = 4
SEQ = 4096
DEPTH = 2

QB = 128
EPS = 1e-6
A_HEADS = 4
A_HD = 64
A_VD = 2 * A_HD
B_HEADS = 8
B_Q_RANK = 256
B_KV_RANK = 128
B_NOPE = 64
B_ROPE = 32
B_VD = 64
ROPE_THETA = 10000.0
C_HEADS = 8
C_KV_HEADS = 2
C_HD = 64
WINDOW = 128
N_BUCKETS = 32
MAX_DIST = 128
N_BIAS_HEADS = A_HEADS + C_HEADS
D_FF = 2816
CONV_W = 3
A_Q = A_HEADS * 2 * A_HD
A_V = A_HEADS * A_VD
C_Q = C_HEADS * C_HD
C_KV = C_KV_HEADS * C_HD
IN_SIZES = (A_Q, A_Q, A_V, B_Q_RANK, B_KV_RANK, B_ROPE, C_Q, C_KV, C_KV, D_MODEL, D_MODEL, D_MODEL)
D_IN = sum(IN_SIZES)

kernel_name = 'hybrid_diffattn_mla_swa_convffn'

F32 = jnp.float32


def rms_norm(t, g):
    tf = t.astype(F32)
    y = tf * lax.rsqrt(jnp.mean(tf * tf, axis=-1, keepdims=True) + EPS)
    return (y * g.astype(F32)).astype(t.dtype)


def t5_bucket(rel):
    n = jnp.maximum(rel, 0)
    max_exact = N_BUCKETS // 2
    nf = jnp.maximum(n, 1).astype(F32)
    large = max_exact + (jnp.log(nf / max_exact) / math.log(MAX_DIST / max_exact)
                         * (N_BUCKETS - max_exact)).astype(jnp.int32)
    large = jnp.minimum(large, N_BUCKETS - 1)
    return jnp.where(n < max_exact, n, large)


def rel_bias(table, q_pos, k_pos):
    rel = q_pos[..., :, None] - k_pos[..., None, :]
    return jnp.take(table, t5_bucket(rel), axis=0).astype(F32)


def causal_mask(q0, q1):
    return jnp.arange(q0, q1)[:, None] >= jnp.arange(q1)[None, :]


def rope_cos_sin(pos):
    inv = 1.0 / (ROPE_THETA ** (jnp.arange(0, B_ROPE, 2, dtype=F32) / B_ROPE))
    ang = pos.astype(F32)[..., None] * inv
    return jnp.cos(ang), jnp.sin(ang)


def apply_rope(t, cos, sin):
    t1, t2 = jnp.split(t.astype(F32), 2, axis=-1)
    return jnp.concatenate([t1 * cos - t2 * sin, t2 * cos + t1 * sin], axis=-1).astype(t.dtype)


def diff_attention(q, k, v, pos, table, lam_q1, lam_k1, lam_q2, lam_k2, q_g, k_g, subln_g, lambda_init):
    b, s = q.shape[:2]
    q = rms_norm(q, q_g).transpose(0, 2, 3, 1, 4)
    k = rms_norm(k, k_g).transpose(0, 2, 3, 1, 4)
    v = v.transpose(0, 2, 1, 3)
    lam = (jnp.exp(jnp.sum(lam_q1.astype(F32) * lam_k1.astype(F32)))
           - jnp.exp(jnp.sum(lam_q2.astype(F32) * lam_k2.astype(F32))) + lambda_init)
    scale = A_HD ** -0.5
    outs = []
    for i in range(s // QB):
        q0, q1 = i * QB, (i + 1) * QB
        logits = jnp.einsum('bhmqd,bhmkd->bhmqk', q[:, :, :, q0:q1], k[:, :, :, :q1]).astype(F32) * scale
        bias = rel_bias(table, pos[:, q0:q1], pos[:, :q1]).transpose(0, 3, 1, 2)
        logits = jnp.where(causal_mask(q0, q1), logits + bias[:, :, None], -jnp.inf)
        p = jax.nn.softmax(logits, axis=-1)
        p = p[:, :, 0] - lam * p[:, :, 1]
        outs.append(jnp.einsum('bhqk,bhkd->bhqd', p.astype(v.dtype), v[:, :, :q1]))
    o = jnp.concatenate(outs, axis=2)
    o = rms_norm(o, subln_g) * (1.0 - lambda_init)
    return o.transpose(0, 2, 1, 3).reshape(b, s, A_HEADS * A_VD)


def mla_attention(c_q, c_kv, k_pe, pos, q_a_g, kv_a_g, w_uq, w_ukv, qn_g, qr_g, kn_g, kr_g):
    b, s = c_q.shape[:2]
    q = (rms_norm(c_q, q_a_g) @ w_uq).reshape(b, s, B_HEADS, B_NOPE + B_ROPE)
    kv = (rms_norm(c_kv, kv_a_g) @ w_ukv).reshape(b, s, B_HEADS, B_NOPE + B_VD)
    cos, sin = rope_cos_sin(pos)
    q_nope = rms_norm(q[..., :B_NOPE], qn_g).transpose(0, 2, 1, 3)
    q_pe = apply_rope(rms_norm(q[..., B_NOPE:], qr_g), cos[:, :, None], sin[:, :, None]).transpose(0, 2, 1, 3)
    k_nope = rms_norm(kv[..., :B_NOPE], kn_g).transpose(0, 2, 1, 3)
    v = kv[..., B_NOPE:].transpose(0, 2, 1, 3)
    k_pe = apply_rope(rms_norm(k_pe, kr_g), cos, sin)
    scale = (B_NOPE + B_ROPE) ** -0.5
    outs = []
    for i in range(s // QB):
        q0, q1 = i * QB, (i + 1) * QB
        logits = (jnp.einsum('bhqd,bhkd->bhqk', q_nope[:, :, q0:q1], k_nope[:, :, :q1]).astype(F32)
                  + jnp.einsum('bhqr,bkr->bhqk', q_pe[:, :, q0:q1], k_pe[:, :q1]).astype(F32)) * scale
        logits = jnp.where(causal_mask(q0, q1), logits, -jnp.inf)
        p = jax.nn.softmax(logits, axis=-1)
        outs.append(jnp.einsum('bhqk,bhkd->bhqd', p.astype(v.dtype), v[:, :, :q1]))
    o = jnp.concatenate(outs, axis=2)
    return o.transpose(0, 2, 1, 3).reshape(b, s, B_HEADS * B_VD)


def swa_sink_attention(q, k, v, pos, table, sinks, q_g, k_g):
    b, s = q.shape[:2]
    nb = s // WINDOW
    grp = C_HEADS // C_KV_HEADS
    q = rms_norm(q, q_g)
    k = rms_norm(k, k_g)
    qb = q.reshape(b, nb, WINDOW, C_KV_HEADS, grp, C_HD).transpose(0, 3, 4, 1, 2, 5)

    def band(t):
        t = jnp.pad(t, ((0, 0), (WINDOW, 0), (0, 0), (0, 0)))
        t = t.reshape(b, nb + 1, WINDOW, C_KV_HEADS, C_HD).transpose(0, 3, 1, 2, 4)
        return jnp.concatenate([t[:, :, :-1], t[:, :, 1:]], axis=3)

    kb, vb = band(k), band(v)
    logits = jnp.einsum('bkgnqd,bkncd->bkgnqc', qb, kb).astype(F32) * (C_HD ** -0.5)
    pos_q = pos.reshape(b, nb, WINDOW)
    pos_p = jnp.pad(pos, ((0, 0), (WINDOW, 0))).reshape(b, nb + 1, WINDOW)
    pos_k = jnp.concatenate([pos_p[:, :-1], pos_p[:, 1:]], axis=-1)
    bias = rel_bias(table, pos_q, pos_k)
    bias = bias.reshape(b, nb, WINDOW, 2 * WINDOW, C_KV_HEADS, grp).transpose(0, 4, 5, 1, 2, 3)
    a_idx = jnp.arange(WINDOW)[:, None]
    c_idx = jnp.arange(2 * WINDOW)[None, :]
    blk = jnp.arange(nb)[:, None, None]
    mask = (c_idx > a_idx) & (c_idx <= a_idx + WINDOW) & ((blk > 0) | (c_idx >= WINDOW))
    logits = jnp.where(mask, logits + bias, -jnp.inf)
    sink = jnp.broadcast_to(sinks.astype(F32).reshape(C_KV_HEADS, grp)[None, :, :, None, None, None],
                            logits.shape[:-1] + (1,))
    p = jax.nn.softmax(jnp.concatenate([logits, sink], axis=-1), axis=-1)[..., :-1]
    o = jnp.einsum('bkgnqc,bkncd->bkgnqd', p.astype(vb.dtype), vb)
    return o.transpose(0, 3, 4, 1, 2, 5).reshape(b, s, C_HEADS * C_HD)


def conv_ffn(h, w_up, conv_w, conv_b, w_down):
    u = h @ w_up
    u = lax.conv_general_dilated(u, conv_w.astype(u.dtype), window_strides=(1,), padding=((CONV_W - 1, 0),),
                                 dimension_numbers=('NWC', 'WIO', 'NWC'),
                                 feature_group_count=2 * D_FF) + conv_b
    g, up = jnp.split(u, 2, axis=-1)
    return (jax.nn.silu(g) * up) @ w_down


def setup_inputs(seed: int = 0) -> dict:
    key = jax.random.key(seed)
    ks = iter(jax.random.split(key, 48))
    L = DEPTH

    def nrm(shape, scale):
        return jax.random.normal(next(ks), shape, F32) * scale

    def gain(shape):
        return 1.0 + nrm(shape, 0.02)

    x = nrm((BATCH, SEQ, D_MODEL), 1.0)
    offset = jax.random.randint(next(ks), (BATCH, 1), 0, SEQ, dtype=jnp.int32)
    positions = offset + jnp.arange(SEQ, dtype=jnp.int32)[None, :]
    return {
        'x': x,
        'positions': positions,
        'rel_bias_table': nrm((N_BUCKETS, N_BIAS_HEADS), 0.2),
        'ln_mix_g': gain((L, D_MODEL)),
        'w_in': nrm((L, D_MODEL, D_IN), D_MODEL ** -0.5),
        'a_q_g': gain((L, A_HD)),
        'a_k_g': gain((L, A_HD)),
        'a_lam_q1': nrm((L, A_HD), 0.1),
        'a_lam_k1': nrm((L, A_HD), 0.1),
        'a_lam_q2': nrm((L, A_HD), 0.1),
        'a_lam_k2': nrm((L, A_HD), 0.1),
        'a_subln_g': gain((L, A_VD)),
        'b_q_a_g': gain((L, B_Q_RANK)),
        'b_kv_a_g': gain((L, B_KV_RANK)),
        'b_w_uq': nrm((L, B_Q_RANK, B_HEADS * (B_NOPE + B_ROPE)), B_Q_RANK ** -0.5),
        'b_w_ukv': nrm((L, B_KV_RANK, B_HEADS * (B_NOPE + B_VD)), B_KV_RANK ** -0.5),
        'b_qn_g': gain((L, B_NOPE)),
        'b_qr_g': gain((L, B_ROPE)),
        'b_kn_g': gain((L, B_NOPE)),
        'b_kr_g': gain((L, B_ROPE)),
        'c_q_g': gain((L, C_HD)),
        'c_k_g': gain((L, C_HD)),
        'c_sinks': nrm((L, C_HEADS), 1.0),
        'p_a': nrm((L, A_V, D_MODEL), A_V ** -0.5),
        'p_b': nrm((L, B_HEADS * B_VD, D_MODEL), (B_HEADS * B_VD) ** -0.5),
        'p_c': nrm((L, C_Q, D_MODEL), C_Q ** -0.5),
        'w_o': nrm((L, D_MODEL, D_MODEL), D_MODEL ** -0.5),
        'ln_ffn_g': gain((L, D_MODEL)),
        'w_up': nrm((L, D_MODEL, 2 * D_FF), D_MODEL ** -0.5),
        'conv_w': nrm((L, CONV_W, 1, 2 * D_FF), CONV_W ** -0.5),
        'conv_b': nrm((L, 2 * D_FF), 0.02),
        'w_down': nrm((L, D_FF, D_MODEL), D_FF ** -0.5),
    }


def reference(x, positions, rel_bias_table, ln_mix_g, w_in, a_q_g, a_k_g, a_lam_q1, a_lam_k1, a_lam_q2,
              a_lam_k2, a_subln_g, b_q_a_g, b_kv_a_g, b_w_uq, b_w_ukv, b_qn_g, b_qr_g, b_kn_g, b_kr_g,
              c_q_g, c_k_g, c_sinks, p_a, p_b, p_c, w_o, ln_ffn_g, w_up, conv_w, conv_b, w_down):
    b, s = x.shape[:2]
    offsets = np.cumsum(IN_SIZES)[:-1].tolist()
    table_a = rel_bias_table[:, :A_HEADS]
    table_c = rel_bias_table[:, A_HEADS:]
    for l in range(DEPTH):
        lambda_init = 0.8 - 0.6 * math.exp(-0.3 * l)
        h = rms_norm(x, ln_mix_g[l])
        proj = h @ w_in[l]
        qa, ka, va, cq, ckv, kpe, qc, kc, vc, ga, gb, gc = jnp.split(proj, offsets, axis=-1)
        ya = diff_attention(qa.reshape(b, s, A_HEADS, 2, A_HD), ka.reshape(b, s, A_HEADS, 2, A_HD),
                            va.reshape(b, s, A_HEADS, A_VD), positions, table_a,
                            a_lam_q1[l], a_lam_k1[l], a_lam_q2[l], a_lam_k2[l],
                            a_q_g[l], a_k_g[l], a_subln_g[l], lambda_init)
        yb = mla_attention(cq, ckv, kpe, positions, b_q_a_g[l], b_kv_a_g[l], b_w_uq[l], b_w_ukv[l],
                           b_qn_g[l], b_qr_g[l], b_kn_g[l], b_kr_g[l])
        yc = swa_sink_attention(qc.reshape(b, s, C_HEADS, C_HD), kc.reshape(b, s, C_KV_HEADS, C_HD),
                                vc.reshape(b, s, C_KV_HEADS, C_HD), positions, table_c,
                                c_sinks[l], c_q_g[l], c_k_g[l])
        merged = (jax.nn.sigmoid(ga) * (ya @ p_a[l])
                  + jax.nn.sigmoid(gb) * (yb @ p_b[l])
                  + jax.nn.sigmoid(gc) * (yc @ p_c[l]))
        x = x + merged @ w_o[l]
        x = x + conv_ffn(rms_norm(x, ln_ffn_g[l]), w_up[l], conv_w[l], conv_b[l], w_down[l])
    return x
```

```python
import functools
import math

import jax
import jax.numpy as jnp
import numpy as np
from jax import lax
from jax.experimental import pallas as pl
from jax.experimental.pallas import tpu as pltpu

F32 = jnp.float32
BF16 = jnp.bfloat16

D_MODEL = 1024
DEPTH = 2
EPS = 1e-6
A_HEADS = 4
A_HD = 64
A_VD = 2 * A_HD
B_HEADS = 8
B_Q_RANK = 256
B_KV_RANK = 128
B_NOPE = 64
B_ROPE = 32
B_VD = 64
ROPE_THETA = 10000.0
C_HEADS = 8
C_KV_HEADS = 2
C_HD = 64
WINDOW = 128
N_BUCKETS = 32
MAX_DIST = 128
D_FF = 2816
CONV_W = 3

LANES = 128
HALF = LANES // 2
NEG = -1e30
VMEM_LIMIT = 56 * 1024 * 1024

T_ATT = 512
TM_IN = 512
TM_MERGE = 512
TM_FFN = 512
FF_CHUNK = 256
N_FF_CHUNKS = D_FF // FF_CHUNK
HALO = 8

_SEG = {}
_off = 0
for _name, _w in (("qa", 512), ("ka", 512), ("va", 512), ("cq", 256), ("ckv", 128), ("kpe", 128),
                  ("kpe_sw", 128), ("qc", 512), ("kc", 256), ("vc", 256)):
    _SEG[_name] = (_off, _off + _w)
    _off += _w
W_ATTN_COLS = _off


def _params(sem, vmem=VMEM_LIMIT):
    return pltpu.CompilerParams(dimension_semantics=sem, vmem_limit_bytes=vmem)


def _const_spec(shape):
    nd = len(shape)
    return pl.BlockSpec(shape, lambda *_: (0,) * nd, pipeline_mode=pl.Buffered(1))


def _lane_lo():
    return lax.broadcasted_iota(jnp.int32, (1, LANES), 1) < HALF


def _rope_kernel(pos_ref, inv_ref, sign_ref, cos_ref, sin_ref):
    ang = pos_ref[...].astype(F32) * inv_ref[...]
    cos_ref[...] = jnp.cos(ang)
    sin_ref[...] = jnp.sin(ang) * sign_ref[...]


def _rope_tables(pos_col):
    n = pos_col.shape[0]
    inv = 1.0 / (ROPE_THETA ** (jnp.arange(0, B_ROPE, 2, dtype=F32) / B_ROPE))
    z = jnp.zeros((B_NOPE,), F32)
    zp = jnp.zeros((LANES - B_NOPE - B_ROPE,), F32)
    inv_pat = jnp.concatenate([z, inv, inv, zp])[None, :]
    ones = jnp.ones((B_ROPE // 2,), F32)
    sign_pat = jnp.concatenate([z, -ones, ones, zp])[None, :]
    tm = 2048
    return pl.pallas_call(
        _rope_kernel,
        grid=(n // tm,),
        in_specs=[pl.BlockSpec((tm, 1), lambda i: (i, 0)),
                  pl.BlockSpec((1, LANES), lambda i: (0, 0)),
                  pl.BlockSpec((1, LANES), lambda i: (0, 0))],
        out_specs=[pl.BlockSpec((tm, LANES), lambda i: (i, 0))] * 2,
        out_shape=[jax.ShapeDtypeStruct((n, LANES), F32)] * 2,
        compiler_params=_params(("parallel",)),
        name="rope_tables",
    )(pos_col, inv_pat, sign_pat)


def _bucket(rel):
    n = jnp.maximum(rel, 0)
    max_exact = N_BUCKETS // 2
    nf = jnp.maximum(n, 1).astype(F32)
    large = max_exact + (jnp.log(nf / max_exact) / math.log(MAX_DIST / max_exact)
                         * (N_BUCKETS - max_exact)).astype(jnp.int32)
    large = jnp.minimum(large, N_BUCKETS - 1)
    return jnp.where(n < max_exact, n, large)


def _lookup(tab_ref, bucket, col):
    out = jnp.zeros(bucket.shape, F32)
    for k in range(N_BUCKETS):
        out = jnp.where(bucket == k, tab_ref[k, col], out)
    return out


def _bias_a_kernel(tab_ref, out_ref):
    h = pl.program_id(0)
    t = out_ref.shape[-1]
    row = lax.broadcasted_iota(jnp.int32, (t, t), 0)
    col = lax.broadcasted_iota(jnp.int32, (t, t), 1)
    far = tab_ref[N_BUCKETS - 1, h]
    rel = row - col
    out_ref[0, 0] = jnp.where(rel >= 0, _lookup(tab_ref, _bucket(rel), h) - far, NEG)
    out_ref[0, 1] = _lookup(tab_ref, _bucket(rel + t), h) - far


def _bias_c_kernel(tab_ref, out_ref):
    h = pl.program_id(0)
    row = lax.broadcasted_iota(jnp.int32, (WINDOW, 2 * WINDOW), 0)
    col = lax.broadcasted_iota(jnp.int32, (WINDOW, 2 * WINDOW), 1)
    rel = row + WINDOW - col
    valid = (rel >= 0) & (rel < WINDOW)
    out_ref[0] = jnp.where(valid, _lookup(tab_ref, _bucket(rel), h + A_HEADS), NEG)


def _bias_tiles(table):
    smem = pl.BlockSpec(memory_space=pltpu.SMEM)
    bias_a = pl.pallas_call(
        _bias_a_kernel,
        grid=(A_HEADS,),
        in_specs=[smem],
        out_specs=pl.BlockSpec((1, 2, T_ATT, T_ATT), lambda h: (h, 0, 0, 0)),
        out_shape=jax.ShapeDtypeStruct((A_HEADS, 2, T_ATT, T_ATT), F32),
        compiler_params=_params(("parallel",)),
        name="bias_a",
    )(table)
    bias_c = pl.pallas_call(
        _bias_c_kernel,
        grid=(C_HEADS,),
        in_specs=[smem],
        out_specs=pl.BlockSpec((1, WINDOW, 2 * WINDOW), lambda h: (h, 0, 0)),
        out_shape=jax.ShapeDtypeStruct((C_HEADS, WINDOW, 2 * WINDOW), F32),
        compiler_params=_params(("parallel",)),
        name="bias_c",
    )(table)
    return bias_a, bias_c


def _rms(t, width):
    return lax.rsqrt(jnp.sum(t * t, axis=-1, keepdims=True) / width + EPS)


def _in_kernel(x_ref, cos_ref, sin_ref, gmix_ref, ga_ref, glat_ref, gqb_ref, gkb_ref, gc_ref,
               w_ref, wuq_ref, wukv_ref,
               qa_ref, ka_ref, va_ref, qb_ref, kb_ref, vb_ref, qc_ref, kc_ref, vc_ref):
    x = x_ref[...]
    h = (x * _rms(x, D_MODEL) * gmix_ref[...]).astype(BF16)
    lo = _lane_lo()

    def proj(name):
        a, b = _SEG[name]
        return jnp.dot(h, w_ref[:, a:b], preferred_element_type=F32)

    def norm_halves(t, g, out_ref):
        for j in range(t.shape[1] // LANES):
            sl = slice(j * LANES, (j + 1) * LANES)
            tj = t[:, sl]
            sq = tj * tj
            s_lo = jnp.sum(jnp.where(lo, sq, 0.0), axis=-1, keepdims=True)
            s_hi = jnp.sum(jnp.where(lo, 0.0, sq), axis=-1, keepdims=True)
            r = jnp.where(lo, lax.rsqrt(s_lo / HALF + EPS), lax.rsqrt(s_hi / HALF + EPS))
            out_ref[:, sl] = (tj * r * g[:, sl]).astype(out_ref.dtype)

    norm_halves(proj("qa"), ga_ref[0:1, :], qa_ref)
    norm_halves(proj("ka"), ga_ref[1:2, :], ka_ref)
    va_ref[...] = proj("va").astype(BF16)
    gc = gc_ref[...]
    norm_halves(proj("qc"), gc[:, :C_HEADS * C_HD], qc_ref)
    norm_halves(proj("kc"), gc[:, C_HEADS * C_HD:], kc_ref)
    vc_ref[...] = proj("vc").astype(BF16)

    cos = cos_ref[...]
    sin = sin_ref[...]
    glat = glat_ref[...]

    cq = proj("cq")
    cqn = (cq * _rms(cq, B_Q_RANK) * glat[:, :B_Q_RANK]).astype(BF16)
    uq = jnp.dot(cqn, wuq_ref[...], preferred_element_type=F32)
    hw = B_HEADS * LANES
    for j in range(B_HEADS):
        sl = slice(j * LANES, (j + 1) * LANES)
        raw = uq[:, sl]
        raw_sw = uq[:, hw + j * LANES: hw + (j + 1) * LANES]
        sq = raw * raw
        s_n = jnp.sum(jnp.where(lo, sq, 0.0), axis=-1, keepdims=True)
        s_r = jnp.sum(jnp.where(lo, 0.0, sq), axis=-1, keepdims=True)
        r = jnp.where(lo, lax.rsqrt(s_n / B_NOPE + EPS), lax.rsqrt(s_r / B_ROPE + EPS))
        out = r * (raw * gqb_ref[0:1, sl] * cos + raw_sw * gqb_ref[1:2, sl] * sin)
        qb_ref[:, sl] = out.astype(BF16)

    ckv = proj("ckv")
    ckvn = (ckv * _rms(ckv, B_KV_RANK) * glat[:, B_Q_RANK:]).astype(BF16)
    ukv = jnp.dot(ckvn, wukv_ref[...], preferred_element_type=F32)
    vb_ref[...] = ukv[:, hw:].astype(BF16)
    kpe = proj("kpe")
    kpe_sw = proj("kpe_sw")
    kpe_out = _rms(kpe, B_ROPE) * (kpe * gkb_ref[1:2, :] * cos + kpe_sw * gkb_ref[2:3, :] * sin)
    for j in range(B_HEADS):
        sl = slice(j * LANES, (j + 1) * LANES)
        raw = ukv[:, sl]
        kb_ref[:, sl] = (raw * _rms(raw, B_NOPE) * gkb_ref[0:1, :] + kpe_out).astype(BF16)


def _in_proj(x2, cos, sin, gmix, ga, glat, gqb, gkb, gc, w_attn, wuq, wukv):
    n = x2.shape[0]
    tm = TM_IN
    row = lambda w: pl.BlockSpec((tm, w), lambda i: (i, 0))
    outs = (("qa", 512), ("ka", 512), ("va", 512), ("qb", 1024), ("kb", 1024), ("vb", 512),
            ("qc", 512), ("kc", 256), ("vc", 256))
    return pl.pallas_call(
        _in_kernel,
        grid=(n // tm,),
        in_specs=[row(D_MODEL), row(LANES), row(LANES),
                  _const_spec(gmix.shape), _const_spec(ga.shape), _const_spec(glat.shape),
                  _const_spec(gqb.shape), _const_spec(gkb.shape), _const_spec(gc.shape),
                  _const_spec(w_attn.shape), _const_spec(wuq.shape), _const_spec(wukv.shape)],
        out_specs=[row(w) for _, w in outs],
        out_shape=[jax.ShapeDtypeStruct((n, w), BF16) for _, w in outs],
        compiler_params=_params(("parallel",)),
        name="in_proj",
    )(x2, cos, sin, gmix, ga, glat, gqb, gkb, gc, w_attn, wuq, wukv)


def _flash_kernel(*refs, mode, lambda_init):
    if mode == "diff":
        q_ref, k_ref, v_ref, bias_ref, lam_ref, subg_ref, o_ref = refs
    else:
        q_ref, k_ref, v_ref, o_ref = refs
    t = q_ref.shape[0]
    qi = pl.program_id(2)
    lo = _lane_lo()
    q = q_ref[...]
    if mode == "diff":
        zero = jnp.zeros_like(q)
        qs = (jnp.where(lo, q, zero), jnp.where(lo, zero, q))
    else:
        qs = (q[:, :LANES], q[:, LANES:])

    def step(ki, carry, kind):
        start = pl.multiple_of(ki * t, t)
        k = k_ref[pl.ds(start, t), :]
        v = v_ref[pl.ds(start, t), :]
        new = []
        for j in range(2):
            m, l, acc = carry[j]
            kj = k if mode == "diff" else k[:, j * LANES:(j + 1) * LANES]
            s = lax.dot_general(qs[j], kj, (((1,), (1,)), ((), ())), preferred_element_type=F32)
            if mode == "diff":
                if kind == "diag":
                    s = s + bias_ref[0, 0]
                elif kind == "prev":
                    s = s + bias_ref[0, 1]
            elif kind == "diag":
                row = lax.broadcasted_iota(jnp.int32, (t, t), 0)
                col = lax.broadcasted_iota(jnp.int32, (t, t), 1)
                s = jnp.where(row >= col, s, NEG)
            m_new = jnp.maximum(m, jnp.max(s, axis=-1, keepdims=True))
            alpha = jnp.exp(m - m_new)
            p = jnp.exp(s - m_new)
            l = alpha * l + jnp.sum(p, axis=-1, keepdims=True)
            acc = alpha * acc + jnp.dot(p.astype(BF16), v, preferred_element_type=F32)
            new.append((m_new, l, acc))
        return tuple(new)

    init = tuple((jnp.full((t, 1), NEG, F32), jnp.zeros((t, 1), F32), jnp.zeros((t, LANES), F32))
                 for _ in range(2))
    if mode == "diff":
        n_far = jnp.maximum(qi - 1, 0)
        carry = lax.fori_loop(0, n_far, functools.partial(step, kind="far"), init)
        carry = lax.fori_loop(n_far, qi, functools.partial(step, kind="prev"), carry)
    else:
        carry = lax.fori_loop(0, qi, functools.partial(step, kind="far"), init)
    (_, l0, acc0), (_, l1, acc1) = step(qi, carry, "diag")
    o0 = acc0 / l0
    o1 = acc1 / l1
    if mode == "diff":
        lv = lam_ref[...]
        lam = (jnp.exp(jnp.sum(lv[0:1] * lv[1:2], axis=-1, keepdims=True))
               - jnp.exp(jnp.sum(lv[2:3] * lv[3:4], axis=-1, keepdims=True)) + lambda_init)
        o = o0 - lam * o1
        o = o * _rms(o, A_VD) * subg_ref[...] * (1.0 - lambda_init)
    else:
        o = jnp.where(lo, o0, o1)
    o_ref[...] = o.astype(o_ref.dtype)


def _flash(q, k, v, batch, seq, mode, extra=(), lambda_init=0.0):
    n = q.shape[0]
    t = T_ATT
    nq = seq // t
    qw = LANES if mode == "diff" else 2 * LANES
    groups = q.shape[1] // qw
    in_specs = [pl.BlockSpec((t, qw), lambda b, g, i: (b * nq + i, g)),
                pl.BlockSpec((seq, qw), lambda b, g, i: (b, g)),
                pl.BlockSpec((seq, LANES), lambda b, g, i: (b, g))]
    if mode == "diff":
        bias, lam, subg = extra
        in_specs += [pl.BlockSpec((1, 2, t, t), lambda b, g, i: (g, 0, 0, 0)),
                     pl.BlockSpec(lam.shape, lambda b, g, i: (0, 0)),
                     pl.BlockSpec(subg.shape, lambda b, g, i: (0, 0))]
    return pl.pallas_call(
        functools.partial(_flash_kernel, mode=mode, lambda_init=lambda_init),
        grid=(batch, groups, nq),
        in_specs=in_specs,
        out_specs=pl.BlockSpec((t, LANES), lambda b, g, i: (b * nq + i, g)),
        out_shape=jax.ShapeDtypeStruct((n, groups * LANES), BF16),
        compiler_params=_params(("parallel", "parallel", "arbitrary")),
        name="flash_" + mode,
    )(q, k, v, *extra)


def _swa_kernel(sink_ref, q_ref, kp_ref, kc_ref, vp_ref, vc_ref, bias_ref, o_ref):
    nb = pl.program_id(1)
    lo = _lane_lo()
    col = lax.broadcasted_iota(jnp.int32, (WINDOW, 2 * WINDOW), 1)
    keep = (col >= WINDOW) | (nb > 0)
    grp = C_HEADS // C_KV_HEADS
    for hp in range(C_HEADS // 2):
        kv = (2 * hp) // grp
        sl = slice(hp * LANES, (hp + 1) * LANES)
        ksl = slice(kv * LANES, (kv + 1) * LANES)
        q = q_ref[:, sl]
        zero = jnp.zeros_like(q)
        k = jnp.concatenate([kp_ref[:, ksl], kc_ref[:, ksl]], axis=0)
        v = jnp.concatenate([vp_ref[:, ksl], vc_ref[:, ksl]], axis=0)
        outs = []
        for j in range(2):
            head = 2 * hp + j
            qj = jnp.where(lo, q, zero) if j == 0 else jnp.where(lo, zero, q)
            s = lax.dot_general(qj, k, (((1,), (1,)), ((), ())), preferred_element_type=F32)
            s = jnp.where(keep, s + bias_ref[head], NEG)
            sink = sink_ref[head]
            m = jnp.maximum(jnp.max(s, axis=-1, keepdims=True), sink)
            p = jnp.exp(s - m)
            den = jnp.sum(p, axis=-1, keepdims=True) + jnp.exp(sink - m)
            outs.append(jnp.dot(p.astype(BF16), v, preferred_element_type=F32) / den)
        o_ref[:, sl] = jnp.where(lo, outs[0], outs[1]).astype(o_ref.dtype)


def _swa(sinks, q, k, v, bias, batch, seq):
    n = q.shape[0]
    nb = seq // WINDOW
    cur = lambda b, i: (b * nb + i, 0)
    prev = lambda b, i: (b * nb + jnp.maximum(i - 1, 0), 0)
    kw = k.shape[1]
    return pl.pallas_call(
        _swa_kernel,
        grid=(batch, nb),
        in_specs=[pl.BlockSpec(memory_space=pltpu.SMEM),
                  pl.BlockSpec((WINDOW, q.shape[1]), cur),
                  pl.BlockSpec((WINDOW, kw), prev), pl.BlockSpec((WINDOW, kw), cur),
                  pl.BlockSpec((WINDOW, kw), prev), pl.BlockSpec((WINDOW, kw), cur),
                  pl.BlockSpec(bias.shape, lambda b, i: (0, 0, 0))],
        out_specs=pl.BlockSpec((WINDOW, q.shape[1]), cur),
        out_shape=jax.ShapeDtypeStruct((n, q.shape[1]), BF16),
        compiler_params=_params(("parallel", "arbitrary")),
        name="swa",
    )(sinks, q, k, k, v, v, bias)


def _merge_kernel(x_ref, ya_ref, yb_ref, yc_ref, gmix_ref, wg_ref, pa_ref, pb_ref, pc_ref, wo_ref, o_ref):
    x = x_ref[...]
    h = (x * _rms(x, D_MODEL) * gmix_ref[...]).astype(BF16)
    merged = None
    for j, (y_ref, p_ref) in enumerate(((ya_ref, pa_ref), (yb_ref, pb_ref), (yc_ref, pc_ref))):
        gate = jnp.dot(h, wg_ref[:, j * D_MODEL:(j + 1) * D_MODEL], preferred_element_type=F32)
        term = jax.nn.sigmoid(gate) * jnp.dot(y_ref[...], p_ref[...], preferred_element_type=F32)
        merged = term if merged is None else merged + term
    o_ref[...] = x + jnp.dot(merged.astype(BF16), wo_ref[...], preferred_element_type=F32)


def _merge(x2, ya, yb, yc, gmix, wg, pa, pb, pc, wo):
    n = x2.shape[0]
    tm = TM_MERGE
    row = lambda w: pl.BlockSpec((tm, w), lambda i: (i, 0))
    return pl.pallas_call(
        _merge_kernel,
        grid=(n // tm,),
        in_specs=[row(D_MODEL), row(ya.shape[1]), row(yb.shape[1]), row(yc.shape[1]),
                  _const_spec(gmix.shape), _const_spec(wg.shape), _const_spec(pa.shape),
                  _const_spec(pb.shape), _const_spec(pc.shape), _const_spec(wo.shape)],
        out_specs=row(D_MODEL),
        out_shape=jax.ShapeDtypeStruct((n, D_MODEL), F32),
        compiler_params=_params(("parallel",)),
        name="merge",
    )(x2, ya, yb, yc, gmix, wg, pa, pb, pc, wo)


def _ffn_kernel(x_ref, g_ref, wup_ref, cw_ref, cb_ref, wdn_ref, o_ref, ubuf, carry, *, tiles_per_seq):
    tm = x_ref.shape[0]
    cw2 = 2 * FF_CHUNK

    @pl.when(pl.program_id(0) % tiles_per_seq == 0)
    def _():
        carry[...] = jnp.zeros_like(carry)

    x = x_ref[...]
    h = (x * _rms(x, D_MODEL) * g_ref[...]).astype(BF16)
    out = x
    for j in range(N_FF_CHUNKS):
        sl = slice(j * cw2, (j + 1) * cw2)
        ubuf[0:HALO, :] = carry[:, sl]
        ubuf[HALO:HALO + tm, :] = jnp.dot(h, wup_ref[:, sl], preferred_element_type=F32)
        carry[:, sl] = ubuf[tm:tm + HALO, :]
        y = cb_ref[:, sl]
        for tap in range(CONV_W):
            shift = CONV_W - 1 - tap
            y = y + cw_ref[tap:tap + 1, sl] * ubuf[HALO - shift:HALO - shift + tm, :]
        gate = y[:, :FF_CHUNK]
        act = (gate * jax.nn.sigmoid(gate) * y[:, FF_CHUNK:]).astype(BF16)
        out = out + jnp.dot(act, wdn_ref[j * FF_CHUNK:(j + 1) * FF_CHUNK, :], preferred_element_type=F32)
    o_ref[...] = out


def _ffn(x2, g, wup, cw, cb, wdn, seq):
    n = x2.shape[0]
    tm = TM_FFN
    row = pl.BlockSpec((tm, D_MODEL), lambda i: (i, 0))
    return pl.pallas_call(
        functools.partial(_ffn_kernel, tiles_per_seq=seq // tm),
        grid=(n // tm,),
        in_specs=[row, _const_spec(g.shape), _const_spec(wup.shape), _const_spec(cw.shape),
                  _const_spec(cb.shape), _const_spec(wdn.shape)],
        out_specs=row,
        out_shape=jax.ShapeDtypeStruct((n, D_MODEL), F32),
        scratch_shapes=[pltpu.VMEM((HALO + tm, 2 * FF_CHUNK), F32),
                        pltpu.VMEM((HALO, 2 * D_FF), F32)],
        compiler_params=_params(("arbitrary",)),
        name="ffn",
    )(x2, g, wup, cw, cb, wdn)


def _layer_params(l, w_in, a_q_g, a_k_g, b_q_a_g, b_kv_a_g, b_w_uq, b_w_ukv, b_qn_g, b_qr_g, b_kn_g,
                  b_kr_g, c_q_g, c_k_g, w_up, conv_w, conv_b):
    offs = np.cumsum((512, 512, 512, B_Q_RANK, B_KV_RANK, B_ROPE, 512, 128, 128)).tolist()
    w = w_in[l]
    qa_w, ka_w, va_w, cq_w, ckv_w, kpe_w, qc_w, kc_w, vc_w, wg = jnp.split(w, offs, axis=1)
    z = lambda c: jnp.zeros((w.shape[0], c), w.dtype)
    r = B_ROPE // 2
    pad = LANES - B_NOPE - B_ROPE
    dup = lambda t: jnp.concatenate([t[:, :C_HD], t[:, :C_HD], t[:, C_HD:], t[:, C_HD:]], axis=1)
    w_attn = jnp.concatenate([
        qa_w, ka_w, va_w, cq_w, ckv_w,
        z(B_NOPE), kpe_w, z(pad),
        z(B_NOPE), kpe_w[:, r:], kpe_w[:, :r], z(pad),
        qc_w, dup(kc_w), dup(vc_w)], axis=1).astype(BF16)

    uq = b_w_uq[l].reshape(B_Q_RANK, B_HEADS, B_NOPE + B_ROPE)
    nope, pe = uq[..., :B_NOPE], uq[..., B_NOPE:]
    zq = lambda c: jnp.zeros((B_Q_RANK, B_HEADS, c), uq.dtype)
    wuq = jnp.concatenate([
        jnp.concatenate([nope, pe, zq(pad)], axis=-1).reshape(B_Q_RANK, B_HEADS * LANES),
        jnp.concatenate([zq(B_NOPE), pe[..., r:], pe[..., :r], zq(pad)], axis=-1).reshape(B_Q_RANK, B_HEADS * LANES),
    ], axis=1).astype(BF16)

    ukv = b_w_ukv[l].reshape(B_KV_RANK, B_HEADS, B_NOPE + B_VD)
    zk = jnp.zeros((B_KV_RANK, B_HEADS, LANES - B_NOPE), ukv.dtype)
    wukv = jnp.concatenate([
        jnp.concatenate([ukv[..., :B_NOPE], zk], axis=-1).reshape(B_KV_RANK, B_HEADS * LANES),
        ukv[..., B_NOPE:].reshape(B_KV_RANK, B_HEADS * B_VD)], axis=1).astype(BF16)

    scale_a = A_HD ** -0.5
    scale_b = (B_NOPE + B_ROPE) ** -0.5
    scale_c = C_HD ** -0.5
    ga = jnp.stack([jnp.tile(a_q_g[l], 2 * A_HEADS) * scale_a, jnp.tile(a_k_g[l], 2 * A_HEADS)])
    glat = jnp.concatenate([b_q_a_g[l], b_kv_a_g[l]])[None, :]
    zl = lambda c: jnp.zeros((c,), F32)
    qr, kr = b_qr_g[l], b_kr_g[l]
    gqb = jnp.stack([
        jnp.tile(jnp.concatenate([b_qn_g[l], qr, zl(pad)]), B_HEADS),
        jnp.tile(jnp.concatenate([zl(B_NOPE), qr[r:], qr[:r], zl(pad)]), B_HEADS)]) * scale_b
    gkb = jnp.stack([
        jnp.concatenate([b_kn_g[l], zl(LANES - B_NOPE)]),
        jnp.concatenate([zl(B_NOPE), kr, zl(pad)]),
        jnp.concatenate([zl(B_NOPE), kr[r:], kr[:r], zl(pad)])])
    gc = jnp.concatenate([jnp.tile(c_q_g[l], C_HEADS) * scale_c, jnp.tile(c_k_g[l], 2 * C_KV_HEADS)])[None, :]

    def chunked(t):
        g, u = t[..., :D_FF], t[..., D_FF:]
        lead = t.shape[:-1]
        g = g.reshape(lead + (N_FF_CHUNKS, FF_CHUNK))
        u = u.reshape(lead + (N_FF_CHUNKS, FF_CHUNK))
        return jnp.concatenate([g, u], axis=-1).reshape(lead + (2 * D_FF,))

    wup = chunked(w_up[l]).astype(BF16)
    cw = chunked(conv_w[l, :, 0, :])
    cb = chunked(conv_b[l])[None, :]
    return dict(w_attn=w_attn, wg=wg.astype(BF16), wuq=wuq, wukv=wukv, ga=ga, glat=glat, gqb=gqb,
                gkb=gkb, gc=gc, wup=wup, cw=cw, cb=cb)


def kernel(x, positions, rel_bias_table, ln_mix_g, w_in, a_q_g, a_k_g, a_lam_q1, a_lam_k1, a_lam_q2, a_lam_k2, a_subln_g, b_q_a_g, b_kv_a_g, b_w_uq, b_w_ukv, b_qn_g, b_qr_g, b_kn_g, b_kr_g, c_q_g, c_k_g, c_sinks, p_a, p_b, p_c, w_o, ln_ffn_g, w_up, conv_w, conv_b, w_down):
    batch, seq, d = x.shape
    n = batch * seq
    assert d == D_MODEL and seq % T_ATT == 0 and n % TM_IN == 0 and seq % TM_FFN == 0
    x2 = x.reshape(n, d)
    cos, sin = _rope_tables(positions.reshape(n, 1))
    bias_a, bias_c = _bias_tiles(rel_bias_table)
    for l in range(DEPTH):
        lambda_init = 0.8 - 0.6 * math.exp(-0.3 * l)
        p = _layer_params(l, w_in, a_q_g, a_k_g, b_q_a_g, b_kv_a_g, b_w_uq, b_w_ukv, b_qn_g, b_qr_g,
                          b_kn_g, b_kr_g, c_q_g, c_k_g, w_up, conv_w, conv_b)
        gmix = ln_mix_g[l][None, :]
        qa, ka, va, qb, kb, vb, qc, kc, vc = _in_proj(
            x2, cos, sin, gmix, p["ga"], p["glat"], p["gqb"], p["gkb"], p["gc"],
            p["w_attn"], p["wuq"], p["wukv"])
        lam = jnp.stack([a_lam_q1[l], a_lam_k1[l], a_lam_q2[l], a_lam_k2[l]])
        ya = _flash(qa, ka, va, batch, seq, "diff", extra=(bias_a, lam, a_subln_g[l][None, :]),
                    lambda_init=lambda_init)
        yb = _flash(qb, kb, vb, batch, seq, "mla")
        yc = _swa(c_sinks[l], qc, kc, vc, bias_c, batch, seq)
        x2 = _merge(x2, ya, yb, yc, gmix, p["wg"], p_a[l].astype(BF16), p_b[l].astype(BF16),
                    p_c[l].astype(BF16), w_o[l].astype(BF16))
        x2 = _ffn(x2, ln_ffn_g[l][None, :], p["wup"], p["cw"], p["cb"], w_down[l].astype(BF16), seq)
    return x2.reshape(batch, seq, d)
```

```python
import functools
import math

import jax
import jax.numpy as jnp
import numpy as np
from jax import lax
from jax.experimental import pallas as pl
from jax.experimental.pallas import tpu as pltpu

F32 = jnp.float32
BF16 = jnp.bfloat16

D_MODEL = 1024
DEPTH = 2
EPS = 1e-6
A_HEADS = 4
A_HD = 64
A_VD = 2 * A_HD
B_HEADS = 8
B_Q_RANK = 256
B_KV_RANK = 128
B_NOPE = 64
B_ROPE = 32
B_VD = 64
ROPE_THETA = 10000.0
C_HEADS = 8
C_KV_HEADS = 2
C_HD = 64
WINDOW = 128
N_BUCKETS = 32
MAX_DIST = 128
D_FF = 2816
CONV_W = 3

LANES = 128
HALF = LANES // 2
NEG = -1e30
LOG2E = math.log2(math.e)
ROW_CHUNK = 32
VMEM_LIMIT = 56 * 1024 * 1024

T_ATT = 512
TM_IN = 512
TM_MERGE = 512
TM_FFN = 512
FF_CHUNK = 256
N_FF_CHUNKS = D_FF // FF_CHUNK
HALO = 8

_SEG = {}
_off = 0
for _name, _w in (("qa", 512), ("ka", 512), ("va", 512), ("cq", 256), ("ckv", 128), ("kpe", 128),
                  ("kpe_sw", 128), ("qc", 512), ("kc", 256), ("vc", 256)):
    _SEG[_name] = (_off, _off + _w)
    _off += _w
W_ATTN_COLS = _off


def _params(sem, vmem=VMEM_LIMIT):
    return pltpu.CompilerParams(dimension_semantics=sem, vmem_limit_bytes=vmem)


def _const_spec(shape):
    nd = len(shape)
    return pl.BlockSpec(shape, lambda *_: (0,) * nd, pipeline_mode=pl.Buffered(1))


def _lane_lo():
    return lax.broadcasted_iota(jnp.int32, (1, LANES), 1) < HALF


def _rope_kernel(pos_ref, inv_ref, sign_ref, cos_ref, sin_ref):
    ang = pos_ref[...].astype(F32) * inv_ref[...]
    cos_ref[...] = jnp.cos(ang)
    sin_ref[...] = jnp.sin(ang) * sign_ref[...]


def _rope_tables(pos_col):
    n = pos_col.shape[0]
    inv = 1.0 / (ROPE_THETA ** (jnp.arange(0, B_ROPE, 2, dtype=F32) / B_ROPE))
    z = jnp.zeros((B_NOPE,), F32)
    zp = jnp.zeros((LANES - B_NOPE - B_ROPE,), F32)
    inv_pat = jnp.concatenate([z, inv, inv, zp])[None, :]
    ones = jnp.ones((B_ROPE // 2,), F32)
    sign_pat = jnp.concatenate([z, -ones, ones, zp])[None, :]
    tm = 2048
    return pl.pallas_call(
        _rope_kernel,
        grid=(n // tm,),
        in_specs=[pl.BlockSpec((tm, 1), lambda i: (i, 0)),
                  pl.BlockSpec((1, LANES), lambda i: (0, 0)),
                  pl.BlockSpec((1, LANES), lambda i: (0, 0))],
        out_specs=[pl.BlockSpec((tm, LANES), lambda i: (i, 0))] * 2,
        out_shape=[jax.ShapeDtypeStruct((n, LANES), F32)] * 2,
        compiler_params=_params(("parallel",)),
        name="rope_tables",
    )(pos_col, inv_pat, sign_pat)


def _bucket(rel):
    n = jnp.maximum(rel, 0)
    max_exact = N_BUCKETS // 2
    nf = jnp.maximum(n, 1).astype(F32)
    large = max_exact + (jnp.log(nf / max_exact) / math.log(MAX_DIST / max_exact)
                         * (N_BUCKETS - max_exact)).astype(jnp.int32)
    large = jnp.minimum(large, N_BUCKETS - 1)
    return jnp.where(n < max_exact, n, large)


def _lookup(tab_ref, bucket, col):
    out = jnp.zeros(bucket.shape, F32)
    for k in range(N_BUCKETS):
        out = jnp.where(bucket == k, tab_ref[k, col], out)
    return out


def _bias_a_kernel(tab_ref, out_ref):
    h = pl.program_id(0)
    t = out_ref.shape[-1]
    key = lax.broadcasted_iota(jnp.int32, (t, t), 0)
    qry = lax.broadcasted_iota(jnp.int32, (t, t), 1)
    far = tab_ref[N_BUCKETS - 1, h]
    rel = qry - key
    out_ref[0, 0] = jnp.where(rel >= 0, (_lookup(tab_ref, _bucket(rel), h) - far) * LOG2E, NEG)
    out_ref[0, 1] = (_lookup(tab_ref, _bucket(rel + t), h) - far) * LOG2E


def _bias_c_kernel(tab_ref, out_ref):
    h = pl.program_id(0)
    row = lax.broadcasted_iota(jnp.int32, (WINDOW, 2 * WINDOW), 0)
    col = lax.broadcasted_iota(jnp.int32, (WINDOW, 2 * WINDOW), 1)
    rel = row + WINDOW - col
    valid = (rel >= 0) & (rel < WINDOW)
    out_ref[0] = jnp.where(valid, _lookup(tab_ref, _bucket(rel), h + A_HEADS), NEG)


def _bias_tiles(table):
    smem = pl.BlockSpec(memory_space=pltpu.SMEM)
    bias_a = pl.pallas_call(
        _bias_a_kernel,
        grid=(A_HEADS,),
        in_specs=[smem],
        out_specs=pl.BlockSpec((1, 2, T_ATT, T_ATT), lambda h: (h, 0, 0, 0)),
        out_shape=jax.ShapeDtypeStruct((A_HEADS, 2, T_ATT, T_ATT), F32),
        compiler_params=_params(("parallel",)),
        name="bias_a",
    )(table)
    bias_c = pl.pallas_call(
        _bias_c_kernel,
        grid=(C_HEADS,),
        in_specs=[smem],
        out_specs=pl.BlockSpec((1, WINDOW, 2 * WINDOW), lambda h: (h, 0, 0)),
        out_shape=jax.ShapeDtypeStruct((C_HEADS, WINDOW, 2 * WINDOW), F32),
        compiler_params=_params(("parallel",)),
        name="bias_c",
    )(table)
    return bias_a, bias_c


def _rms(t, width):
    return lax.rsqrt(jnp.sum(t * t, axis=-1, keepdims=True) / width + EPS)


def _in_kernel(x_ref, cos_ref, sin_ref, gmix_ref, ga_ref, glat_ref, gqb_ref, gkb_ref, gc_ref,
               w_ref, wuq_ref, wukv_ref,
               qa_ref, ka_ref, va_ref, qb_ref, kb_ref, vb_ref, qc_ref, kc_ref, vc_ref):
    x = x_ref[...]
    h = (x * _rms(x, D_MODEL) * gmix_ref[...]).astype(BF16)
    lo = _lane_lo()

    def proj(name):
        a, b = _SEG[name]
        return jnp.dot(h, w_ref[:, a:b], preferred_element_type=F32)

    def norm_halves(t, g, out_ref):
        for j in range(t.shape[1] // LANES):
            sl = slice(j * LANES, (j + 1) * LANES)
            tj = t[:, sl]
            sq = tj * tj
            s_lo = jnp.sum(jnp.where(lo, sq, 0.0), axis=-1, keepdims=True)
            s_hi = jnp.sum(jnp.where(lo, 0.0, sq), axis=-1, keepdims=True)
            r = jnp.where(lo, lax.rsqrt(s_lo / HALF + EPS), lax.rsqrt(s_hi / HALF + EPS))
            out_ref[:, sl] = (tj * r * g[:, sl]).astype(out_ref.dtype)

    norm_halves(proj("qa"), ga_ref[0:1, :], qa_ref)
    norm_halves(proj("ka"), ga_ref[1:2, :], ka_ref)
    va_ref[...] = proj("va").T.astype(BF16)
    gc = gc_ref[...]
    norm_halves(proj("qc"), gc[:, :C_HEADS * C_HD], qc_ref)
    norm_halves(proj("kc"), gc[:, C_HEADS * C_HD:], kc_ref)
    vc_ref[...] = proj("vc").astype(BF16)

    cos = cos_ref[...]
    sin = sin_ref[...]
    glat = glat_ref[...]

    cq = proj("cq")
    cqn = (cq * _rms(cq, B_Q_RANK) * glat[:, :B_Q_RANK]).astype(BF16)
    uq = jnp.dot(cqn, wuq_ref[...], preferred_element_type=F32)
    hw = B_HEADS * LANES
    for j in range(B_HEADS):
        sl = slice(j * LANES, (j + 1) * LANES)
        raw = uq[:, sl]
        raw_sw = uq[:, hw + j * LANES: hw + (j + 1) * LANES]
        sq = raw * raw
        s_n = jnp.sum(jnp.where(lo, sq, 0.0), axis=-1, keepdims=True)
        s_r = jnp.sum(jnp.where(lo, 0.0, sq), axis=-1, keepdims=True)
        r = jnp.where(lo, lax.rsqrt(s_n / B_NOPE + EPS), lax.rsqrt(s_r / B_ROPE + EPS))
        out = r * (raw * gqb_ref[0:1, sl] * cos + raw_sw * gqb_ref[1:2, sl] * sin)
        qb_ref[:, sl] = out.astype(BF16)

    ckv = proj("ckv")
    ckvn = (ckv * _rms(ckv, B_KV_RANK) * glat[:, B_Q_RANK:]).astype(BF16)
    ukv = jnp.dot(ckvn, wukv_ref[...], preferred_element_type=F32)
    vb_ref[...] = ukv[:, hw:].T.astype(BF16)
    kpe = proj("kpe")
    kpe_sw = proj("kpe_sw")
    kpe_out = _rms(kpe, B_ROPE) * (kpe * gkb_ref[1:2, :] * cos + kpe_sw * gkb_ref[2:3, :] * sin)
    for j in range(B_HEADS):
        sl = slice(j * LANES, (j + 1) * LANES)
        raw = ukv[:, sl]
        kb_ref[:, sl] = (raw * _rms(raw, B_NOPE) * gkb_ref[0:1, :] + kpe_out).astype(BF16)


def _in_proj(x2, cos, sin, gmix, ga, glat, gqb, gkb, gc, w_attn, wuq, wukv, batch, seq):
    n = x2.shape[0]
    tm = TM_IN
    nps = seq // tm
    row = lambda w: pl.BlockSpec((tm, w), lambda i: (i, 0))
    outs = (("qa", 512), ("ka", 512), ("va", None), ("qb", 1024), ("kb", 1024), ("vb", None),
            ("qc", 512), ("kc", 256), ("vc", 256))
    vt_spec = pl.BlockSpec((None, 512, tm), lambda i: (i // nps, 0, i % nps))
    vt_shape = jax.ShapeDtypeStruct((batch, 512, seq), BF16)
    return pl.pallas_call(
        _in_kernel,
        grid=(n // tm,),
        in_specs=[row(D_MODEL), row(LANES), row(LANES),
                  _const_spec(gmix.shape), _const_spec(ga.shape), _const_spec(glat.shape),
                  _const_spec(gqb.shape), _const_spec(gkb.shape), _const_spec(gc.shape),
                  _const_spec(w_attn.shape), _const_spec(wuq.shape), _const_spec(wukv.shape)],
        out_specs=[vt_spec if w is None else row(w) for _, w in outs],
        out_shape=[vt_shape if w is None else jax.ShapeDtypeStruct((n, w), BF16) for _, w in outs],
        compiler_params=_params(("parallel",)),
        name="in_proj",
    )(x2, cos, sin, gmix, ga, glat, gqb, gkb, gc, w_attn, wuq, wukv)


def _flash_kernel(*refs, mode, lambda_init):
    if mode == "diff":
        (q_ref, k_ref, vt_ref, bias_ref, lam_ref, subg_ref, o_ref,
         s_scr, p_scr, acc_scr, m_scr, l_scr, mt_scr) = refs
    else:
        q_ref, k_ref, vt_ref, o_ref, s_scr, p_scr, acc_scr, m_scr, l_scr, mt_scr = refs
    t = q_ref.shape[0]
    qi = pl.program_id(2)
    q = q_ref[...]
    if mode == "diff":
        lo = _lane_lo()
        zero = jnp.zeros_like(q)
        qs = (jnp.where(lo, q, zero), jnp.where(lo, zero, q))
    else:
        qs = (q[:, :LANES], q[:, LANES:])
    acc_scr[...] = jnp.zeros_like(acc_scr)
    m_scr[...] = jnp.full(m_scr.shape, NEG, F32)
    l_scr[...] = jnp.zeros_like(l_scr)

    def qk(ki, slot):
        start = pl.multiple_of(ki * t, t)
        k = k_ref[pl.ds(start, t), :]
        for j in range(2):
            kj = k if mode == "diff" else k[:, j * LANES:(j + 1) * LANES]
            s = lax.dot_general(kj, qs[j], (((1,), (1,)), ((), ())), preferred_element_type=F32)
            s_scr[2 * slot + j] = s
            mt_scr[2 * slot + j] = jnp.max(s, axis=0, keepdims=True)

    def softmax_pv(ki, slot, kind):
        start = pl.multiple_of(ki * t, t)
        vt = vt_ref[:, pl.ds(start, t)]
        chunks = range(0, t, ROW_CHUNK)
        if mode != "diff" and kind == "prev":
            kind = "far"
        for j in range(2):
            c = 2 * slot + j
            if kind == "far":
                mc = mt_scr[c]
            else:
                tops = []
                for r0 in chunks:
                    x = s_scr[c, r0:r0 + ROW_CHUNK, :]
                    if mode == "diff":
                        x = x + bias_ref[0, 0 if kind == "diag" else 1, r0:r0 + ROW_CHUNK, :]
                    else:
                        key = lax.broadcasted_iota(jnp.int32, (ROW_CHUNK, t), 0) + r0
                        qry = lax.broadcasted_iota(jnp.int32, (ROW_CHUNK, t), 1)
                        x = jnp.where(key <= qry, x, NEG)
                    s_scr[c, r0:r0 + ROW_CHUNK, :] = x
                    top = x[0:8, :]
                    for r in range(8, ROW_CHUNK, 8):
                        top = jnp.maximum(top, x[r:r + 8, :])
                    tops.append(top)
                while len(tops) > 1:
                    tops = [jnp.maximum(a, b) for a, b in zip(tops[0::2], tops[1::2])]
                mc = jnp.max(tops[0], axis=0, keepdims=True)
            m_old = m_scr[j]
            m_new = jnp.maximum(m_old, mc)
            alpha = jnp.exp2(m_old - m_new)
            parts = []
            for r0 in chunks:
                p = jnp.exp2(s_scr[c, r0:r0 + ROW_CHUNK, :] - m_new)
                p_scr[j, r0:r0 + ROW_CHUNK, :] = p.astype(BF16)
                part = p[0:8, :]
                for r in range(8, ROW_CHUNK, 8):
                    part = part + p[r:r + 8, :]
                parts.append(part)
            while len(parts) > 1:
                parts = [a + b for a, b in zip(parts[0::2], parts[1::2])]
            m_scr[j] = m_new
            l_scr[j] = alpha * l_scr[j] + jnp.sum(parts[0], axis=0, keepdims=True)
            acc_scr[j] = alpha * acc_scr[j] + jnp.dot(vt, p_scr[j], preferred_element_type=F32)

    n_far = jnp.maximum(qi - 1, 0)
    qk(0, 0)

    def far_pair(jj, carry):
        qk(2 * jj + 1, 1)
        softmax_pv(2 * jj, 0, "far")
        qk(2 * jj + 2, 0)
        softmax_pv(2 * jj + 1, 1, "far")
        return carry

    lax.fori_loop(0, n_far // 2, far_pair, 0)

    @pl.when(qi == 0)
    def _():
        softmax_pv(qi, 0, "diag")

    @pl.when((qi > 0) & (n_far % 2 == 0))
    def _():
        qk(qi, 1)
        softmax_pv(qi - 1, 0, "prev")
        softmax_pv(qi, 1, "diag")

    @pl.when(n_far % 2 == 1)
    def _():
        qk(qi - 1, 1)
        softmax_pv(qi - 2, 0, "far")
        qk(qi, 0)
        softmax_pv(qi - 1, 1, "prev")
        softmax_pv(qi, 0, "diag")

    o0 = acc_scr[0] / l_scr[0]
    o1 = acc_scr[1] / l_scr[1]
    if mode == "diff":
        lv = lam_ref[...]
        lam = (jnp.exp(jnp.sum(lv[0:1] * lv[1:2], axis=-1, keepdims=True))
               - jnp.exp(jnp.sum(lv[2:3] * lv[3:4], axis=-1, keepdims=True)) + lambda_init)
        o = o0 - lam * o1
        o = o * lax.rsqrt(jnp.sum(o * o, axis=0, keepdims=True) / A_VD + EPS)
        o = o.T * subg_ref[...] * (1.0 - lambda_init)
    else:
        upper = lax.broadcasted_iota(jnp.int32, (LANES, 1), 0) < HALF
        o = jnp.where(upper, o0, o1).T
    o_ref[...] = o.astype(o_ref.dtype)


def _flash(q, k, vt, batch, seq, mode, extra=(), lambda_init=0.0):
    n = q.shape[0]
    t = T_ATT
    nq = seq // t
    qw = LANES if mode == "diff" else 2 * LANES
    groups = q.shape[1] // qw
    in_specs = [pl.BlockSpec((t, qw), lambda b, g, i: (b * nq + i, g)),
                pl.BlockSpec((seq, qw), lambda b, g, i: (b, g)),
                pl.BlockSpec((None, LANES, seq), lambda b, g, i: (b, g, 0))]
    if mode == "diff":
        bias, lam, subg = extra
        in_specs += [pl.BlockSpec((1, 2, t, t), lambda b, g, i: (g, 0, 0, 0)),
                     pl.BlockSpec(lam.shape, lambda b, g, i: (0, 0)),
                     pl.BlockSpec(subg.shape, lambda b, g, i: (0, 0))]
    return pl.pallas_call(
        functools.partial(_flash_kernel, mode=mode, lambda_init=lambda_init),
        grid=(batch, groups, nq),
        in_specs=in_specs,
        out_specs=pl.BlockSpec((t, LANES), lambda b, g, i: (b * nq + i, g)),
        out_shape=jax.ShapeDtypeStruct((n, groups * LANES), BF16),
        scratch_shapes=[pltpu.VMEM((4, t, t), F32), pltpu.VMEM((2, t, t), BF16),
                        pltpu.VMEM((2, LANES, t), F32), pltpu.VMEM((2, 1, t), F32),
                        pltpu.VMEM((2, 1, t), F32), pltpu.VMEM((4, 1, t), F32)],
        compiler_params=_params(("parallel", "parallel", "arbitrary")),
        name="flash_" + mode,
    )(q, k, vt, *extra)


def _swa_kernel(sink_ref, q_ref, kp_ref, kc_ref, vp_ref, vc_ref, bias_ref, o_ref):
    nb = pl.program_id(1)
    lo = _lane_lo()
    col = lax.broadcasted_iota(jnp.int32, (WINDOW, 2 * WINDOW), 1)
    keep = (col >= WINDOW) | (nb > 0)
    grp = C_HEADS // C_KV_HEADS
    for hp in range(C_HEADS // 2):
        kv = (2 * hp) // grp
        sl = slice(hp * LANES, (hp + 1) * LANES)
        ksl = slice(kv * LANES, (kv + 1) * LANES)
        q = q_ref[:, sl]
        zero = jnp.zeros_like(q)
        k = jnp.concatenate([kp_ref[:, ksl], kc_ref[:, ksl]], axis=0)
        v = jnp.concatenate([vp_ref[:, ksl], vc_ref[:, ksl]], axis=0)
        outs = []
        for j in range(2):
            head = 2 * hp + j
            qj = jnp.where(lo, q, zero) if j == 0 else jnp.where(lo, zero, q)
            s = lax.dot_general(qj, k, (((1,), (1,)), ((), ())), preferred_element_type=F32)
            s = jnp.where(keep, s + bias_ref[head], NEG)
            sink = sink_ref[head]
            m = jnp.maximum(jnp.max(s, axis=-1, keepdims=True), sink)
            p = jnp.exp(s - m)
            den = jnp.sum(p, axis=-1, keepdims=True) + jnp.exp(sink - m)
            outs.append(jnp.dot(p.astype(BF16), v, preferred_element_type=F32) / den)
        o_ref[:, sl] = jnp.where(lo, outs[0], outs[1]).astype(o_ref.dtype)


def _swa(sinks, q, k, v, bias, batch, seq):
    n = q.shape[0]
    nb = seq // WINDOW
    cur = lambda b, i: (b * nb + i, 0)
    prev = lambda b, i: (b * nb + jnp.maximum(i - 1, 0), 0)
    kw = k.shape[1]
    return pl.pallas_call(
        _swa_kernel,
        grid=(batch, nb),
        in_specs=[pl.BlockSpec(memory_space=pltpu.SMEM),
                  pl.BlockSpec((WINDOW, q.shape[1]), cur),
                  pl.BlockSpec((WINDOW, kw), prev), pl.BlockSpec((WINDOW, kw), cur),
                  pl.BlockSpec((WINDOW, kw), prev), pl.BlockSpec((WINDOW, kw), cur),
                  pl.BlockSpec(bias.shape, lambda b, i: (0, 0, 0))],
        out_specs=pl.BlockSpec((WINDOW, q.shape[1]), cur),
        out_shape=jax.ShapeDtypeStruct((n, q.shape[1]), BF16),
        compiler_params=_params(("parallel", "arbitrary")),
        name="swa",
    )(sinks, q, k, k, v, v, bias)


def _merge_kernel(x_ref, ya_ref, yb_ref, yc_ref, gmix_ref, wg_ref, pa_ref, pb_ref, pc_ref, wo_ref, o_ref):
    x = x_ref[...]
    h = (x * _rms(x, D_MODEL) * gmix_ref[...]).astype(BF16)
    merged = None
    for j, (y_ref, p_ref) in enumerate(((ya_ref, pa_ref), (yb_ref, pb_ref), (yc_ref, pc_ref))):
        gate = jnp.dot(h, wg_ref[:, j * D_MODEL:(j + 1) * D_MODEL], preferred_element_type=F32)
        term = jax.nn.sigmoid(gate) * jnp.dot(y_ref[...], p_ref[...], preferred_element_type=F32)
        merged = term if merged is None else merged + term
    o_ref[...] = x + jnp.dot(merged.astype(BF16), wo_ref[...], preferred_element_type=F32)


def _merge(x2, ya, yb, yc, gmix, wg, pa, pb, pc, wo):
    n = x2.shape[0]
    tm = TM_MERGE
    row = lambda w: pl.BlockSpec((tm, w), lambda i: (i, 0))
    return pl.pallas_call(
        _merge_kernel,
        grid=(n // tm,),
        in_specs=[row(D_MODEL), row(ya.shape[1]), row(yb.shape[1]), row(yc.shape[1]),
                  _const_spec(gmix.shape), _const_spec(wg.shape), _const_spec(pa.shape),
                  _const_spec(pb.shape), _const_spec(pc.shape), _const_spec(wo.shape)],
        out_specs=row(D_MODEL),
        out_shape=jax.ShapeDtypeStruct((n, D_MODEL), F32),
        compiler_params=_params(("parallel",)),
        name="merge",
    )(x2, ya, yb, yc, gmix, wg, pa, pb, pc, wo)


def _ffn_kernel(x_ref, g_ref, wup_ref, cw_ref, cb_ref, wdn_ref, o_ref, ubuf, carry, *, tiles_per_seq):
    tm = x_ref.shape[0]
    cw2 = 2 * FF_CHUNK

    @pl.when(pl.program_id(0) % tiles_per_seq == 0)
    def _():
        carry[...] = jnp.zeros_like(carry)

    x = x_ref[...]
    h = (x * _rms(x, D_MODEL) * g_ref[...]).astype(BF16)
    out = x
    for j in range(N_FF_CHUNKS):
        sl = slice(j * cw2, (j + 1) * cw2)
        ubuf[0:HALO, :] = carry[:, sl]
        ubuf[HALO:HALO + tm, :] = jnp.dot(h, wup_ref[:, sl], preferred_element_type=F32)
        carry[:, sl] = ubuf[tm:tm + HALO, :]
        y = cb_ref[:, sl]
        for tap in range(CONV_W):
            shift = CONV_W - 1 - tap
            y = y + cw_ref[tap:tap + 1, sl] * ubuf[HALO - shift:HALO - shift + tm, :]
        gate = y[:, :FF_CHUNK]
        act = (gate * jax.nn.sigmoid(gate) * y[:, FF_CHUNK:]).astype(BF16)
        out = out + jnp.dot(act, wdn_ref[j * FF_CHUNK:(j + 1) * FF_CHUNK, :], preferred_element_type=F32)
    o_ref[...] = out


def _ffn(x2, g, wup, cw, cb, wdn, seq):
    n = x2.shape[0]
    tm = TM_FFN
    row = pl.BlockSpec((tm, D_MODEL), lambda i: (i, 0))
    return pl.pallas_call(
        functools.partial(_ffn_kernel, tiles_per_seq=seq // tm),
        grid=(n // tm,),
        in_specs=[row, _const_spec(g.shape), _const_spec(wup.shape), _const_spec(cw.shape),
                  _const_spec(cb.shape), _const_spec(wdn.shape)],
        out_specs=row,
        out_shape=jax.ShapeDtypeStruct((n, D_MODEL), F32),
        scratch_shapes=[pltpu.VMEM((HALO + tm, 2 * FF_CHUNK), F32),
                        pltpu.VMEM((HALO, 2 * D_FF), F32)],
        compiler_params=_params(("arbitrary",)),
        name="ffn",
    )(x2, g, wup, cw, cb, wdn)


def _layer_params(l, w_in, a_q_g, a_k_g, b_q_a_g, b_kv_a_g, b_w_uq, b_w_ukv, b_qn_g, b_qr_g, b_kn_g,
                  b_kr_g, c_q_g, c_k_g, w_up, conv_w, conv_b):
    offs = np.cumsum((512, 512, 512, B_Q_RANK, B_KV_RANK, B_ROPE, 512, 128, 128)).tolist()
    w = w_in[l]
    qa_w, ka_w, va_w, cq_w, ckv_w, kpe_w, qc_w, kc_w, vc_w, wg = jnp.split(w, offs, axis=1)
    z = lambda c: jnp.zeros((w.shape[0], c), w.dtype)
    r = B_ROPE // 2
    pad = LANES - B_NOPE - B_ROPE
    dup = lambda t: jnp.concatenate([t[:, :C_HD], t[:, :C_HD], t[:, C_HD:], t[:, C_HD:]], axis=1)
    w_attn = jnp.concatenate([
        qa_w, ka_w, va_w, cq_w, ckv_w,
        z(B_NOPE), kpe_w, z(pad),
        z(B_NOPE), kpe_w[:, r:], kpe_w[:, :r], z(pad),
        qc_w, dup(kc_w), dup(vc_w)], axis=1).astype(BF16)

    uq = b_w_uq[l].reshape(B_Q_RANK, B_HEADS, B_NOPE + B_ROPE)
    nope, pe = uq[..., :B_NOPE], uq[..., B_NOPE:]
    zq = lambda c: jnp.zeros((B_Q_RANK, B_HEADS, c), uq.dtype)
    wuq = jnp.concatenate([
        jnp.concatenate([nope, pe, zq(pad)], axis=-1).reshape(B_Q_RANK, B_HEADS * LANES),
        jnp.concatenate([zq(B_NOPE), pe[..., r:], pe[..., :r], zq(pad)], axis=-1).reshape(B_Q_RANK, B_HEADS * LANES),
    ], axis=1).astype(BF16)

    ukv = b_w_ukv[l].reshape(B_KV_RANK, B_HEADS, B_NOPE + B_VD)
    zk = jnp.zeros((B_KV_RANK, B_HEADS, LANES - B_NOPE), ukv.dtype)
    wukv = jnp.concatenate([
        jnp.concatenate([ukv[..., :B_NOPE], zk], axis=-1).reshape(B_KV_RANK, B_HEADS * LANES),
        ukv[..., B_NOPE:].reshape(B_KV_RANK, B_HEADS * B_VD)], axis=1).astype(BF16)

    scale_a = A_HD ** -0.5 * LOG2E
    scale_b = (B_NOPE + B_ROPE) ** -0.5 * LOG2E
    scale_c = C_HD ** -0.5
    ga = jnp.stack([jnp.tile(a_q_g[l], 2 * A_HEADS) * scale_a, jnp.tile(a_k_g[l], 2 * A_HEADS)])
    glat = jnp.concatenate([b_q_a_g[l], b_kv_a_g[l]])[None, :]
    zl = lambda c: jnp.zeros((c,), F32)
    qr, kr = b_qr_g[l], b_kr_g[l]
    gqb = jnp.stack([
        jnp.tile(jnp.concatenate([b_qn_g[l], qr, zl(pad)]), B_HEADS),
        jnp.tile(jnp.concatenate([zl(B_NOPE), qr[r:], qr[:r], zl(pad)]), B_HEADS)]) * scale_b
    gkb = jnp.stack([
        jnp.concatenate([b_kn_g[l], zl(LANES - B_NOPE)]),
        jnp.concatenate([zl(B_NOPE), kr, zl(pad)]),
        jnp.concatenate([zl(B_NOPE), kr[r:], kr[:r], zl(pad)])])
    gc = jnp.concatenate([jnp.tile(c_q_g[l], C_HEADS) * scale_c, jnp.tile(c_k_g[l], 2 * C_KV_HEADS)])[None, :]

    def chunked(t):
        g, u = t[..., :D_FF], t[..., D_FF:]
        lead = t.shape[:-1]
        g = g.reshape(lead + (N_FF_CHUNKS, FF_CHUNK))
        u = u.reshape(lead + (N_FF_CHUNKS, FF_CHUNK))
        return jnp.concatenate([g, u], axis=-1).reshape(lead + (2 * D_FF,))

    wup = chunked(w_up[l]).astype(BF16)
    cw = chunked(conv_w[l, :, 0, :])
    cb = chunked(conv_b[l])[None, :]
    return dict(w_attn=w_attn, wg=wg.astype(BF16), wuq=wuq, wukv=wukv, ga=ga, glat=glat, gqb=gqb,
                gkb=gkb, gc=gc, wup=wup, cw=cw, cb=cb)


def kernel(x, positions, rel_bias_table, ln_mix_g, w_in, a_q_g, a_k_g, a_lam_q1, a_lam_k1, a_lam_q2, a_lam_k2, a_subln_g, b_q_a_g, b_kv_a_g, b_w_uq, b_w_ukv, b_qn_g, b_qr_g, b_kn_g, b_kr_g, c_q_g, c_k_g, c_sinks, p_a, p_b, p_c, w_o, ln_ffn_g, w_up, conv_w, conv_b, w_down):
    batch, seq, d = x.shape
    n = batch * seq
    assert d == D_MODEL and seq % T_ATT == 0 and n % TM_IN == 0 and seq % TM_FFN == 0
    x2 = x.reshape(n, d)
    cos, sin = _rope_tables(positions.reshape(n, 1))
    bias_a, bias_c = _bias_tiles(rel_bias_table)
    for l in range(DEPTH):
        lambda_init = 0.8 - 0.6 * math.exp(-0.3 * l)
        p = _layer_params(l, w_in, a_q_g, a_k_g, b_q_a_g, b_kv_a_g, b_w_uq, b_w_ukv, b_qn_g, b_qr_g,
                          b_kn_g, b_kr_g, c_q_g, c_k_g, w_up, conv_w, conv_b)
        gmix = ln_mix_g[l][None, :]
        qa, ka, va, qb, kb, vb, qc, kc, vc = _in_proj(
            x2, cos, sin, gmix, p["ga"], p["glat"], p["gqb"], p["gkb"], p["gc"],
            p["w_attn"], p["wuq"], p["wukv"], batch, seq)
        lam = jnp.stack([a_lam_q1[l], a_lam_k1[l], a_lam_q2[l], a_lam_k2[l]])
        ya = _flash(qa, ka, va, batch, seq, "diff", extra=(bias_a, lam, a_subln_g[l][None, :]),
                    lambda_init=lambda_init)
        yb = _flash(qb, kb, vb, batch, seq, "mla")
        yc = _swa(c_sinks[l], qc, kc, vc, bias_c, batch, seq)
        x2 = _merge(x2, ya, yb, yc, gmix, p["wg"], p_a[l].astype(BF16), p_b[l].astype(BF16),
                    p_c[l].astype(BF16), w_o[l].astype(BF16))
        x2 = _ffn(x2, ln_ffn_g[l][None, :], p["wup"], p["cw"], p["cb"], w_down[l].astype(BF16), seq)
    return x2.reshape(batch, seq, d)
```

```python
import functools
import math

import jax
import jax.numpy as jnp
import numpy as np
from jax import lax
from jax.experimental import pallas as pl
from jax.experimental.pallas import tpu as pltpu

F32 = jnp.float32
BF16 = jnp.bfloat16

D_MODEL = 1024
DEPTH = 2
EPS = 1e-6
A_HEADS = 4
A_HD = 64
A_VD = 2 * A_HD
B_HEADS = 8
B_Q_RANK = 256
B_KV_RANK = 128
B_NOPE = 64
B_ROPE = 32
B_VD = 64
ROPE_THETA = 10000.0
C_HEADS = 8
C_KV_HEADS = 2
C_HD = 64
WINDOW = 128
N_BUCKETS = 32
MAX_DIST = 128
D_FF = 2816
CONV_W = 3

LANES = 128
HALF = LANES // 2
NEG = -1e30
LOG2E = math.log2(math.e)
ROW_CHUNK = 32
DEN_ROWS = 16
VMEM_LIMIT = 56 * 1024 * 1024

T_ATT = 512
TM_IN = 512
TM_MERGE = 512
TM_FFN = 512
FF_CHUNK = 256
N_FF_CHUNKS = D_FF // FF_CHUNK
HALO = 8

_SEG = {"qa": (0, 0, 512), "ka": (0, 512, 1024), "va": (0, 1024, 1536), "cq": (0, 1536, 1792),
        "ckv": (0, 1792, 1920), "qc": (1, 0, 512),
        "kpe": (2, 0, 128), "kpe_sw": (2, 128, 256), "kc": (2, 256, 512), "vc": (2, 512, 768)}


def _params(sem, vmem=VMEM_LIMIT):
    return pltpu.CompilerParams(dimension_semantics=sem, vmem_limit_bytes=vmem)


def _const_spec(shape):
    nd = len(shape)
    return pl.BlockSpec(shape, lambda *_: (0,) * nd, pipeline_mode=pl.Buffered(1))


def _lane_lo():
    return lax.broadcasted_iota(jnp.int32, (1, LANES), 1) < HALF


def _rope_kernel(pos_ref, inv_ref, sign_ref, cos_ref, sin_ref):
    ang = pos_ref[...].astype(F32) * inv_ref[...]
    cos_ref[...] = jnp.cos(ang)
    sin_ref[...] = jnp.sin(ang) * sign_ref[...]


def _rope_tables(pos_col):
    n = pos_col.shape[0]
    inv = 1.0 / (ROPE_THETA ** (jnp.arange(0, B_ROPE, 2, dtype=F32) / B_ROPE))
    z = jnp.zeros((B_NOPE,), F32)
    zp = jnp.zeros((LANES - B_NOPE - B_ROPE,), F32)
    inv_pat = jnp.concatenate([z, inv, inv, zp])[None, :]
    ones = jnp.ones((B_ROPE // 2,), F32)
    sign_pat = jnp.concatenate([z, -ones, ones, zp])[None, :]
    tm = 2048
    return pl.pallas_call(
        _rope_kernel,
        grid=(n // tm,),
        in_specs=[pl.BlockSpec((tm, 1), lambda i: (i, 0)),
                  pl.BlockSpec((1, LANES), lambda i: (0, 0)),
                  pl.BlockSpec((1, LANES), lambda i: (0, 0))],
        out_specs=[pl.BlockSpec((tm, LANES), lambda i: (i, 0))] * 2,
        out_shape=[jax.ShapeDtypeStruct((n, LANES), F32)] * 2,
        compiler_params=_params(("parallel",)),
        name="rope_tables",
    )(pos_col, inv_pat, sign_pat)


def _bucket(rel):
    n = jnp.maximum(rel, 0)
    max_exact = N_BUCKETS // 2
    nf = jnp.maximum(n, 1).astype(F32)
    large = max_exact + (jnp.log(nf / max_exact) / math.log(MAX_DIST / max_exact)
                         * (N_BUCKETS - max_exact)).astype(jnp.int32)
    large = jnp.minimum(large, N_BUCKETS - 1)
    return jnp.where(n < max_exact, n, large)


def _lookup(tab_ref, bucket, col):
    out = jnp.zeros(bucket.shape, F32)
    for k in range(N_BUCKETS):
        out = jnp.where(bucket == k, tab_ref[k, col], out)
    return out


def _bias_a_kernel(tab_ref, out_ref):
    h = pl.program_id(0)
    t = out_ref.shape[-1]
    key = lax.broadcasted_iota(jnp.int32, (t, t), 0)
    qry = lax.broadcasted_iota(jnp.int32, (t, t), 1)
    far = tab_ref[N_BUCKETS - 1, h]
    rel = qry - key
    out_ref[0, 0] = jnp.where(rel >= 0, (_lookup(tab_ref, _bucket(rel), h) - far) * LOG2E, NEG)
    out_ref[0, 1] = (_lookup(tab_ref, _bucket(rel + t), h) - far) * LOG2E


def _bias_c_kernel(tab_ref, out_ref):
    h = pl.program_id(0)
    row = lax.broadcasted_iota(jnp.int32, (WINDOW, 2 * WINDOW), 0)
    col = lax.broadcasted_iota(jnp.int32, (WINDOW, 2 * WINDOW), 1)
    rel = row + WINDOW - col
    valid = (rel >= 0) & (rel < WINDOW)
    out_ref[0] = jnp.where(valid, _lookup(tab_ref, _bucket(rel), h + A_HEADS), NEG)


def _bias_tiles(table):
    smem = pl.BlockSpec(memory_space=pltpu.SMEM)
    bias_a = pl.pallas_call(
        _bias_a_kernel,
        grid=(A_HEADS,),
        in_specs=[smem],
        out_specs=pl.BlockSpec((1, 2, T_ATT, T_ATT), lambda h: (h, 0, 0, 0)),
        out_shape=jax.ShapeDtypeStruct((A_HEADS, 2, T_ATT, T_ATT), F32),
        compiler_params=_params(("parallel",)),
        name="bias_a",
    )(table)
    bias_c = pl.pallas_call(
        _bias_c_kernel,
        grid=(C_HEADS,),
        in_specs=[smem],
        out_specs=pl.BlockSpec((1, WINDOW, 2 * WINDOW), lambda h: (h, 0, 0)),
        out_shape=jax.ShapeDtypeStruct((C_HEADS, WINDOW, 2 * WINDOW), F32),
        compiler_params=_params(("parallel",)),
        name="bias_c",
    )(table)
    return bias_a, bias_c


def _rms(t, width):
    return lax.rsqrt(jnp.sum(t * t, axis=-1, keepdims=True) / width + EPS)


def _in_kernel(x_ref, cos_ref, sin_ref, gmix_ref, ga_ref, glat_ref, gqb_ref, gkb_ref, gc_ref,
               w1_ref, wqc_ref, ws_ref, wuq_ref, wukv_ref,
               qa_ref, ka_ref, va_ref, qb_ref, kb_ref, vb_ref, qc_ref, kc_ref, vc_ref):
    x = x_ref[...]
    h = (x * _rms(x, D_MODEL) * gmix_ref[...]).astype(BF16)
    lo = _lane_lo()
    w_refs = (w1_ref, wqc_ref, ws_ref)

    def proj(name):
        which, a, b = _SEG[name]
        return jnp.dot(h, w_refs[which][:, a:b], preferred_element_type=F32)

    def norm_halves(t, g, out_ref):
        for j in range(t.shape[1] // LANES):
            sl = slice(j * LANES, (j + 1) * LANES)
            tj = t[:, sl]
            sq = tj * tj
            s_lo = jnp.sum(jnp.where(lo, sq, 0.0), axis=-1, keepdims=True)
            s_hi = jnp.sum(jnp.where(lo, 0.0, sq), axis=-1, keepdims=True)
            r = jnp.where(lo, lax.rsqrt(s_lo / HALF + EPS), lax.rsqrt(s_hi / HALF + EPS))
            out_ref[:, sl] = (tj * r * g[:, sl]).astype(out_ref.dtype)

    glat = glat_ref[...]
    cq = proj("cq")
    ckv = proj("ckv")
    kpe = proj("kpe")
    kpe_sw = proj("kpe_sw")
    cqn = (cq * _rms(cq, B_Q_RANK) * glat[:, :B_Q_RANK]).astype(BF16)
    ckvn = (ckv * _rms(ckv, B_KV_RANK) * glat[:, B_Q_RANK:]).astype(BF16)
    uq = jnp.dot(cqn, wuq_ref[...], preferred_element_type=F32)
    ukv = jnp.dot(ckvn, wukv_ref[...], preferred_element_type=F32)
    qa = proj("qa")
    ka = proj("ka")
    qc = proj("qc")
    kc = proj("kc")
    va = proj("va")
    vc = proj("vc")

    cos = cos_ref[...]
    sin = sin_ref[...]
    hw = B_HEADS * LANES
    for j in range(B_HEADS):
        sl = slice(j * LANES, (j + 1) * LANES)
        raw = uq[:, sl]
        raw_sw = uq[:, hw + j * LANES: hw + (j + 1) * LANES]
        sq = raw * raw
        s_n = jnp.sum(jnp.where(lo, sq, 0.0), axis=-1, keepdims=True)
        s_r = jnp.sum(jnp.where(lo, 0.0, sq), axis=-1, keepdims=True)
        r = jnp.where(lo, lax.rsqrt(s_n / B_NOPE + EPS), lax.rsqrt(s_r / B_ROPE + EPS))
        out = r * (raw * gqb_ref[0:1, sl] * cos + raw_sw * gqb_ref[1:2, sl] * sin)
        qb_ref[:, sl] = out.astype(BF16)

    vb_ref[...] = ukv[:, hw:].T.astype(BF16)
    kpe_out = _rms(kpe, B_ROPE) * (kpe * gkb_ref[1:2, :] * cos + kpe_sw * gkb_ref[2:3, :] * sin)
    for j in range(B_HEADS):
        sl = slice(j * LANES, (j + 1) * LANES)
        raw = ukv[:, sl]
        kb_ref[:, sl] = (raw * _rms(raw, B_NOPE) * gkb_ref[0:1, :] + kpe_out).astype(BF16)

    norm_halves(qa, ga_ref[0:1, :], qa_ref)
    norm_halves(ka, ga_ref[1:2, :], ka_ref)
    gc = gc_ref[...]
    norm_halves(qc, gc[:, :C_HEADS * C_HD], qc_ref)
    norm_halves(kc, gc[:, C_HEADS * C_HD:], kc_ref)
    va_ref[...] = va.T.astype(BF16)
    vc_ref[...] = vc.astype(BF16)


def _in_proj(x2, cos, sin, gmix, ga, glat, gqb, gkb, gc, w1, wqc, ws, wuq, wukv, batch, seq):
    n = x2.shape[0]
    tm = TM_IN
    nps = seq // tm
    row = lambda w: pl.BlockSpec((tm, w), lambda i: (i, 0))
    outs = (("qa", 512), ("ka", 512), ("va", None), ("qb", 1024), ("kb", 1024), ("vb", None),
            ("qc", 512), ("kc", 256), ("vc", 256))
    vt_spec = pl.BlockSpec((None, 512, tm), lambda i: (i // nps, 0, i % nps))
    vt_shape = jax.ShapeDtypeStruct((batch, 512, seq), BF16)
    return pl.pallas_call(
        _in_kernel,
        grid=(n // tm,),
        in_specs=[row(D_MODEL), row(LANES), row(LANES),
                  _const_spec(gmix.shape), _const_spec(ga.shape), _const_spec(glat.shape),
                  _const_spec(gqb.shape), _const_spec(gkb.shape), _const_spec(gc.shape),
                  _const_spec(w1.shape), _const_spec(wqc.shape), _const_spec(ws.shape),
                  _const_spec(wuq.shape), _const_spec(wukv.shape)],
        out_specs=[vt_spec if w is None else row(w) for _, w in outs],
        out_shape=[vt_shape if w is None else jax.ShapeDtypeStruct((n, w), BF16) for _, w in outs],
        compiler_params=_params(("parallel",)),
        name="in_proj",
    )(x2, cos, sin, gmix, ga, glat, gqb, gkb, gc, w1, wqc, ws, wuq, wukv)


def _flash_kernel(*refs, mode, lambda_init):
    if mode == "diff":
        (q_ref, k_ref, vt_ref, bias_ref, lam_ref, subg_ref, o_ref,
         s_scr, p_scr, acc_scr, m_scr, mt_scr, al_scr) = refs
    else:
        q_ref, k_ref, vt_ref, o_ref, s_scr, p_scr, acc_scr, m_scr, mt_scr, al_scr = refs
    t = q_ref.shape[0]
    qi = pl.program_id(2)
    q = q_ref[...]
    if mode == "diff":
        lo = _lane_lo()
        zero = jnp.zeros_like(q)
        qs = (jnp.where(lo, q, zero), jnp.where(lo, zero, q))
    else:
        qs = (q[:, :LANES], q[:, LANES:])
    acc_scr[...] = jnp.zeros_like(acc_scr)
    m_scr[...] = jnp.full(m_scr.shape, NEG, F32)
    p_scr[2] = jnp.zeros(p_scr.shape[1:], BF16)
    p_scr[3] = jnp.zeros(p_scr.shape[1:], BF16)
    al_scr[2] = jnp.ones(al_scr.shape[1:], F32)
    al_scr[3] = jnp.ones(al_scr.shape[1:], F32)

    def qk(ki, slot):
        start = pl.multiple_of(ki * t, t)
        k = k_ref[pl.ds(start, t), :]
        for j in range(2):
            kj = k if mode == "diff" else k[:, j * LANES:(j + 1) * LANES]
            s = lax.dot_general(kj, qs[j], (((1,), (1,)), ((), ())), preferred_element_type=F32)
            s_scr[2 * slot + j] = s
            mt_scr[2 * slot + j] = jnp.max(s, axis=0, keepdims=True)

    def pv(ki, slot):
        start = pl.multiple_of(jnp.maximum(ki, 0) * t, t)
        vt = jnp.concatenate([vt_ref[:, pl.ds(start, t)], jnp.ones((DEN_ROWS, t), BF16)], axis=0)
        for j in range(2):
            c = 2 * slot + j
            acc_scr[j] = al_scr[c] * acc_scr[j] + jnp.dot(vt, p_scr[c], preferred_element_type=F32)

    def softmax(slot, kind):
        chunks = range(0, t, ROW_CHUNK)
        if mode != "diff" and kind == "prev":
            kind = "far"
        for j in range(2):
            c = 2 * slot + j
            if kind == "far":
                mc = mt_scr[c]
            else:
                tops = []
                for r0 in chunks:
                    x = s_scr[c, r0:r0 + ROW_CHUNK, :]
                    if mode == "diff":
                        x = x + bias_ref[0, 0 if kind == "diag" else 1, r0:r0 + ROW_CHUNK, :]
                    else:
                        key = lax.broadcasted_iota(jnp.int32, (ROW_CHUNK, t), 0) + r0
                        qry = lax.broadcasted_iota(jnp.int32, (ROW_CHUNK, t), 1)
                        x = jnp.where(key <= qry, x, NEG)
                    s_scr[c, r0:r0 + ROW_CHUNK, :] = x
                    top = x[0:8, :]
                    for r in range(8, ROW_CHUNK, 8):
                        top = jnp.maximum(top, x[r:r + 8, :])
                    tops.append(top)
                while len(tops) > 1:
                    tops = [jnp.maximum(a, b) for a, b in zip(tops[0::2], tops[1::2])]
                mc = jnp.max(tops[0], axis=0, keepdims=True)
            m_old = m_scr[j]
            m_new = jnp.maximum(m_old, mc)
            al_scr[c] = jnp.exp2(m_old - m_new)
            m_scr[j] = m_new
            for r0 in chunks:
                d = s_scr[c, r0:r0 + ROW_CHUNK, :] - m_new
                p_scr[c, r0:r0 + ROW_CHUNK, :] = jnp.exp2(d.astype(BF16))

    def stage(ti, slot, kind, qk_next=True):
        if qk_next:
            qk(ti + 1, 1 - slot)
        softmax(slot, kind)
        pv(ti - 1, 1 - slot)

    n_far = jnp.maximum(qi - 1, 0)
    done = 2 * (n_far // 2)
    qk(0, 0)

    def far_pair(jj, carry):
        stage(2 * jj, 0, "far")
        stage(2 * jj + 1, 1, "far")
        return carry

    lax.fori_loop(0, n_far // 2, far_pair, 0)

    @pl.when(qi == 0)
    def _():
        softmax(0, "diag")
        pv(qi, 0)

    @pl.when((qi > 0) & (n_far % 2 == 0))
    def _():
        stage(done, 0, "prev")
        stage(done + 1, 1, "diag", qk_next=False)
        pv(done + 1, 1)

    @pl.when(n_far % 2 == 1)
    def _():
        stage(done, 0, "far")
        stage(done + 1, 1, "prev")
        stage(done + 2, 0, "diag", qk_next=False)
        pv(done + 2, 0)

    o0 = acc_scr[0, :LANES, :] / acc_scr[0, LANES:LANES + 1, :]
    o1 = acc_scr[1, :LANES, :] / acc_scr[1, LANES:LANES + 1, :]
    if mode == "diff":
        lv = lam_ref[...]
        lam = (jnp.exp(jnp.sum(lv[0:1] * lv[1:2], axis=-1, keepdims=True))
               - jnp.exp(jnp.sum(lv[2:3] * lv[3:4], axis=-1, keepdims=True)) + lambda_init)
        o = o0 - lam * o1
        o = o * lax.rsqrt(jnp.sum(o * o, axis=0, keepdims=True) / A_VD + EPS)
        o = o.T * subg_ref[...] * (1.0 - lambda_init)
    else:
        upper = lax.broadcasted_iota(jnp.int32, (LANES, 1), 0) < HALF
        o = jnp.where(upper, o0, o1).T
    o_ref[...] = o.astype(o_ref.dtype)


def _flash(q, k, vt, batch, seq, mode, extra=(), lambda_init=0.0):
    n = q.shape[0]
    t = T_ATT
    nq = seq // t
    qw = LANES if mode == "diff" else 2 * LANES
    groups = q.shape[1] // qw
    in_specs = [pl.BlockSpec((t, qw), lambda b, g, i: (b * nq + i, g)),
                pl.BlockSpec((seq, qw), lambda b, g, i: (b, g)),
                pl.BlockSpec((None, LANES, seq), lambda b, g, i: (b, g, 0))]
    if mode == "diff":
        bias, lam, subg = extra
        in_specs += [pl.BlockSpec((1, 2, t, t), lambda b, g, i: (g, 0, 0, 0)),
                     pl.BlockSpec(lam.shape, lambda b, g, i: (0, 0)),
                     pl.BlockSpec(subg.shape, lambda b, g, i: (0, 0))]
    return pl.pallas_call(
        functools.partial(_flash_kernel, mode=mode, lambda_init=lambda_init),
        grid=(batch, groups, nq),
        in_specs=in_specs,
        out_specs=pl.BlockSpec((t, LANES), lambda b, g, i: (b * nq + i, g)),
        out_shape=jax.ShapeDtypeStruct((n, groups * LANES), BF16),
        scratch_shapes=[pltpu.VMEM((4, t, t), F32), pltpu.VMEM((4, t, t), BF16),
                        pltpu.VMEM((2, LANES + DEN_ROWS, t), F32), pltpu.VMEM((2, 1, t), F32),
                        pltpu.VMEM((4, 1, t), F32), pltpu.VMEM((4, 1, t), F32)],
        compiler_params=_params(("parallel", "parallel", "arbitrary")),
        name="flash_" + mode,
    )(q, k, vt, *extra)


def _swa_kernel(sink_ref, q_ref, kp_ref, kc_ref, vp_ref, vc_ref, bias_ref, o_ref):
    nb = pl.program_id(1)
    lo = _lane_lo()
    col = lax.broadcasted_iota(jnp.int32, (WINDOW, 2 * WINDOW), 1)
    keep = (col >= WINDOW) | (nb > 0)
    grp = C_HEADS // C_KV_HEADS
    for hp in range(C_HEADS // 2):
        kv = (2 * hp) // grp
        sl = slice(hp * LANES, (hp + 1) * LANES)
        ksl = slice(kv * LANES, (kv + 1) * LANES)
        q = q_ref[:, sl]
        zero = jnp.zeros_like(q)
        k = jnp.concatenate([kp_ref[:, ksl], kc_ref[:, ksl]], axis=0)
        v = jnp.concatenate([vp_ref[:, ksl], vc_ref[:, ksl]], axis=0)
        outs = []
        for j in range(2):
            head = 2 * hp + j
            qj = jnp.where(lo, q, zero) if j == 0 else jnp.where(lo, zero, q)
            s = lax.dot_general(qj, k, (((1,), (1,)), ((), ())), preferred_element_type=F32)
            s = jnp.where(keep, s + bias_ref[head], NEG)
            sink = sink_ref[head]
            m = jnp.maximum(jnp.max(s, axis=-1, keepdims=True), sink)
            p = jnp.exp(s - m)
            den = jnp.sum(p, axis=-1, keepdims=True) + jnp.exp(sink - m)
            outs.append(jnp.dot(p.astype(BF16), v, preferred_element_type=F32) / den)
        o_ref[:, sl] = jnp.where(lo, outs[0], outs[1]).astype(o_ref.dtype)


def _swa(sinks, q, k, v, bias, batch, seq):
    n = q.shape[0]
    nb = seq // WINDOW
    cur = lambda b, i: (b * nb + i, 0)
    prev = lambda b, i: (b * nb + jnp.maximum(i - 1, 0), 0)
    kw = k.shape[1]
    return pl.pallas_call(
        _swa_kernel,
        grid=(batch, nb),
        in_specs=[pl.BlockSpec(memory_space=pltpu.SMEM),
                  pl.BlockSpec((WINDOW, q.shape[1]), cur),
                  pl.BlockSpec((WINDOW, kw), prev), pl.BlockSpec((WINDOW, kw), cur),
                  pl.BlockSpec((WINDOW, kw), prev), pl.BlockSpec((WINDOW, kw), cur),
                  pl.BlockSpec(bias.shape, lambda b, i: (0, 0, 0))],
        out_specs=pl.BlockSpec((WINDOW, q.shape[1]), cur),
        out_shape=jax.ShapeDtypeStruct((n, q.shape[1]), BF16),
        compiler_params=_params(("parallel", "arbitrary")),
        name="swa",
    )(sinks, q, k, k, v, v, bias)


def _merge_kernel(x_ref, ya_ref, yb_ref, yc_ref, gmix_ref, wg_ref, pa_ref, pb_ref, pc_ref, wo_ref, o_ref):
    x = x_ref[...]
    h = (x * _rms(x, D_MODEL) * gmix_ref[...]).astype(BF16)
    merged = None
    for j, (y_ref, p_ref) in enumerate(((ya_ref, pa_ref), (yb_ref, pb_ref), (yc_ref, pc_ref))):
        gate = jnp.dot(h, wg_ref[:, j * D_MODEL:(j + 1) * D_MODEL], preferred_element_type=F32)
        term = jax.nn.sigmoid(gate) * jnp.dot(y_ref[...], p_ref[...], preferred_element_type=F32)
        merged = term if merged is None else merged + term
    o_ref[...] = x + jnp.dot(merged.astype(BF16), wo_ref[...], preferred_element_type=F32)


def _merge(x2, ya, yb, yc, gmix, wg, pa, pb, pc, wo):
    n = x2.shape[0]
    tm = TM_MERGE
    row = lambda w: pl.BlockSpec((tm, w), lambda i: (i, 0))
    return pl.pallas_call(
        _merge_kernel,
        grid=(n // tm,),
        in_specs=[row(D_MODEL), row(ya.shape[1]), row(yb.shape[1]), row(yc.shape[1]),
                  _const_spec(gmix.shape), _const_spec(wg.shape), _const_spec(pa.shape),
                  _const_spec(pb.shape), _const_spec(pc.shape), _const_spec(wo.shape)],
        out_specs=row(D_MODEL),
        out_shape=jax.ShapeDtypeStruct((n, D_MODEL), F32),
        compiler_params=_params(("parallel",)),
        name="merge",
    )(x2, ya, yb, yc, gmix, wg, pa, pb, pc, wo)


def _ffn_kernel(x_ref, g_ref, wup_ref, cw_ref, cb_ref, wdn_ref, o_ref, ubuf, act, carry, *, tiles_per_seq):
    tm = x_ref.shape[0]
    fc = FF_CHUNK

    @pl.when(pl.program_id(0) % tiles_per_seq == 0)
    def _():
        carry[...] = jnp.zeros_like(carry)

    x = x_ref[...]
    h = (x * _rms(x, D_MODEL) * g_ref[...]).astype(BF16)

    def cols(ref, rows, j):
        return jnp.concatenate([ref[rows, j * fc:(j + 1) * fc], ref[rows, D_FF + j * fc:D_FF + (j + 1) * fc]],
                               axis=1)

    def up(j):
        for half, base in enumerate((j * fc, D_FF + j * fc)):
            ubuf[j, HALO:HALO + tm, half * fc:(half + 1) * fc] = jnp.dot(
                h, wup_ref[:, base:base + fc], preferred_element_type=F32)

    def conv_act(j):
        ubuf[j, 0:HALO, :] = carry[j]
        carry[j] = ubuf[j, tm:tm + HALO, :]
        y = cols(cb_ref, slice(0, 1), j)
        for tap in range(CONV_W):
            shift = CONV_W - 1 - tap
            y = y + cols(cw_ref, slice(tap, tap + 1), j) * ubuf[j, HALO - shift:HALO - shift + tm, :]
        gate = y[:, :fc]
        act[:, j * fc:(j + 1) * fc] = (gate * jax.nn.sigmoid(gate) * y[:, fc:]).astype(BF16)

    up(0)
    for j in range(N_FF_CHUNKS):
        if j + 1 < N_FF_CHUNKS:
            up(j + 1)
        conv_act(j)
    o_ref[...] = x + jnp.dot(act[...], wdn_ref[...], preferred_element_type=F32)


def _ffn(x2, g, wup, cw, cb, wdn, seq):
    n = x2.shape[0]
    tm = TM_FFN
    row = pl.BlockSpec((tm, D_MODEL), lambda i: (i, 0))
    return pl.pallas_call(
        functools.partial(_ffn_kernel, tiles_per_seq=seq // tm),
        grid=(n // tm,),
        in_specs=[row, _const_spec(g.shape), _const_spec(wup.shape), _const_spec(cw.shape),
                  _const_spec(cb.shape), _const_spec(wdn.shape)],
        out_specs=row,
        out_shape=jax.ShapeDtypeStruct((n, D_MODEL), F32),
        scratch_shapes=[pltpu.VMEM((N_FF_CHUNKS, HALO + tm, 2 * FF_CHUNK), F32),
                        pltpu.VMEM((tm, D_FF), BF16),
                        pltpu.VMEM((N_FF_CHUNKS, HALO, 2 * FF_CHUNK), F32)],
        compiler_params=_params(("arbitrary",)),
        name="ffn",
    )(x2, g, wup, cw, cb, wdn)


def _layer_params(l, w_in, a_q_g, a_k_g, b_q_a_g, b_kv_a_g, b_w_uq, b_w_ukv, b_qn_g, b_qr_g, b_kn_g,
                  b_kr_g, c_q_g, c_k_g, w_up, conv_w, conv_b):
    w = w_in[l]
    o_kpe = 3 * 512 + B_Q_RANK + B_KV_RANK
    o_qc = o_kpe + B_ROPE
    o_kc = o_qc + C_HEADS * C_HD
    o_vc = o_kc + C_KV_HEADS * C_HD
    o_g = o_vc + C_KV_HEADS * C_HD
    w1 = w[:, :o_kpe].astype(BF16)
    wqc = w[:, o_qc:o_kc].astype(BF16)
    wg = w[:, o_g:].astype(BF16)
    kpe_w = w[:, o_kpe:o_qc].astype(BF16)
    kc_w = w[:, o_kc:o_vc].astype(BF16)
    vc_w = w[:, o_vc:o_g].astype(BF16)
    z = lambda c: jnp.zeros((w.shape[0], c), BF16)
    r = B_ROPE // 2
    pad = LANES - B_NOPE - B_ROPE
    dup = lambda t: jnp.concatenate([t[:, :C_HD], t[:, :C_HD], t[:, C_HD:], t[:, C_HD:]], axis=1)
    ws = jnp.concatenate([
        z(B_NOPE), kpe_w, z(pad),
        z(B_NOPE), kpe_w[:, r:], kpe_w[:, :r], z(pad),
        dup(kc_w), dup(vc_w)], axis=1)

    uq = b_w_uq[l].reshape(B_Q_RANK, B_HEADS, B_NOPE + B_ROPE)
    nope, pe = uq[..., :B_NOPE], uq[..., B_NOPE:]
    zq = lambda c: jnp.zeros((B_Q_RANK, B_HEADS, c), uq.dtype)
    wuq = jnp.concatenate([
        jnp.concatenate([nope, pe, zq(pad)], axis=-1).reshape(B_Q_RANK, B_HEADS * LANES),
        jnp.concatenate([zq(B_NOPE), pe[..., r:], pe[..., :r], zq(pad)], axis=-1).reshape(B_Q_RANK, B_HEADS * LANES),
    ], axis=1).astype(BF16)

    ukv = b_w_ukv[l].reshape(B_KV_RANK, B_HEADS, B_NOPE + B_VD)
    zk = jnp.zeros((B_KV_RANK, B_HEADS, LANES - B_NOPE), ukv.dtype)
    wukv = jnp.concatenate([
        jnp.concatenate([ukv[..., :B_NOPE], zk], axis=-1).reshape(B_KV_RANK, B_HEADS * LANES),
        ukv[..., B_NOPE:].reshape(B_KV_RANK, B_HEADS * B_VD)], axis=1).astype(BF16)

    scale_a = A_HD ** -0.5 * LOG2E
    scale_b = (B_NOPE + B_ROPE) ** -0.5 * LOG2E
    scale_c = C_HD ** -0.5
    ga = jnp.stack([jnp.tile(a_q_g[l], 2 * A_HEADS) * scale_a, jnp.tile(a_k_g[l], 2 * A_HEADS)])
    glat = jnp.concatenate([b_q_a_g[l], b_kv_a_g[l]])[None, :]
    zl = lambda c: jnp.zeros((c,), F32)
    qr, kr = b_qr_g[l], b_kr_g[l]
    gqb = jnp.stack([
        jnp.tile(jnp.concatenate([b_qn_g[l], qr, zl(pad)]), B_HEADS),
        jnp.tile(jnp.concatenate([zl(B_NOPE), qr[r:], qr[:r], zl(pad)]), B_HEADS)]) * scale_b
    gkb = jnp.stack([
        jnp.concatenate([b_kn_g[l], zl(LANES - B_NOPE)]),
        jnp.concatenate([zl(B_NOPE), kr, zl(pad)]),
        jnp.concatenate([zl(B_NOPE), kr[r:], kr[:r], zl(pad)])])
    gc = jnp.concatenate([jnp.tile(c_q_g[l], C_HEADS) * scale_c, jnp.tile(c_k_g[l], 2 * C_KV_HEADS)])[None, :]

    return dict(w1=w1, wqc=wqc, ws=ws, wg=wg, wuq=wuq, wukv=wukv, ga=ga, glat=glat, gqb=gqb,
                gkb=gkb, gc=gc, wup=w_up[l].astype(BF16), cw=conv_w[l, :, 0, :], cb=conv_b[l][None, :])


def kernel(x, positions, rel_bias_table, ln_mix_g, w_in, a_q_g, a_k_g, a_lam_q1, a_lam_k1, a_lam_q2, a_lam_k2, a_subln_g, b_q_a_g, b_kv_a_g, b_w_uq, b_w_ukv, b_qn_g, b_qr_g, b_kn_g, b_kr_g, c_q_g, c_k_g, c_sinks, p_a, p_b, p_c, w_o, ln_ffn_g, w_up, conv_w, conv_b, w_down):
    batch, seq, d = x.shape
    n = batch * seq
    assert d == D_MODEL and seq % T_ATT == 0 and n % TM_IN == 0 and seq % TM_FFN == 0
    x2 = x.reshape(n, d)
    cos, sin = _rope_tables(positions.reshape(n, 1))
    bias_a, bias_c = _bias_tiles(rel_bias_table)
    for l in range(DEPTH):
        lambda_init = 0.8 - 0.6 * math.exp(-0.3 * l)
        p = _layer_params(l, w_in, a_q_g, a_k_g, b_q_a_g, b_kv_a_g, b_w_uq, b_w_ukv, b_qn_g, b_qr_g,
                          b_kn_g, b_kr_g, c_q_g, c_k_g, w_up, conv_w, conv_b)
        gmix = ln_mix_g[l][None, :]
        qa, ka, va, qb, kb, vb, qc, kc, vc = _in_proj(
            x2, cos, sin, gmix, p["ga"], p["glat"], p["gqb"], p["gkb"], p["gc"],
            p["w1"], p["wqc"], p["ws"], p["wuq"], p["wukv"], batch, seq)
        lam = jnp.stack([a_lam_q1[l], a_lam_k1[l], a_lam_q2[l], a_lam_k2[l]])
        ya = _flash(qa, ka, va, batch, seq, "diff", extra=(bias_a, lam, a_subln_g[l][None, :]),
                    lambda_init=lambda_init)
        yb = _flash(qb, kb, vb, batch, seq, "mla")
        yc = _swa(c_sinks[l], qc, kc, vc, bias_c, batch, seq)
        x2 = _merge(x2, ya, yb, yc, gmix, p["wg"], p_a[l].astype(BF16), p_b[l].astype(BF16),
                    p_c[l].astype(BF16), w_o[l].astype(BF16))
        x2 = _ffn(x2, ln_ffn_g[l][None, :], p["wup"], p["cw"], p["cb"], w_down[l].astype(BF16), seq)
    return x2.reshape(batch, seq, d)
```

```python
import functools
import math

import jax
import jax.numpy as jnp
import numpy as np
from jax import lax
from jax.experimental import pallas as pl
from jax.experimental.pallas import tpu as pltpu

F32 = jnp.float32
BF16 = jnp.bfloat16

D_MODEL = 1024
DEPTH = 2
EPS = 1e-6
A_HEADS = 4
A_HD = 64
A_VD = 2 * A_HD
B_HEADS = 8
B_Q_RANK = 256
B_KV_RANK = 128
B_NOPE = 64
B_ROPE = 32
B_VD = 64
ROPE_THETA = 10000.0
C_HEADS = 8
C_KV_HEADS = 2
C_HD = 64
WINDOW = 128
N_BUCKETS = 32
MAX_DIST = 128
D_FF = 2816
CONV_W = 3

LANES = 128
HALF = LANES // 2
NEG = -1e30
LOG2E = math.log2(math.e)
ROW_CHUNK = 32
DEN_ROWS = 16
KIND_FAR, KIND_PREV, KIND_DIAG = 0, 1, 2
S_SLOTS = 4
P_SLOTS = 2
ITEMS_PER_TRIP = 12
VMEM_LIMIT = 56 * 1024 * 1024

T_ATT = 512
TM_IN = 512
TM_MERGE = 512
TM_FFN = 512
FF_CHUNK = 256
N_FF_CHUNKS = D_FF // FF_CHUNK
HALO = 8

_SEG = {"qa": (0, 0, 512), "ka": (0, 512, 1024), "va": (0, 1024, 1536), "cq": (0, 1536, 1792),
        "ckv": (0, 1792, 1920), "qc": (1, 0, 512),
        "kpe": (2, 0, 128), "kpe_sw": (2, 128, 256), "kc": (2, 256, 512), "vc": (2, 512, 768)}


def _params(sem, vmem=VMEM_LIMIT):
    return pltpu.CompilerParams(dimension_semantics=sem, vmem_limit_bytes=vmem)


def _const_spec(shape):
    nd = len(shape)
    return pl.BlockSpec(shape, lambda *_: (0,) * nd, pipeline_mode=pl.Buffered(1))


def _lane_lo():
    return lax.broadcasted_iota(jnp.int32, (1, LANES), 1) < HALF


def _rope_kernel(pos_ref, inv_ref, sign_ref, cos_ref, sin_ref):
    ang = pos_ref[...].astype(F32) * inv_ref[...]
    cos_ref[...] = jnp.cos(ang)
    sin_ref[...] = jnp.sin(ang) * sign_ref[...]


def _rope_tables(pos_col):
    n = pos_col.shape[0]
    inv = 1.0 / (ROPE_THETA ** (jnp.arange(0, B_ROPE, 2, dtype=F32) / B_ROPE))
    z = jnp.zeros((B_NOPE,), F32)
    zp = jnp.zeros((LANES - B_NOPE - B_ROPE,), F32)
    inv_pat = jnp.concatenate([z, inv, inv, zp])[None, :]
    ones = jnp.ones((B_ROPE // 2,), F32)
    sign_pat = jnp.concatenate([z, -ones, ones, zp])[None, :]
    tm = 2048
    return pl.pallas_call(
        _rope_kernel,
        grid=(n // tm,),
        in_specs=[pl.BlockSpec((tm, 1), lambda i: (i, 0)),
                  pl.BlockSpec((1, LANES), lambda i: (0, 0)),
                  pl.BlockSpec((1, LANES), lambda i: (0, 0))],
        out_specs=[pl.BlockSpec((tm, LANES), lambda i: (i, 0))] * 2,
        out_shape=[jax.ShapeDtypeStruct((n, LANES), F32)] * 2,
        compiler_params=_params(("parallel",)),
        name="rope_tables",
    )(pos_col, inv_pat, sign_pat)


def _bucket(rel):
    n = jnp.maximum(rel, 0)
    max_exact = N_BUCKETS // 2
    nf = jnp.maximum(n, 1).astype(F32)
    large = max_exact + (jnp.log(nf / max_exact) / math.log(MAX_DIST / max_exact)
                         * (N_BUCKETS - max_exact)).astype(jnp.int32)
    large = jnp.minimum(large, N_BUCKETS - 1)
    return jnp.where(n < max_exact, n, large)


def _lookup(tab_ref, bucket, col):
    out = jnp.zeros(bucket.shape, F32)
    for k in range(N_BUCKETS):
        out = jnp.where(bucket == k, tab_ref[k, col], out)
    return out


def _bias_a_kernel(tab_ref, out_ref):
    h = pl.program_id(0)
    t = out_ref.shape[-1]
    key = lax.broadcasted_iota(jnp.int32, (t, t), 0)
    qry = lax.broadcasted_iota(jnp.int32, (t, t), 1)
    far = tab_ref[N_BUCKETS - 1, h]
    rel = qry - key
    out_ref[0, KIND_FAR] = jnp.zeros((t, t), F32)
    out_ref[0, KIND_PREV] = (_lookup(tab_ref, _bucket(rel + t), h) - far) * LOG2E
    out_ref[0, KIND_DIAG] = jnp.where(rel >= 0, (_lookup(tab_ref, _bucket(rel), h) - far) * LOG2E, NEG)


def _mask_kernel(out_ref):
    t = out_ref.shape[-1]
    key = lax.broadcasted_iota(jnp.int32, (t, t), 0)
    qry = lax.broadcasted_iota(jnp.int32, (t, t), 1)
    out_ref[0, KIND_FAR] = jnp.zeros((t, t), F32)
    out_ref[0, KIND_PREV] = jnp.zeros((t, t), F32)
    out_ref[0, KIND_DIAG] = jnp.where(key <= qry, 0.0, NEG)


def _bias_c_kernel(tab_ref, out_ref):
    h = pl.program_id(0)
    row = lax.broadcasted_iota(jnp.int32, (WINDOW, 2 * WINDOW), 0)
    col = lax.broadcasted_iota(jnp.int32, (WINDOW, 2 * WINDOW), 1)
    rel = row + WINDOW - col
    valid = (rel >= 0) & (rel < WINDOW)
    out_ref[0] = jnp.where(valid, _lookup(tab_ref, _bucket(rel), h + A_HEADS), NEG)


def _bias_tiles(table):
    smem = pl.BlockSpec(memory_space=pltpu.SMEM)
    bias_a = pl.pallas_call(
        _bias_a_kernel,
        grid=(A_HEADS,),
        in_specs=[smem],
        out_specs=pl.BlockSpec((1, 3, T_ATT, T_ATT), lambda h: (h, 0, 0, 0)),
        out_shape=jax.ShapeDtypeStruct((A_HEADS, 3, T_ATT, T_ATT), F32),
        compiler_params=_params(("parallel",)),
        name="bias_a",
    )(table)
    mask_b = pl.pallas_call(
        _mask_kernel,
        out_shape=jax.ShapeDtypeStruct((1, 3, T_ATT, T_ATT), F32),
        compiler_params=pltpu.CompilerParams(vmem_limit_bytes=VMEM_LIMIT),
        name="mask_b",
    )()
    bias_c = pl.pallas_call(
        _bias_c_kernel,
        grid=(C_HEADS,),
        in_specs=[smem],
        out_specs=pl.BlockSpec((1, WINDOW, 2 * WINDOW), lambda h: (h, 0, 0)),
        out_shape=jax.ShapeDtypeStruct((C_HEADS, WINDOW, 2 * WINDOW), F32),
        compiler_params=_params(("parallel",)),
        name="bias_c",
    )(table)
    return bias_a, mask_b, bias_c


def _rms(t, width):
    return lax.rsqrt(jnp.sum(t * t, axis=-1, keepdims=True) / width + EPS)


def _in_kernel(x_ref, cos_ref, sin_ref, gmix_ref, ga_ref, glat_ref, gqb_ref, gkb_ref, gc_ref,
               w1_ref, wqc_ref, ws_ref, wuq_ref, wukv_ref,
               qa_ref, ka_ref, va_ref, qb_ref, kb_ref, vb_ref, qc_ref, kc_ref, vc_ref):
    x = x_ref[...]
    h = (x * _rms(x, D_MODEL) * gmix_ref[...]).astype(BF16)
    lo = _lane_lo()
    w_refs = (w1_ref, wqc_ref, ws_ref)

    def proj(name):
        which, a, b = _SEG[name]
        return jnp.dot(h, w_refs[which][:, a:b], preferred_element_type=F32)

    def norm_halves(t, g, out_ref):
        for j in range(t.shape[1] // LANES):
            sl = slice(j * LANES, (j + 1) * LANES)
            tj = t[:, sl]
            sq = tj * tj
            s_lo = jnp.sum(jnp.where(lo, sq, 0.0), axis=-1, keepdims=True)
            s_hi = jnp.sum(jnp.where(lo, 0.0, sq), axis=-1, keepdims=True)
            r = jnp.where(lo, lax.rsqrt(s_lo / HALF + EPS), lax.rsqrt(s_hi / HALF + EPS))
            out_ref[:, sl] = (tj * r * g[:, sl]).astype(out_ref.dtype)

    glat = glat_ref[...]
    cq = proj("cq")
    ckv = proj("ckv")
    kpe = proj("kpe")
    kpe_sw = proj("kpe_sw")
    cqn = (cq * _rms(cq, B_Q_RANK) * glat[:, :B_Q_RANK]).astype(BF16)
    ckvn = (ckv * _rms(ckv, B_KV_RANK) * glat[:, B_Q_RANK:]).astype(BF16)
    uq = jnp.dot(cqn, wuq_ref[...], preferred_element_type=F32)
    ukv = jnp.dot(ckvn, wukv_ref[...], preferred_element_type=F32)
    qa = proj("qa")
    ka = proj("ka")
    qc = proj("qc")
    kc = proj("kc")
    va = proj("va")
    vc = proj("vc")

    cos = cos_ref[...]
    sin = sin_ref[...]
    hw = B_HEADS * LANES
    for j in range(B_HEADS):
        sl = slice(j * LANES, (j + 1) * LANES)
        raw = uq[:, sl]
        raw_sw = uq[:, hw + j * LANES: hw + (j + 1) * LANES]
        sq = raw * raw
        s_n = jnp.sum(jnp.where(lo, sq, 0.0), axis=-1, keepdims=True)
        s_r = jnp.sum(jnp.where(lo, 0.0, sq), axis=-1, keepdims=True)
        r = jnp.where(lo, lax.rsqrt(s_n / B_NOPE + EPS), lax.rsqrt(s_r / B_ROPE + EPS))
        out = r * (raw * gqb_ref[0:1, sl] * cos + raw_sw * gqb_ref[1:2, sl] * sin)
        qb_ref[:, sl] = out.astype(BF16)

    vb_ref[...] = ukv[:, hw:].T.astype(BF16)
    kpe_out = _rms(kpe, B_ROPE) * (kpe * gkb_ref[1:2, :] * cos + kpe_sw * gkb_ref[2:3, :] * sin)
    for j in range(B_HEADS):
        sl = slice(j * LANES, (j + 1) * LANES)
        raw = ukv[:, sl]
        kb_ref[:, sl] = (raw * _rms(raw, B_NOPE) * gkb_ref[0:1, :] + kpe_out).astype(BF16)

    norm_halves(qa, ga_ref[0:1, :], qa_ref)
    norm_halves(ka, ga_ref[1:2, :], ka_ref)
    gc = gc_ref[...]
    norm_halves(qc, gc[:, :C_HEADS * C_HD], qc_ref)
    norm_halves(kc, gc[:, C_HEADS * C_HD:], kc_ref)
    va_ref[...] = va.T.astype(BF16)
    vc_ref[...] = vc.astype(BF16)


def _in_proj(x2, cos, sin, gmix, ga, glat, gqb, gkb, gc, w1, wqc, ws, wuq, wukv, batch, seq):
    n = x2.shape[0]
    tm = TM_IN
    nps = seq // tm
    row = lambda w: pl.BlockSpec((tm, w), lambda i: (i, 0))
    outs = (("qa", 512), ("ka", 512), ("va", None), ("qb", 1024), ("kb", 1024), ("vb", None),
            ("qc", 512), ("kc", 256), ("vc", 256))
    vt_spec = pl.BlockSpec((None, 512, tm), lambda i: (i // nps, 0, i % nps))
    vt_shape = jax.ShapeDtypeStruct((batch, 512, seq), BF16)
    return pl.pallas_call(
        _in_kernel,
        grid=(n // tm,),
        in_specs=[row(D_MODEL), row(LANES), row(LANES),
                  _const_spec(gmix.shape), _const_spec(ga.shape), _const_spec(glat.shape),
                  _const_spec(gqb.shape), _const_spec(gkb.shape), _const_spec(gc.shape),
                  _const_spec(w1.shape), _const_spec(wqc.shape), _const_spec(ws.shape),
                  _const_spec(wuq.shape), _const_spec(wukv.shape)],
        out_specs=[vt_spec if w is None else row(w) for _, w in outs],
        out_shape=[vt_shape if w is None else jax.ShapeDtypeStruct((n, w), BF16) for _, w in outs],
        compiler_params=_params(("parallel",)),
        name="in_proj",
    )(x2, cos, sin, gmix, ga, glat, gqb, gkb, gc, w1, wqc, ws, wuq, wukv)


def _flash_kernel(*refs, mode, lambda_init):
    if mode == "diff":
        (q_ref, k_ref, vt_ref, bias_ref, lam_ref, subg_ref, o_ref,
         s_scr, p_scr, acc_scr, m_scr, mt_scr, al_scr) = refs
    else:
        q_ref, k_ref, vt_ref, o_ref, s_scr, p_scr, acc_scr, m_scr, mt_scr, al_scr = refs
    t = q_ref.shape[0]
    qi = pl.program_id(2)
    q = q_ref[...]
    if mode == "diff":
        lo = _lane_lo()
        zero = jnp.zeros_like(q)
        qs = (jnp.where(lo, q, zero), jnp.where(lo, zero, q))
    else:
        qs = (q[:, :LANES], q[:, LANES:])
    acc_scr[...] = jnp.zeros_like(acc_scr)
    m_scr[...] = jnp.full(m_scr.shape, NEG, F32)
    p_scr[2] = jnp.zeros(p_scr.shape[1:], BF16)
    p_scr[3] = jnp.zeros(p_scr.shape[1:], BF16)
    al_scr[2] = jnp.ones(al_scr.shape[1:], F32)
    al_scr[3] = jnp.ones(al_scr.shape[1:], F32)

    def qk(ki, slot):
        start = pl.multiple_of(ki * t, t)
        k = k_ref[pl.ds(start, t), :]
        for j in range(2):
            kj = k if mode == "diff" else k[:, j * LANES:(j + 1) * LANES]
            s = lax.dot_general(kj, qs[j], (((1,), (1,)), ((), ())), preferred_element_type=F32)
            s_scr[2 * slot + j] = s
            mt_scr[2 * slot + j] = jnp.max(s, axis=0, keepdims=True)

    def pv(ki, slot):
        start = pl.multiple_of(jnp.maximum(ki, 0) * t, t)
        vt = jnp.concatenate([vt_ref[:, pl.ds(start, t)], jnp.ones((DEN_ROWS, t), BF16)], axis=0)
        for j in range(2):
            c = 2 * slot + j
            acc_scr[j] = al_scr[c] * acc_scr[j] + jnp.dot(vt, p_scr[c], preferred_element_type=F32)

    def softmax(slot, kind):
        chunks = range(0, t, ROW_CHUNK)
        if mode != "diff" and kind == "prev":
            kind = "far"
        for j in range(2):
            c = 2 * slot + j
            if kind == "far":
                mc = mt_scr[c]
            else:
                tops = []
                for r0 in chunks:
                    x = s_scr[c, r0:r0 + ROW_CHUNK, :]
                    if mode == "diff":
                        x = x + bias_ref[0, 0 if kind == "diag" else 1, r0:r0 + ROW_CHUNK, :]
                    else:
                        key = lax.broadcasted_iota(jnp.int32, (ROW_CHUNK, t), 0) + r0
                        qry = lax.broadcasted_iota(jnp.int32, (ROW_CHUNK, t), 1)
                        x = jnp.where(key <= qry, x, NEG)
                    s_scr[c, r0:r0 + ROW_CHUNK, :] = x
                    top = x[0:8, :]
                    for r in range(8, ROW_CHUNK, 8):
                        top = jnp.maximum(top, x[r:r + 8, :])
                    tops.append(top)
                while len(tops) > 1:
                    tops = [jnp.maximum(a, b) for a, b in zip(tops[0::2], tops[1::2])]
                mc = jnp.max(tops[0], axis=0, keepdims=True)
            m_old = m_scr[j]
            m_new = jnp.maximum(m_old, mc)
            al_scr[c] = jnp.exp2(m_old - m_new)
            m_scr[j] = m_new
            for r0 in chunks:
                d = s_scr[c, r0:r0 + ROW_CHUNK, :] - m_new
                p_scr[c, r0:r0 + ROW_CHUNK, :] = jnp.exp2(d.astype(BF16))

    def stage(ti, slot, kind, qk_next=True):
        if qk_next:
            qk(ti + 1, 1 - slot)
        softmax(slot, kind)
        pv(ti - 1, 1 - slot)

    n_far = jnp.maximum(qi - 1, 0)
    done = 2 * (n_far // 2)
    qk(0, 0)

    def far_pair(jj, carry):
        stage(2 * jj, 0, "far")
        stage(2 * jj + 1, 1, "far")
        return carry

    lax.fori_loop(0, n_far // 2, far_pair, 0)

    @pl.when(qi == 0)
    def _():
        softmax(0, "diag")
        pv(qi, 0)

    @pl.when((qi > 0) & (n_far % 2 == 0))
    def _():
        stage(done, 0, "prev")
        stage(done + 1, 1, "diag", qk_next=False)
        pv(done + 1, 1)

    @pl.when(n_far % 2 == 1)
    def _():
        stage(done, 0, "far")
        stage(done + 1, 1, "prev")
        stage(done + 2, 0, "diag", qk_next=False)
        pv(done + 2, 0)

    o0 = acc_scr[0, :LANES, :] / acc_scr[0, LANES:LANES + 1, :]
    o1 = acc_scr[1, :LANES, :] / acc_scr[1, LANES:LANES + 1, :]
    if mode == "diff":
        lv = lam_ref[...]
        lam = (jnp.exp(jnp.sum(lv[0:1] * lv[1:2], axis=-1, keepdims=True))
               - jnp.exp(jnp.sum(lv[2:3] * lv[3:4], axis=-1, keepdims=True)) + lambda_init)
        o = o0 - lam * o1
        o = o * lax.rsqrt(jnp.sum(o * o, axis=0, keepdims=True) / A_VD + EPS)
        o = o.T * subg_ref[...] * (1.0 - lambda_init)
    else:
        upper = lax.broadcasted_iota(jnp.int32, (LANES, 1), 0) < HALF
        o = jnp.where(upper, o0, o1).T
    o_ref[...] = o.astype(o_ref.dtype)


def _flash(q, k, vt, batch, seq, mode, extra=(), lambda_init=0.0):
    n = q.shape[0]
    t = T_ATT
    nq = seq // t
    qw = LANES if mode == "diff" else 2 * LANES
    groups = q.shape[1] // qw
    in_specs = [pl.BlockSpec((t, qw), lambda b, g, i: (b * nq + i, g)),
                pl.BlockSpec((seq, qw), lambda b, g, i: (b, g)),
                pl.BlockSpec((None, LANES, seq), lambda b, g, i: (b, g, 0))]
    if mode == "diff":
        bias, lam, subg = extra
        in_specs += [pl.BlockSpec((1, 2, t, t), lambda b, g, i: (g, 0, 0, 0)),
                     pl.BlockSpec(lam.shape, lambda b, g, i: (0, 0)),
                     pl.BlockSpec(subg.shape, lambda b, g, i: (0, 0))]
    return pl.pallas_call(
        functools.partial(_flash_kernel, mode=mode, lambda_init=lambda_init),
        grid=(batch, groups, nq),
        in_specs=in_specs,
        out_specs=pl.BlockSpec((t, LANES), lambda b, g, i: (b * nq + i, g)),
        out_shape=jax.ShapeDtypeStruct((n, groups * LANES), BF16),
        scratch_shapes=[pltpu.VMEM((4, t, t), F32), pltpu.VMEM((4, t, t), BF16),
                        pltpu.VMEM((2, LANES + DEN_ROWS, t), F32), pltpu.VMEM((2, 1, t), F32),
                        pltpu.VMEM((4, 1, t), F32), pltpu.VMEM((4, 1, t), F32)],
        compiler_params=_params(("parallel", "parallel", "arbitrary")),
        name="flash_" + mode,
    )(q, k, vt, *extra)


def _attn_items(nq):
    items = [(qi, ki, KIND_DIAG if ki == qi else KIND_PREV if ki == qi - 1 else KIND_FAR)
             for qi in range(nq) for ki in range(qi + 1)]
    items += [items[-1]] * 2
    return np.asarray(items, np.int32).T


def _attn_kernel(*refs, mode, lambda_init, n_items):
    if mode == "diff":
        (tab_ref, q_ref, k_ref, vt_ref, bias_ref, lam_ref, subg_ref, o_ref,
         s_scr, p_scr, acc_scr, m_scr, mt_scr, al_scr) = refs
    else:
        tab_ref, q_ref, k_ref, vt_ref, bias_ref, o_ref, s_scr, p_scr, acc_scr, m_scr, mt_scr, al_scr = refs
    t = T_ATT
    lo = _lane_lo()
    m_scr[...] = jnp.full(m_scr.shape, NEG, F32)
    acc_scr[...] = jnp.zeros_like(acc_scr)
    if mode == "diff":
        lv = lam_ref[...]
        lam = (jnp.exp(jnp.sum(lv[0:1] * lv[1:2], axis=-1, keepdims=True))
               - jnp.exp(jnp.sum(lv[2:3] * lv[3:4], axis=-1, keepdims=True)) + lambda_init)
        out_gain = subg_ref[...] * (1.0 - lambda_init)

    def rows(idx):
        return pl.ds(pl.multiple_of(idx * t, t), t)

    def scores(n, slot):
        q = q_ref[rows(tab_ref[0, n]), :]
        k = k_ref[rows(tab_ref[1, n]), :]
        kind = tab_ref[2, n]
        for j in range(2):
            if mode == "diff":
                zero = jnp.zeros_like(q)
                qj = jnp.where(lo, q, zero) if j == 0 else jnp.where(lo, zero, q)
                kj = k
            else:
                qj = q[:, j * LANES:(j + 1) * LANES]
                kj = k[:, j * LANES:(j + 1) * LANES]
            s = lax.dot_general(kj, qj, (((1,), (1,)), ((), ())), preferred_element_type=F32)
            s = s + bias_ref[kind]
            s_scr[2 * slot + j] = s
            mt_scr[2 * slot + j] = jnp.max(s, axis=0, keepdims=True)

    def softmax(n, slot, pslot):
        restart = jnp.where(tab_ref[1, n] == 0, NEG, 0.0)
        for j in range(2):
            c = 2 * slot + j
            pc = 2 * pslot + j
            m_old = m_scr[j] + restart
            m_new = jnp.maximum(m_old, mt_scr[c])
            al_scr[pc] = jnp.exp2(m_old - m_new)
            m_scr[j] = m_new
            for r0 in range(0, t, ROW_CHUNK):
                d = s_scr[c, r0:r0 + ROW_CHUNK, :] - m_new
                p_scr[pc, r0:r0 + ROW_CHUNK, :] = jnp.exp2(d.astype(BF16))

    def values(n, pslot):
        n = jnp.maximum(n, 0)
        vt = jnp.concatenate([vt_ref[:, rows(tab_ref[1, n])], jnp.ones((DEN_ROWS, t), BF16)], axis=0)
        outs = []
        for j in range(2):
            pc = 2 * pslot + j
            acc = al_scr[pc] * acc_scr[j] + jnp.dot(vt, p_scr[pc], preferred_element_type=F32)
            acc_scr[j] = acc
            outs.append(acc[:LANES, :] / acc[LANES:LANES + 1, :])
        if mode == "diff":
            o = outs[0] - lam * outs[1]
            o = o * lax.rsqrt(jnp.sum(o * o, axis=0, keepdims=True) / A_VD + EPS)
            o = o.T * out_gain
        else:
            upper = lax.broadcasted_iota(jnp.int32, (LANES, 1), 0) < HALF
            o = jnp.where(upper, outs[0], outs[1]).T
        o_ref[rows(tab_ref[0, n]), :] = o.astype(o_ref.dtype)

    scores(0, 0)
    scores(1, 1)

    def body(i, carry):
        for u in range(ITEMS_PER_TRIP):
            n = ITEMS_PER_TRIP * i + u
            s = u % S_SLOTS
            scores(n + 2, (s + 2) % S_SLOTS)
            softmax(n, s, s % P_SLOTS)
            if u > 0:
                values(n - 1, (s + 1) % P_SLOTS)
        values(ITEMS_PER_TRIP * i + ITEMS_PER_TRIP - 1, (ITEMS_PER_TRIP - 1) % P_SLOTS)
        return carry

    lax.fori_loop(0, n_items // ITEMS_PER_TRIP, body, 0)


def _attn(q, k, vt, bias, batch, seq, mode, extra=(), lambda_init=0.0):
    n = q.shape[0]
    t = T_ATT
    nq = seq // t
    tab = _attn_items(nq)
    n_items = tab.shape[1] - 2
    assert n_items % ITEMS_PER_TRIP == 0 and ITEMS_PER_TRIP % S_SLOTS == 0 and S_SLOTS % P_SLOTS == 0
    qw = LANES if mode == "diff" else 2 * LANES
    groups = q.shape[1] // qw
    per_head = bias.shape[0] > 1
    in_specs = [pl.BlockSpec(memory_space=pltpu.SMEM),
                pl.BlockSpec((seq, qw), lambda b, g: (b, g)),
                pl.BlockSpec((seq, qw), lambda b, g: (b, g)),
                pl.BlockSpec((None, LANES, seq), lambda b, g: (b, g, 0)),
                pl.BlockSpec((None, 3, t, t), lambda b, g: (g if per_head else 0, 0, 0, 0))]
    in_specs += [pl.BlockSpec(e.shape, lambda b, g: (0, 0)) for e in extra]
    return pl.pallas_call(
        functools.partial(_attn_kernel, mode=mode, lambda_init=lambda_init, n_items=n_items),
        grid=(batch, groups),
        in_specs=in_specs,
        out_specs=pl.BlockSpec((seq, LANES), lambda b, g: (b, g)),
        out_shape=jax.ShapeDtypeStruct((n, groups * LANES), BF16),
        scratch_shapes=[pltpu.VMEM((2 * S_SLOTS, t, t), F32), pltpu.VMEM((2 * P_SLOTS, t, t), BF16),
                        pltpu.VMEM((2, LANES + DEN_ROWS, t), F32), pltpu.VMEM((2, 1, t), F32),
                        pltpu.VMEM((2 * S_SLOTS, 1, t), F32), pltpu.VMEM((2 * P_SLOTS, 1, t), F32)],
        compiler_params=_params(("parallel", "parallel")),
        name="attn_" + mode,
    )(jnp.asarray(tab), q, k, vt, bias, *extra)


def _swa_kernel(sink_ref, q_ref, kp_ref, kc_ref, vp_ref, vc_ref, bias_ref, o_ref):
    nb = pl.program_id(1)
    lo = _lane_lo()
    col = lax.broadcasted_iota(jnp.int32, (WINDOW, 2 * WINDOW), 1)
    keep = (col >= WINDOW) | (nb > 0)
    grp = C_HEADS // C_KV_HEADS
    scores = []
    for head in range(C_HEADS):
        hp, j = divmod(head, 2)
        ksl = slice((head // grp) * LANES, (head // grp + 1) * LANES)
        q = q_ref[:, hp * LANES:(hp + 1) * LANES]
        zero = jnp.zeros_like(q)
        k = jnp.concatenate([kp_ref[:, ksl], kc_ref[:, ksl]], axis=0)
        qj = jnp.where(lo, q, zero) if j == 0 else jnp.where(lo, zero, q)
        scores.append(lax.dot_general(qj, k, (((1,), (1,)), ((), ())), preferred_element_type=F32))
    probs = []
    for head in range(C_HEADS):
        s = jnp.where(keep, scores[head] + bias_ref[head], NEG)
        sink = sink_ref[head]
        m = jnp.maximum(jnp.max(s, axis=-1, keepdims=True), sink)
        p = jnp.exp(s - m)
        den = jnp.sum(p, axis=-1, keepdims=True) + jnp.exp(sink - m)
        probs.append((p.astype(BF16), den))
    for hp in range(C_HEADS // 2):
        ksl = slice(((2 * hp) // grp) * LANES, ((2 * hp) // grp + 1) * LANES)
        v = jnp.concatenate([vp_ref[:, ksl], vc_ref[:, ksl]], axis=0)
        outs = [jnp.dot(p, v, preferred_element_type=F32) / den for p, den in probs[2 * hp:2 * hp + 2]]
        o_ref[:, hp * LANES:(hp + 1) * LANES] = jnp.where(lo, outs[0], outs[1]).astype(o_ref.dtype)


def _swa(sinks, q, k, v, bias, batch, seq):
    n = q.shape[0]
    nb = seq // WINDOW
    cur = lambda b, i: (b * nb + i, 0)
    prev = lambda b, i: (b * nb + jnp.maximum(i - 1, 0), 0)
    kw = k.shape[1]
    return pl.pallas_call(
        _swa_kernel,
        grid=(batch, nb),
        in_specs=[pl.BlockSpec(memory_space=pltpu.SMEM),
                  pl.BlockSpec((WINDOW, q.shape[1]), cur),
                  pl.BlockSpec((WINDOW, kw), prev), pl.BlockSpec((WINDOW, kw), cur),
                  pl.BlockSpec((WINDOW, kw), prev), pl.BlockSpec((WINDOW, kw), cur),
                  pl.BlockSpec(bias.shape, lambda b, i: (0, 0, 0))],
        out_specs=pl.BlockSpec((WINDOW, q.shape[1]), cur),
        out_shape=jax.ShapeDtypeStruct((n, q.shape[1]), BF16),
        compiler_params=_params(("parallel", "arbitrary")),
        name="swa",
    )(sinks, q, k, k, v, v, bias)


def _merge_kernel(x_ref, ya_ref, yb_ref, yc_ref, gmix_ref, wg_ref, pa_ref, pb_ref, pc_ref, wo_ref, o_ref):
    x = x_ref[...]
    h = (x * _rms(x, D_MODEL) * gmix_ref[...]).astype(BF16)
    merged = None
    for j, (y_ref, p_ref) in enumerate(((ya_ref, pa_ref), (yb_ref, pb_ref), (yc_ref, pc_ref))):
        gate = jnp.dot(h, wg_ref[:, j * D_MODEL:(j + 1) * D_MODEL], preferred_element_type=F32)
        term = jax.nn.sigmoid(gate) * jnp.dot(y_ref[...], p_ref[...], preferred_element_type=F32)
        merged = term if merged is None else merged + term
    o_ref[...] = x + jnp.dot(merged.astype(BF16), wo_ref[...], preferred_element_type=F32)


def _merge(x2, ya, yb, yc, gmix, wg, pa, pb, pc, wo):
    n = x2.shape[0]
    tm = TM_MERGE
    row = lambda w: pl.BlockSpec((tm, w), lambda i: (i, 0))
    return pl.pallas_call(
        _merge_kernel,
        grid=(n // tm,),
        in_specs=[row(D_MODEL), row(ya.shape[1]), row(yb.shape[1]), row(yc.shape[1]),
                  _const_spec(gmix.shape), _const_spec(wg.shape), _const_spec(pa.shape),
                  _const_spec(pb.shape), _const_spec(pc.shape), _const_spec(wo.shape)],
        out_specs=row(D_MODEL),
        out_shape=jax.ShapeDtypeStruct((n, D_MODEL), F32),
        compiler_params=_params(("parallel",)),
        name="merge",
    )(x2, ya, yb, yc, gmix, wg, pa, pb, pc, wo)


def _ffn_kernel(x_ref, g_ref, wup_ref, cw_ref, cb_ref, wdn_ref, o_ref, ubuf, act, carry, *, tiles_per_seq):
    tm = x_ref.shape[0]
    fc = FF_CHUNK

    @pl.when(pl.program_id(0) % tiles_per_seq == 0)
    def _():
        carry[...] = jnp.zeros_like(carry)

    x = x_ref[...]
    h = (x * _rms(x, D_MODEL) * g_ref[...]).astype(BF16)

    def cols(ref, rows, j):
        return jnp.concatenate([ref[rows, j * fc:(j + 1) * fc], ref[rows, D_FF + j * fc:D_FF + (j + 1) * fc]],
                               axis=1)

    def up(j):
        for half, base in enumerate((j * fc, D_FF + j * fc)):
            ubuf[j, HALO:HALO + tm, half * fc:(half + 1) * fc] = jnp.dot(
                h, wup_ref[:, base:base + fc], preferred_element_type=F32)

    def conv_act(j):
        ubuf[j, 0:HALO, :] = carry[j]
        carry[j] = ubuf[j, tm:tm + HALO, :]
        y = cols(cb_ref, slice(0, 1), j)
        for tap in range(CONV_W):
            shift = CONV_W - 1 - tap
            y = y + cols(cw_ref, slice(tap, tap + 1), j) * ubuf[j, HALO - shift:HALO - shift + tm, :]
        gate = y[:, :fc]
        act[:, j * fc:(j + 1) * fc] = (gate * jax.nn.sigmoid(gate) * y[:, fc:]).astype(BF16)

    up(0)
    for j in range(N_FF_CHUNKS):
        if j + 1 < N_FF_CHUNKS:
            up(j + 1)
        conv_act(j)
    o_ref[...] = x + jnp.dot(act[...], wdn_ref[...], preferred_element_type=F32)


def _ffn(x2, g, wup, cw, cb, wdn, seq):
    n = x2.shape[0]
    tm = TM_FFN
    row = pl.BlockSpec((tm, D_MODEL), lambda i: (i, 0))
    return pl.pallas_call(
        functools.partial(_ffn_kernel, tiles_per_seq=seq // tm),
        grid=(n // tm,),
        in_specs=[row, _const_spec(g.shape), _const_spec(wup.shape), _const_spec(cw.shape),
                  _const_spec(cb.shape), _const_spec(wdn.shape)],
        out_specs=row,
        out_shape=jax.ShapeDtypeStruct((n, D_MODEL), F32),
        scratch_shapes=[pltpu.VMEM((N_FF_CHUNKS, HALO + tm, 2 * FF_CHUNK), F32),
                        pltpu.VMEM((tm, D_FF), BF16),
                        pltpu.VMEM((N_FF_CHUNKS, HALO, 2 * FF_CHUNK), F32)],
        compiler_params=_params(("arbitrary",)),
        name="ffn",
    )(x2, g, wup, cw, cb, wdn)


def _layer_params(l, w_in, a_q_g, a_k_g, b_q_a_g, b_kv_a_g, b_w_uq, b_w_ukv, b_qn_g, b_qr_g, b_kn_g,
                  b_kr_g, c_q_g, c_k_g, w_up, conv_w, conv_b):
    w = w_in[l]
    o_kpe = 3 * 512 + B_Q_RANK + B_KV_RANK
    o_qc = o_kpe + B_ROPE
    o_kc = o_qc + C_HEADS * C_HD
    o_vc = o_kc + C_KV_HEADS * C_HD
    o_g = o_vc + C_KV_HEADS * C_HD
    w1 = w[:, :o_kpe].astype(BF16)
    wqc = w[:, o_qc:o_kc].astype(BF16)
    wg = w[:, o_g:].astype(BF16)
    kpe_w = w[:, o_kpe:o_qc].astype(BF16)
    kc_w = w[:, o_kc:o_vc].astype(BF16)
    vc_w = w[:, o_vc:o_g].astype(BF16)
    z = lambda c: jnp.zeros((w.shape[0], c), BF16)
    r = B_ROPE // 2
    pad = LANES - B_NOPE - B_ROPE
    dup = lambda t: jnp.concatenate([t[:, :C_HD], t[:, :C_HD], t[:, C_HD:], t[:, C_HD:]], axis=1)
    ws = jnp.concatenate([
        z(B_NOPE), kpe_w, z(pad),
        z(B_NOPE), kpe_w[:, r:], kpe_w[:, :r], z(pad),
        dup(kc_w), dup(vc_w)], axis=1)

    uq = b_w_uq[l].reshape(B_Q_RANK, B_HEADS, B_NOPE + B_ROPE)
    nope, pe = uq[..., :B_NOPE], uq[..., B_NOPE:]
    zq = lambda c: jnp.zeros((B_Q_RANK, B_HEADS, c), uq.dtype)
    wuq = jnp.concatenate([
        jnp.concatenate([nope, pe, zq(pad)], axis=-1).reshape(B_Q_RANK, B_HEADS * LANES),
        jnp.concatenate([zq(B_NOPE), pe[..., r:], pe[..., :r], zq(pad)], axis=-1).reshape(B_Q_RANK, B_HEADS * LANES),
    ], axis=1).astype(BF16)

    ukv = b_w_ukv[l].reshape(B_KV_RANK, B_HEADS, B_NOPE + B_VD)
    zk = jnp.zeros((B_KV_RANK, B_HEADS, LANES - B_NOPE), ukv.dtype)
    wukv = jnp.concatenate([
        jnp.concatenate([ukv[..., :B_NOPE], zk], axis=-1).reshape(B_KV_RANK, B_HEADS * LANES),
        ukv[..., B_NOPE:].reshape(B_KV_RANK, B_HEADS * B_VD)], axis=1).astype(BF16)

    scale_a = A_HD ** -0.5 * LOG2E
    scale_b = (B_NOPE + B_ROPE) ** -0.5 * LOG2E
    scale_c = C_HD ** -0.5
    ga = jnp.stack([jnp.tile(a_q_g[l], 2 * A_HEADS) * scale_a, jnp.tile(a_k_g[l], 2 * A_HEADS)])
    glat = jnp.concatenate([b_q_a_g[l], b_kv_a_g[l]])[None, :]
    zl = lambda c: jnp.zeros((c,), F32)
    qr, kr = b_qr_g[l], b_kr_g[l]
    gqb = jnp.stack([
        jnp.tile(jnp.concatenate([b_qn_g[l], qr, zl(pad)]), B_HEADS),
        jnp.tile(jnp.concatenate([zl(B_NOPE), qr[r:], qr[:r], zl(pad)]), B_HEADS)]) * scale_b
    gkb = jnp.stack([
        jnp.concatenate([b_kn_g[l], zl(LANES - B_NOPE)]),
        jnp.concatenate([zl(B_NOPE), kr, zl(pad)]),
        jnp.concatenate([zl(B_NOPE), kr[r:], kr[:r], zl(pad)])])
    gc = jnp.concatenate([jnp.tile(c_q_g[l], C_HEADS) * scale_c, jnp.tile(c_k_g[l], 2 * C_KV_HEADS)])[None, :]

    return dict(w1=w1, wqc=wqc, ws=ws, wg=wg, wuq=wuq, wukv=wukv, ga=ga, glat=glat, gqb=gqb,
                gkb=gkb, gc=gc, wup=w_up[l].astype(BF16), cw=conv_w[l, :, 0, :], cb=conv_b[l][None, :])


def kernel(x, positions, rel_bias_table, ln_mix_g, w_in, a_q_g, a_k_g, a_lam_q1, a_lam_k1, a_lam_q2, a_lam_k2, a_subln_g, b_q_a_g, b_kv_a_g, b_w_uq, b_w_ukv, b_qn_g, b_qr_g, b_kn_g, b_kr_g, c_q_g, c_k_g, c_sinks, p_a, p_b, p_c, w_o, ln_ffn_g, w_up, conv_w, conv_b, w_down):
    batch, seq, d = x.shape
    n = batch * seq
    assert d == D_MODEL and seq % T_ATT == 0 and n % TM_IN == 0 and seq % TM_FFN == 0
    x2 = x.reshape(n, d)
    cos, sin = _rope_tables(positions.reshape(n, 1))
    bias_a, mask_b, bias_c = _bias_tiles(rel_bias_table)
    for l in range(DEPTH):
        lambda_init = 0.8 - 0.6 * math.exp(-0.3 * l)
        p = _layer_params(l, w_in, a_q_g, a_k_g, b_q_a_g, b_kv_a_g, b_w_uq, b_w_ukv, b_qn_g, b_qr_g,
                          b_kn_g, b_kr_g, c_q_g, c_k_g, w_up, conv_w, conv_b)
        gmix = ln_mix_g[l][None, :]
        qa, ka, va, qb, kb, vb, qc, kc, vc = _in_proj(
            x2, cos, sin, gmix, p["ga"], p["glat"], p["gqb"], p["gkb"], p["gc"],
            p["w1"], p["wqc"], p["ws"], p["wuq"], p["wukv"], batch, seq)
        lam = jnp.stack([a_lam_q1[l], a_lam_k1[l], a_lam_q2[l], a_lam_k2[l]])
        ya = _attn(qa, ka, va, bias_a, batch, seq, "diff", extra=(lam, a_subln_g[l][None, :]),
                   lambda_init=lambda_init)
        yb = _attn(qb, kb, vb, mask_b, batch, seq, "mla")
        yc = _swa(c_sinks[l], qc, kc, vc, bias_c, batch, seq)
        x2 = _merge(x2, ya, yb, yc, gmix, p["wg"], p_a[l].astype(BF16), p_b[l].astype(BF16),
                    p_c[l].astype(BF16), w_o[l].astype(BF16))
        x2 = _ffn(x2, ln_ffn_g[l][None, :], p["wup"], p["cw"], p["cb"], w_down[l].astype(BF16), seq)
    return x2.reshape(batch, seq, d)
```

```python
import functools
import math

import jax
import jax.numpy as jnp
from jax import lax
from jax.experimental import pallas as pl
from jax.experimental.pallas import tpu as pltpu

F32 = jnp.float32
BF16 = jnp.bfloat16

D_MODEL = 1024
DEPTH = 2
EPS = 1e-6
A_HEADS = 4
A_HD = 64
A_VD = 2 * A_HD
B_HEADS = 8
B_Q_RANK = 256
B_KV_RANK = 128
B_NOPE = 64
B_ROPE = 32
B_VD = 64
ROPE_THETA = 10000.0
C_HEADS = 8
C_KV_HEADS = 2
C_HD = 64
WINDOW = 128
N_BUCKETS = 32
MAX_DIST = 128
D_FF = 2816
CONV_W = 3

LANES = 128
HALF = LANES // 2
NEG = -1e30
LOG2E = math.log2(math.e)
ROW_CHUNK = 32
DEN_ROWS = 16
KIND_FAR, KIND_PREV, KIND_DIAG = 0, 1, 2
S_SLOTS = 4
P_SLOTS = 2
VMEM_LIMIT = 56 * 1024 * 1024

T_ATT = 512
TM_IN = 512
TM_MERGE = 512
TM_FFN = 512
FF_CHUNK = 256
N_FF_CHUNKS = D_FF // FF_CHUNK
HALO = 8

_SEG = {"qa": (0, 0, 512), "ka": (0, 512, 1024), "cq": (0, 1536, 1792),
        "ckv": (0, 1792, 1920), "qc": (1, 0, 512),
        "kpe": (2, 0, 128), "kpe_sw": (2, 128, 256), "kc": (2, 256, 512), "vc": (2, 512, 768)}


def _params(sem, vmem=VMEM_LIMIT):
    return pltpu.CompilerParams(dimension_semantics=sem, vmem_limit_bytes=vmem)


def _const_spec(shape):
    nd = len(shape)
    return pl.BlockSpec(shape, lambda *_: (0,) * nd, pipeline_mode=pl.Buffered(1))


def _lane_lo():
    return lax.broadcasted_iota(jnp.int32, (1, LANES), 1) < HALF


def _rope_kernel(pos_ref, inv_ref, sign_ref, cos_ref, sin_ref):
    ang = pos_ref[...].astype(F32) * inv_ref[...]
    cos_ref[...] = jnp.cos(ang)
    sin_ref[...] = jnp.sin(ang) * sign_ref[...]


def _rope_tables(pos_col):
    n = pos_col.shape[0]
    inv = 1.0 / (ROPE_THETA ** (jnp.arange(0, B_ROPE, 2, dtype=F32) / B_ROPE))
    z = jnp.zeros((B_NOPE,), F32)
    zp = jnp.zeros((LANES - B_NOPE - B_ROPE,), F32)
    inv_pat = jnp.concatenate([z, inv, inv, zp])[None, :]
    ones = jnp.ones((B_ROPE // 2,), F32)
    sign_pat = jnp.concatenate([z, -ones, ones, zp])[None, :]
    tm = 2048
    return pl.pallas_call(
        _rope_kernel,
        grid=(n // tm,),
        in_specs=[pl.BlockSpec((tm, 1), lambda i: (i, 0)),
                  pl.BlockSpec((1, LANES), lambda i: (0, 0)),
                  pl.BlockSpec((1, LANES), lambda i: (0, 0))],
        out_specs=[pl.BlockSpec((tm, LANES), lambda i: (i, 0))] * 2,
        out_shape=[jax.ShapeDtypeStruct((n, LANES), F32)] * 2,
        compiler_params=_params(("parallel",)),
        name="rope_tables",
    )(pos_col, inv_pat, sign_pat)


def _bucket(rel):
    n = jnp.maximum(rel, 0)
    max_exact = N_BUCKETS // 2
    nf = jnp.maximum(n, 1).astype(F32)
    large = max_exact + (jnp.log(nf / max_exact) / math.log(MAX_DIST / max_exact)
                         * (N_BUCKETS - max_exact)).astype(jnp.int32)
    large = jnp.minimum(large, N_BUCKETS - 1)
    return jnp.where(n < max_exact, n, large)


def _lookup(tab_ref, bucket, col):
    out = jnp.zeros(bucket.shape, F32)
    for k in range(N_BUCKETS):
        out = jnp.where(bucket == k, tab_ref[k, col], out)
    return out


def _bias_a_kernel(tab_ref, out_ref):
    h = pl.program_id(0)
    t = out_ref.shape[-1]
    key = lax.broadcasted_iota(jnp.int32, (t, t), 0)
    qry = lax.broadcasted_iota(jnp.int32, (t, t), 1)
    far = tab_ref[N_BUCKETS - 1, h]
    rel = qry - key
    out_ref[0, KIND_FAR] = jnp.zeros((t, t), F32)
    out_ref[0, KIND_PREV] = (_lookup(tab_ref, _bucket(rel + t), h) - far) * LOG2E
    out_ref[0, KIND_DIAG] = jnp.where(rel >= 0, (_lookup(tab_ref, _bucket(rel), h) - far) * LOG2E, NEG)


def _mask_kernel(out_ref):
    t = out_ref.shape[-1]
    key = lax.broadcasted_iota(jnp.int32, (t, t), 0)
    qry = lax.broadcasted_iota(jnp.int32, (t, t), 1)
    out_ref[0, KIND_FAR] = jnp.zeros((t, t), F32)
    out_ref[0, KIND_PREV] = jnp.zeros((t, t), F32)
    out_ref[0, KIND_DIAG] = jnp.where(key <= qry, 0.0, NEG)


def _bias_c_kernel(tab_ref, out_ref):
    h = pl.program_id(0)
    row = lax.broadcasted_iota(jnp.int32, (WINDOW, 2 * WINDOW), 0)
    col = lax.broadcasted_iota(jnp.int32, (WINDOW, 2 * WINDOW), 1)
    rel = row + WINDOW - col
    valid = (rel >= 0) & (rel < WINDOW)
    out_ref[0] = jnp.where(valid, _lookup(tab_ref, _bucket(rel), h + A_HEADS), NEG)


def _bias_tiles(table):
    smem = pl.BlockSpec(memory_space=pltpu.SMEM)
    bias_a = pl.pallas_call(
        _bias_a_kernel,
        grid=(A_HEADS,),
        in_specs=[smem],
        out_specs=pl.BlockSpec((1, 3, T_ATT, T_ATT), lambda h: (h, 0, 0, 0)),
        out_shape=jax.ShapeDtypeStruct((A_HEADS, 3, T_ATT, T_ATT), F32),
        compiler_params=_params(("parallel",)),
        name="bias_a",
    )(table)
    mask_b = pl.pallas_call(
        _mask_kernel,
        out_shape=jax.ShapeDtypeStruct((1, 3, T_ATT, T_ATT), F32),
        compiler_params=pltpu.CompilerParams(vmem_limit_bytes=VMEM_LIMIT),
        name="mask_b",
    )()
    bias_c = pl.pallas_call(
        _bias_c_kernel,
        grid=(C_HEADS,),
        in_specs=[smem],
        out_specs=pl.BlockSpec((1, WINDOW, 2 * WINDOW), lambda h: (h, 0, 0)),
        out_shape=jax.ShapeDtypeStruct((C_HEADS, WINDOW, 2 * WINDOW), F32),
        compiler_params=_params(("parallel",)),
        name="bias_c",
    )(table)
    return bias_a, mask_b, bias_c


def _rms(t, width):
    return lax.rsqrt(jnp.sum(t * t, axis=-1, keepdims=True) / width + EPS)


def _in_kernel(x_ref, cos_ref, sin_ref, gmix_ref, ga_ref, glat_ref, gqb_ref, gkb_ref, gc_ref,
               w1_ref, wqc_ref, ws_ref, wuq_ref, wukv_ref, wvat_ref, wvbt_ref,
               qa_ref, ka_ref, va_ref, qb_ref, kb_ref, vb_ref, qc_ref, kc_ref, vc_ref):
    x = x_ref[...]
    h = (x * _rms(x, D_MODEL) * gmix_ref[...]).astype(BF16)
    lo = _lane_lo()
    w_refs = (w1_ref, wqc_ref, ws_ref)

    def proj(name):
        which, a, b = _SEG[name]
        return jnp.dot(h, w_refs[which][:, a:b], preferred_element_type=F32)

    def norm_halves(t, g, out_ref):
        for j in range(t.shape[1] // LANES):
            sl = slice(j * LANES, (j + 1) * LANES)
            tj = t[:, sl]
            sq = tj * tj
            s_lo = jnp.sum(jnp.where(lo, sq, 0.0), axis=-1, keepdims=True)
            s_hi = jnp.sum(jnp.where(lo, 0.0, sq), axis=-1, keepdims=True)
            r = jnp.where(lo, lax.rsqrt(s_lo / HALF + EPS), lax.rsqrt(s_hi / HALF + EPS))
            out_ref[:, sl] = (tj * r * g[:, sl]).astype(out_ref.dtype)

    glat = glat_ref[...]
    cq = proj("cq")
    ckv = proj("ckv")
    kpe = proj("kpe")
    kpe_sw = proj("kpe_sw")
    cqn = (cq * _rms(cq, B_Q_RANK) * glat[:, :B_Q_RANK]).astype(BF16)
    ckvn = (ckv * _rms(ckv, B_KV_RANK) * glat[:, B_Q_RANK:]).astype(BF16)
    uq = jnp.dot(cqn, wuq_ref[...], preferred_element_type=F32)
    ukv = jnp.dot(ckvn, wukv_ref[...], preferred_element_type=F32)
    qa = proj("qa")
    ka = proj("ka")
    qc = proj("qc")
    kc = proj("kc")
    vc = proj("vc")
    nt = (((1,), (1,)), ((), ()))
    va_ref[...] = lax.dot_general(wvat_ref[...], h, nt, preferred_element_type=F32).astype(BF16)
    vb_ref[...] = lax.dot_general(wvbt_ref[...], ckvn, nt, preferred_element_type=F32).astype(BF16)

    cos = cos_ref[...]
    sin = sin_ref[...]
    hw = B_HEADS * LANES
    for j in range(B_HEADS):
        sl = slice(j * LANES, (j + 1) * LANES)
        raw = uq[:, sl]
        raw_sw = uq[:, hw + j * LANES: hw + (j + 1) * LANES]
        sq = raw * raw
        s_n = jnp.sum(jnp.where(lo, sq, 0.0), axis=-1, keepdims=True)
        s_r = jnp.sum(jnp.where(lo, 0.0, sq), axis=-1, keepdims=True)
        r = jnp.where(lo, lax.rsqrt(s_n / B_NOPE + EPS), lax.rsqrt(s_r / B_ROPE + EPS))
        out = r * (raw * gqb_ref[0:1, sl] * cos + raw_sw * gqb_ref[1:2, sl] * sin)
        qb_ref[:, sl] = out.astype(BF16)

    kpe_out = _rms(kpe, B_ROPE) * (kpe * gkb_ref[1:2, :] * cos + kpe_sw * gkb_ref[2:3, :] * sin)
    for j in range(B_HEADS):
        sl = slice(j * LANES, (j + 1) * LANES)
        raw = ukv[:, sl]
        kb_ref[:, sl] = (raw * _rms(raw, B_NOPE) * gkb_ref[0:1, :] + kpe_out).astype(BF16)

    norm_halves(qa, ga_ref[0:1, :], qa_ref)
    norm_halves(ka, ga_ref[1:2, :], ka_ref)
    gc = gc_ref[...]
    norm_halves(qc, gc[:, :C_HEADS * C_HD], qc_ref)
    norm_halves(kc, gc[:, C_HEADS * C_HD:], kc_ref)
    vc_ref[...] = vc.astype(BF16)


def _in_proj(x2, cos, sin, gmix, ga, glat, gqb, gkb, gc, w1, wqc, ws, wuq, wukv, wvat, wvbt, batch, seq):
    n = x2.shape[0]
    tm = TM_IN
    nps = seq // tm
    row = lambda w: pl.BlockSpec((tm, w), lambda i: (i, 0))
    outs = (("qa", 512), ("ka", 512), ("va", None), ("qb", 1024), ("kb", 1024), ("vb", None),
            ("qc", 512), ("kc", 256), ("vc", 256))
    vt_spec = pl.BlockSpec((None, 512, tm), lambda i: (i // nps, 0, i % nps))
    vt_shape = jax.ShapeDtypeStruct((batch, 512, seq), BF16)
    return pl.pallas_call(
        _in_kernel,
        grid=(n // tm,),
        in_specs=[row(D_MODEL), row(LANES), row(LANES),
                  _const_spec(gmix.shape), _const_spec(ga.shape), _const_spec(glat.shape),
                  _const_spec(gqb.shape), _const_spec(gkb.shape), _const_spec(gc.shape),
                  _const_spec(w1.shape), _const_spec(wqc.shape), _const_spec(ws.shape),
                  _const_spec(wuq.shape), _const_spec(wukv.shape),
                  _const_spec(wvat.shape), _const_spec(wvbt.shape)],
        out_specs=[vt_spec if w is None else row(w) for _, w in outs],
        out_shape=[vt_shape if w is None else jax.ShapeDtypeStruct((n, w), BF16) for _, w in outs],
        compiler_params=_params(("parallel",)),
        name="in_proj",
    )(x2, cos, sin, gmix, ga, glat, gqb, gkb, gc, w1, wqc, ws, wuq, wukv, wvat, wvbt)


def _attn_items(nq):
    return [(qi, ki, KIND_DIAG if ki == qi else KIND_PREV if ki == qi - 1 else KIND_FAR)
            for qi in range(nq) for ki in range(qi + 1)]


def _attn_kernel(*refs, mode, lambda_init, nq):
    if mode == "diff":
        q_ref, k_ref, vt_ref, bias_ref, lam_ref, subg_ref, o_ref, s_scr, p_scr, acc_scr, m_scr, mt_scr, al_scr = refs
    else:
        q_ref, k_ref, vt_ref, bias_ref, o_ref, s_scr, p_scr, acc_scr, m_scr, mt_scr, al_scr = refs
    t = T_ATT
    items = _attn_items(nq)
    lo = _lane_lo()
    ones = jnp.ones((DEN_ROWS, t), BF16)
    if mode == "diff":
        lv = lam_ref[...]
        lam = (jnp.exp(jnp.sum(lv[0:1] * lv[1:2], axis=-1, keepdims=True))
               - jnp.exp(jnp.sum(lv[2:3] * lv[3:4], axis=-1, keepdims=True)) + lambda_init)
        out_gain = subg_ref[...] * (1.0 - lambda_init)

    def tile(idx):
        return slice(idx * t, (idx + 1) * t)

    def scores(n):
        qi, ki, kind = items[n]
        q = q_ref[tile(qi), :]
        k = k_ref[tile(ki), :]
        for j in range(2):
            c = 2 * (n % S_SLOTS) + j
            if mode == "diff":
                zero = jnp.zeros_like(q)
                qj = jnp.where(lo, q, zero) if j == 0 else jnp.where(lo, zero, q)
                kj = k
            else:
                qj = q[:, j * LANES:(j + 1) * LANES]
                kj = k[:, j * LANES:(j + 1) * LANES]
            s = lax.dot_general(kj, qj, (((1,), (1,)), ((), ())), preferred_element_type=F32)
            if kind == KIND_DIAG or (kind == KIND_PREV and mode == "diff"):
                s = s + bias_ref[kind]
            s_scr[c] = s
            mt_scr[c] = jnp.max(s, axis=0, keepdims=True)

    def softmax(n):
        first = items[n][1] == 0
        for j in range(2):
            c = 2 * (n % S_SLOTS) + j
            pc = 2 * (n % P_SLOTS) + j
            if first:
                m_new = mt_scr[c]
            else:
                m_old = m_scr[j]
                m_new = jnp.maximum(m_old, mt_scr[c])
                al_scr[pc] = jnp.exp2(m_old - m_new)
            m_scr[j] = m_new
            for r0 in range(0, t, ROW_CHUNK):
                d = s_scr[c, r0:r0 + ROW_CHUNK, :] - m_new
                p_scr[pc, r0:r0 + ROW_CHUNK, :] = jnp.exp2(d.astype(BF16))

    def values(n):
        qi, ki, _ = items[n]
        vt = jnp.concatenate([vt_ref[:, tile(ki)], ones], axis=0)
        accs = []
        for j in range(2):
            pc = 2 * (n % P_SLOTS) + j
            acc = jnp.dot(vt, p_scr[pc], preferred_element_type=F32)
            if ki > 0:
                acc = al_scr[pc] * acc_scr[j] + acc
            if ki < qi:
                acc_scr[j] = acc
            accs.append(acc)
        if ki < qi:
            return
        o0, o1 = (acc[:LANES, :] / acc[LANES:LANES + 1, :] for acc in accs)
        if mode == "diff":
            o = o0 - lam * o1
            o = o * lax.rsqrt(jnp.sum(o * o, axis=0, keepdims=True) / A_VD + EPS)
            o = o.T * out_gain
        else:
            upper = lax.broadcasted_iota(jnp.int32, (LANES, 1), 0) < HALF
            o = jnp.where(upper, o0, o1).T
        o_ref[tile(qi), :] = o.astype(o_ref.dtype)

    scores(0)
    scores(1)
    for n in range(len(items)):
        if n + 2 < len(items):
            scores(n + 2)
        softmax(n)
        if n > 0:
            values(n - 1)
    values(len(items) - 1)


def _attn(q, k, vt, bias, batch, seq, mode, extra=(), lambda_init=0.0):
    n = q.shape[0]
    t = T_ATT
    qw = LANES if mode == "diff" else 2 * LANES
    groups = q.shape[1] // qw
    per_head = bias.shape[0] > 1
    in_specs = [pl.BlockSpec((seq, qw), lambda b, g: (b, g)),
                pl.BlockSpec((seq, qw), lambda b, g: (b, g)),
                pl.BlockSpec((None, LANES, seq), lambda b, g: (b, g, 0)),
                pl.BlockSpec((None, 3, t, t), lambda b, g: (g if per_head else 0, 0, 0, 0))]
    in_specs += [pl.BlockSpec(e.shape, lambda b, g: (0, 0)) for e in extra]
    return pl.pallas_call(
        functools.partial(_attn_kernel, mode=mode, lambda_init=lambda_init, nq=seq // t),
        grid=(batch, groups),
        in_specs=in_specs,
        out_specs=pl.BlockSpec((seq, LANES), lambda b, g: (b, g)),
        out_shape=jax.ShapeDtypeStruct((n, groups * LANES), BF16),
        scratch_shapes=[pltpu.VMEM((2 * S_SLOTS, t, t), F32), pltpu.VMEM((2 * P_SLOTS, t, t), BF16),
                        pltpu.VMEM((2, LANES + DEN_ROWS, t), F32), pltpu.VMEM((2, 1, t), F32),
                        pltpu.VMEM((2 * S_SLOTS, 1, t), F32), pltpu.VMEM((2 * P_SLOTS, 1, t), F32)],
        compiler_params=_params(("parallel", "parallel")),
        name="attn_" + mode,
    )(q, k, vt, bias, *extra)


def _swa_kernel(sink_ref, q_ref, kp_ref, kc_ref, vp_ref, vc_ref, bias_ref, o_ref):
    nb = pl.program_id(1)
    lo = _lane_lo()
    col = lax.broadcasted_iota(jnp.int32, (WINDOW, 2 * WINDOW), 1)
    keep = (col >= WINDOW) | (nb > 0)
    grp = C_HEADS // C_KV_HEADS
    scores = []
    for head in range(C_HEADS):
        hp, j = divmod(head, 2)
        ksl = slice((head // grp) * LANES, (head // grp + 1) * LANES)
        q = q_ref[:, hp * LANES:(hp + 1) * LANES]
        zero = jnp.zeros_like(q)
        k = jnp.concatenate([kp_ref[:, ksl], kc_ref[:, ksl]], axis=0)
        qj = jnp.where(lo, q, zero) if j == 0 else jnp.where(lo, zero, q)
        scores.append(lax.dot_general(qj, k, (((1,), (1,)), ((), ())), preferred_element_type=F32))
    probs = []
    for head in range(C_HEADS):
        s = jnp.where(keep, scores[head] + bias_ref[head], NEG)
        sink = sink_ref[head]
        m = jnp.maximum(jnp.max(s, axis=-1, keepdims=True), sink)
        p = jnp.exp(s - m)
        den = jnp.sum(p, axis=-1, keepdims=True) + jnp.exp(sink - m)
        probs.append((p.astype(BF16), den))
    for hp in range(C_HEADS // 2):
        ksl = slice(((2 * hp) // grp) * LANES, ((2 * hp) // grp + 1) * LANES)
        v = jnp.concatenate([vp_ref[:, ksl], vc_ref[:, ksl]], axis=0)
        outs = [jnp.dot(p, v, preferred_element_type=F32) / den for p, den in probs[2 * hp:2 * hp + 2]]
        o_ref[:, hp * LANES:(hp + 1) * LANES] = jnp.where(lo, outs[0], outs[1]).astype(o_ref.dtype)


def _swa(sinks, q, k, v, bias, batch, seq):
    n = q.shape[0]
    nb = seq // WINDOW
    cur = lambda b, i: (b * nb + i, 0)
    prev = lambda b, i: (b * nb + jnp.maximum(i - 1, 0), 0)
    kw = k.shape[1]
    return pl.pallas_call(
        _swa_kernel,
        grid=(batch, nb),
        in_specs=[pl.BlockSpec(memory_space=pltpu.SMEM),
                  pl.BlockSpec((WINDOW, q.shape[1]), cur),
                  pl.BlockSpec((WINDOW, kw), prev), pl.BlockSpec((WINDOW, kw), cur),
                  pl.BlockSpec((WINDOW, kw), prev), pl.BlockSpec((WINDOW, kw), cur),
                  pl.BlockSpec(bias.shape, lambda b, i: (0, 0, 0))],
        out_specs=pl.BlockSpec((WINDOW, q.shape[1]), cur),
        out_shape=jax.ShapeDtypeStruct((n, q.shape[1]), BF16),
        compiler_params=_params(("parallel", "arbitrary")),
        name="swa",
    )(sinks, q, k, k, v, v, bias)


def _merge_kernel(x_ref, ya_ref, yb_ref, yc_ref, gmix_ref, wg_ref, pa_ref, pb_ref, pc_ref, wo_ref, o_ref):
    x = x_ref[...]
    h = (x * _rms(x, D_MODEL) * gmix_ref[...]).astype(BF16)
    merged = None
    for j, (y_ref, p_ref) in enumerate(((ya_ref, pa_ref), (yb_ref, pb_ref), (yc_ref, pc_ref))):
        gate = jnp.dot(h, wg_ref[:, j * D_MODEL:(j + 1) * D_MODEL], preferred_element_type=F32)
        term = jax.nn.sigmoid(gate) * jnp.dot(y_ref[...], p_ref[...], preferred_element_type=F32)
        merged = term if merged is None else merged + term
    o_ref[...] = x + jnp.dot(merged.astype(BF16), wo_ref[...], preferred_element_type=F32)


def _merge(x2, ya, yb, yc, gmix, wg, pa, pb, pc, wo):
    n = x2.shape[0]
    tm = TM_MERGE
    row = lambda w: pl.BlockSpec((tm, w), lambda i: (i, 0))
    return pl.pallas_call(
        _merge_kernel,
        grid=(n // tm,),
        in_specs=[row(D_MODEL), row(ya.shape[1]), row(yb.shape[1]), row(yc.shape[1]),
                  _const_spec(gmix.shape), _const_spec(wg.shape), _const_spec(pa.shape),
                  _const_spec(pb.shape), _const_spec(pc.shape), _const_spec(wo.shape)],
        out_specs=row(D_MODEL),
        out_shape=jax.ShapeDtypeStruct((n, D_MODEL), F32),
        compiler_params=_params(("parallel",)),
        name="merge",
    )(x2, ya, yb, yc, gmix, wg, pa, pb, pc, wo)


def _ffn_kernel(x_ref, g_ref, wup_ref, cw_ref, cb_ref, wdn_ref, o_ref, ubuf, act, carry, *, tiles_per_seq):
    tm = x_ref.shape[0]
    fc = FF_CHUNK

    @pl.when(pl.program_id(0) % tiles_per_seq == 0)
    def _():
        carry[...] = jnp.zeros_like(carry)

    x = x_ref[...]
    h = (x * _rms(x, D_MODEL) * g_ref[...]).astype(BF16)

    def cols(ref, rows, j):
        return jnp.concatenate([ref[rows, j * fc:(j + 1) * fc], ref[rows, D_FF + j * fc:D_FF + (j + 1) * fc]],
                               axis=1)

    def up(j):
        for half, base in enumerate((j * fc, D_FF + j * fc)):
            ubuf[j, HALO:HALO + tm, half * fc:(half + 1) * fc] = jnp.dot(
                h, wup_ref[:, base:base + fc], preferred_element_type=F32)

    def conv_act(j):
        ubuf[j, 0:HALO, :] = carry[j]
        carry[j] = ubuf[j, tm:tm + HALO, :]
        y = cols(cb_ref, slice(0, 1), j)
        for tap in range(CONV_W):
            shift = CONV_W - 1 - tap
            y = y + cols(cw_ref, slice(tap, tap + 1), j) * ubuf[j, HALO - shift:HALO - shift + tm, :]
        gate = y[:, :fc]
        act[:, j * fc:(j + 1) * fc] = (gate * jax.nn.sigmoid(gate) * y[:, fc:]).astype(BF16)

    up(0)
    for j in range(N_FF_CHUNKS):
        if j + 1 < N_FF_CHUNKS:
            up(j + 1)
        conv_act(j)
    o_ref[...] = x + jnp.dot(act[...], wdn_ref[...], preferred_element_type=F32)


def _ffn(x2, g, wup, cw, cb, wdn, seq):
    n = x2.shape[0]
    tm = TM_FFN
    row = pl.BlockSpec((tm, D_MODEL), lambda i: (i, 0))
    return pl.pallas_call(
        functools.partial(_ffn_kernel, tiles_per_seq=seq // tm),
        grid=(n // tm,),
        in_specs=[row, _const_spec(g.shape), _const_spec(wup.shape), _const_spec(cw.shape),
                  _const_spec(cb.shape), _const_spec(wdn.shape)],
        out_specs=row,
        out_shape=jax.ShapeDtypeStruct((n, D_MODEL), F32),
        scratch_shapes=[pltpu.VMEM((N_FF_CHUNKS, HALO + tm, 2 * FF_CHUNK), F32),
                        pltpu.VMEM((tm, D_FF), BF16),
                        pltpu.VMEM((N_FF_CHUNKS, HALO, 2 * FF_CHUNK), F32)],
        compiler_params=_params(("arbitrary",)),
        name="ffn",
    )(x2, g, wup, cw, cb, wdn)


def _layer_params(l, w_in, a_q_g, a_k_g, b_q_a_g, b_kv_a_g, b_w_uq, b_w_ukv, b_qn_g, b_qr_g, b_kn_g,
                  b_kr_g, c_q_g, c_k_g, w_up, conv_w, conv_b):
    w = w_in[l]
    o_kpe = 3 * 512 + B_Q_RANK + B_KV_RANK
    o_qc = o_kpe + B_ROPE
    o_kc = o_qc + C_HEADS * C_HD
    o_vc = o_kc + C_KV_HEADS * C_HD
    o_g = o_vc + C_KV_HEADS * C_HD
    w1 = w[:, :o_kpe].astype(BF16)
    wqc = w[:, o_qc:o_kc].astype(BF16)
    wg = w[:, o_g:].astype(BF16)
    kpe_w = w[:, o_kpe:o_qc].astype(BF16)
    kc_w = w[:, o_kc:o_vc].astype(BF16)
    vc_w = w[:, o_vc:o_g].astype(BF16)
    z = lambda c: jnp.zeros((w.shape[0], c), BF16)
    r = B_ROPE // 2
    pad = LANES - B_NOPE - B_ROPE
    dup = lambda t: jnp.concatenate([t[:, :C_HD], t[:, :C_HD], t[:, C_HD:], t[:, C_HD:]], axis=1)
    ws = jnp.concatenate([
        z(B_NOPE), kpe_w, z(pad),
        z(B_NOPE), kpe_w[:, r:], kpe_w[:, :r], z(pad),
        dup(kc_w), dup(vc_w)], axis=1)

    uq = b_w_uq[l].reshape(B_Q_RANK, B_HEADS, B_NOPE + B_ROPE)
    nope, pe = uq[..., :B_NOPE], uq[..., B_NOPE:]
    zq = lambda c: jnp.zeros((B_Q_RANK, B_HEADS, c), uq.dtype)
    wuq = jnp.concatenate([
        jnp.concatenate([nope, pe, zq(pad)], axis=-1).reshape(B_Q_RANK, B_HEADS * LANES),
        jnp.concatenate([zq(B_NOPE), pe[..., r:], pe[..., :r], zq(pad)], axis=-1).reshape(B_Q_RANK, B_HEADS * LANES),
    ], axis=1).astype(BF16)

    ukv = b_w_ukv[l].reshape(B_KV_RANK, B_HEADS, B_NOPE + B_VD)
    zk = jnp.zeros((B_KV_RANK, B_HEADS, LANES - B_NOPE), ukv.dtype)
    wukv = jnp.concatenate([ukv[..., :B_NOPE], zk], axis=-1).reshape(B_KV_RANK, B_HEADS * LANES).astype(BF16)
    wvbt = ukv[..., B_NOPE:].reshape(B_KV_RANK, B_HEADS * B_VD).T.astype(BF16)
    wvat = w[:, 2 * 512:3 * 512].T.astype(BF16)

    scale_a = A_HD ** -0.5 * LOG2E
    scale_b = (B_NOPE + B_ROPE) ** -0.5 * LOG2E
    scale_c = C_HD ** -0.5
    ga = jnp.stack([jnp.tile(a_q_g[l], 2 * A_HEADS) * scale_a, jnp.tile(a_k_g[l], 2 * A_HEADS)])
    glat = jnp.concatenate([b_q_a_g[l], b_kv_a_g[l]])[None, :]
    zl = lambda c: jnp.zeros((c,), F32)
    qr, kr = b_qr_g[l], b_kr_g[l]
    gqb = jnp.stack([
        jnp.tile(jnp.concatenate([b_qn_g[l], qr, zl(pad)]), B_HEADS),
        jnp.tile(jnp.concatenate([zl(B_NOPE), qr[r:], qr[:r], zl(pad)]), B_HEADS)]) * scale_b
    gkb = jnp.stack([
        jnp.concatenate([b_kn_g[l], zl(LANES - B_NOPE)]),
        jnp.concatenate([zl(B_NOPE), kr, zl(pad)]),
        jnp.concatenate([zl(B_NOPE), kr[r:], kr[:r], zl(pad)])])
    gc = jnp.concatenate([jnp.tile(c_q_g[l], C_HEADS) * scale_c, jnp.tile(c_k_g[l], 2 * C_KV_HEADS)])[None, :]

    return dict(w1=w1, wqc=wqc, ws=ws, wg=wg, wuq=wuq, wukv=wukv, wvat=wvat, wvbt=wvbt, ga=ga, glat=glat,
                gqb=gqb, gkb=gkb, gc=gc, wup=w_up[l].astype(BF16), cw=conv_w[l, :, 0, :], cb=conv_b[l][None, :])


def kernel(x, positions, rel_bias_table, ln_mix_g, w_in, a_q_g, a_k_g, a_lam_q1, a_lam_k1, a_lam_q2, a_lam_k2, a_subln_g, b_q_a_g, b_kv_a_g, b_w_uq, b_w_ukv, b_qn_g, b_qr_g, b_kn_g, b_kr_g, c_q_g, c_k_g, c_sinks, p_a, p_b, p_c, w_o, ln_ffn_g, w_up, conv_w, conv_b, w_down):
    batch, seq, d = x.shape
    n = batch * seq
    assert d == D_MODEL and seq % T_ATT == 0 and n % TM_IN == 0 and seq % TM_FFN == 0
    x2 = x.reshape(n, d)
    cos, sin = _rope_tables(positions.reshape(n, 1))
    bias_a, mask_b, bias_c = _bias_tiles(rel_bias_table)
    for l in range(DEPTH):
        lambda_init = 0.8 - 0.6 * math.exp(-0.3 * l)
        p = _layer_params(l, w_in, a_q_g, a_k_g, b_q_a_g, b_kv_a_g, b_w_uq, b_w_ukv, b_qn_g, b_qr_g,
                          b_kn_g, b_kr_g, c_q_g, c_k_g, w_up, conv_w, conv_b)
        gmix = ln_mix_g[l][None, :]
        qa, ka, va, qb, kb, vb, qc, kc, vc = _in_proj(
            x2, cos, sin, gmix, p["ga"], p["glat"], p["gqb"], p["gkb"], p["gc"],
            p["w1"], p["wqc"], p["ws"], p["wuq"], p["wukv"], p["wvat"], p["wvbt"], batch, seq)
        lam = jnp.stack([a_lam_q1[l], a_lam_k1[l], a_lam_q2[l], a_lam_k2[l]])
        ya = _attn(qa, ka, va, bias_a, batch, seq, "diff", extra=(lam, a_subln_g[l][None, :]),
                   lambda_init=lambda_init)
        yb = _attn(qb, kb, vb, mask_b, batch, seq, "mla")
        yc = _swa(c_sinks[l], qc, kc, vc, bias_c, batch, seq)
        x2 = _merge(x2, ya, yb, yc, gmix, p["wg"], p_a[l].astype(BF16), p_b[l].astype(BF16),
                    p_c[l].astype(BF16), w_o[l].astype(BF16))
        x2 = _ffn(x2, ln_ffn_g[l][None, :], p["wup"], p["cw"], p["cb"], w_down[l].astype(BF16), seq)
    return x2.reshape(batch, seq, d)
```

```python
import functools
import math

import jax
import jax.numpy as jnp
import numpy as np
from jax import lax
from jax.experimental import pallas as pl
from jax.experimental.pallas import tpu as pltpu

F32 = jnp.float32
BF16 = jnp.bfloat16

D_MODEL = 1024
DEPTH = 2
EPS = 1e-6
A_HEADS = 4
A_HD = 64
A_VD = 2 * A_HD
B_HEADS = 8
B_Q_RANK = 256
B_KV_RANK = 128
B_NOPE = 64
B_ROPE = 32
B_VD = 64
ROPE_THETA = 10000.0
C_HEADS = 8
C_KV_HEADS = 2
C_HD = 64
WINDOW = 128
N_BUCKETS = 32
MAX_DIST = 128
D_FF = 2816
CONV_W = 3

LANES = 128
HALF = LANES // 2
NEG = -1e30
LOG2E = math.log2(math.e)
ROW_CHUNK = 32
DEN_ROWS = 16
KIND_FAR, KIND_PREV, KIND_DIAG = 0, 1, 2
S_SLOTS = 4
P_SLOTS = 2
ITEMS_PER_TRIP = 12
VMEM_LIMIT = 56 * 1024 * 1024

T_ATT = 512
TM_IN = 512
TM_MERGE = 512
TM_FFN = 512
FF_CHUNK = 256
N_FF_CHUNKS = D_FF // FF_CHUNK
HALO = 8

_SEG = {"qa": (0, 0, 512), "ka": (0, 512, 1024), "cq": (0, 1536, 1792),
        "ckv": (0, 1792, 1920), "qc": (1, 0, 512),
        "kpe": (2, 0, 128), "kpe_sw": (2, 128, 256), "kc": (2, 256, 512), "vc": (2, 512, 768)}


def _params(sem, vmem=VMEM_LIMIT):
    return pltpu.CompilerParams(dimension_semantics=sem, vmem_limit_bytes=vmem)


def _const_spec(shape):
    nd = len(shape)
    return pl.BlockSpec(shape, lambda *_: (0,) * nd, pipeline_mode=pl.Buffered(1))


def _lane_lo():
    return lax.broadcasted_iota(jnp.int32, (1, LANES), 1) < HALF


def _cast_kernel(w_ref, o_ref):
    o_ref[...] = w_ref[...].astype(o_ref.dtype)


def _to_bf16(w, layer, row_block):
    _, rows, cols = w.shape
    assert rows % row_block == 0
    return pl.pallas_call(
        _cast_kernel,
        grid=(rows // row_block,),
        in_specs=[pl.BlockSpec((None, row_block, cols), lambda i: (layer, i, 0))],
        out_specs=pl.BlockSpec((row_block, cols), lambda i: (i, 0)),
        out_shape=jax.ShapeDtypeStruct((rows, cols), BF16),
        compiler_params=_params(("parallel",)),
        name="to_bf16",
    )(w)


def _rope_kernel(pos_ref, inv_ref, sign_ref, cos_ref, sin_ref):
    ang = pos_ref[...].astype(F32) * inv_ref[...]
    cos_ref[...] = jnp.cos(ang)
    sin_ref[...] = jnp.sin(ang) * sign_ref[...]


def _rope_tables(pos_col):
    n = pos_col.shape[0]
    inv = 1.0 / (ROPE_THETA ** (jnp.arange(0, B_ROPE, 2, dtype=F32) / B_ROPE))
    z = jnp.zeros((B_NOPE,), F32)
    zp = jnp.zeros((LANES - B_NOPE - B_ROPE,), F32)
    inv_pat = jnp.concatenate([z, inv, inv, zp])[None, :]
    ones = jnp.ones((B_ROPE // 2,), F32)
    sign_pat = jnp.concatenate([z, -ones, ones, zp])[None, :]
    tm = 2048
    return pl.pallas_call(
        _rope_kernel,
        grid=(n // tm,),
        in_specs=[pl.BlockSpec((tm, 1), lambda i: (i, 0)),
                  pl.BlockSpec((1, LANES), lambda i: (0, 0)),
                  pl.BlockSpec((1, LANES), lambda i: (0, 0))],
        out_specs=[pl.BlockSpec((tm, LANES), lambda i: (i, 0))] * 2,
        out_shape=[jax.ShapeDtypeStruct((n, LANES), F32)] * 2,
        compiler_params=_params(("parallel",)),
        name="rope_tables",
    )(pos_col, inv_pat, sign_pat)


def _bucket(rel):
    n = jnp.maximum(rel, 0)
    max_exact = N_BUCKETS // 2
    nf = jnp.maximum(n, 1).astype(F32)
    large = max_exact + (jnp.log(nf / max_exact) / math.log(MAX_DIST / max_exact)
                         * (N_BUCKETS - max_exact)).astype(jnp.int32)
    large = jnp.minimum(large, N_BUCKETS - 1)
    return jnp.where(n < max_exact, n, large)


def _lookup(tab_ref, bucket, col):
    out = jnp.zeros(bucket.shape, F32)
    for k in range(N_BUCKETS):
        out = jnp.where(bucket == k, tab_ref[k, col], out)
    return out


def _bias_a_kernel(tab_ref, out_ref):
    h = pl.program_id(0)
    t = out_ref.shape[-1]
    key = lax.broadcasted_iota(jnp.int32, (t, t), 0)
    qry = lax.broadcasted_iota(jnp.int32, (t, t), 1)
    far = tab_ref[N_BUCKETS - 1, h]
    rel = qry - key
    out_ref[0, KIND_FAR] = jnp.zeros((t, t), F32)
    out_ref[0, KIND_PREV] = (_lookup(tab_ref, _bucket(rel + t), h) - far) * LOG2E
    out_ref[0, KIND_DIAG] = jnp.where(rel >= 0, (_lookup(tab_ref, _bucket(rel), h) - far) * LOG2E, NEG)


def _mask_kernel(out_ref):
    t = out_ref.shape[-1]
    key = lax.broadcasted_iota(jnp.int32, (t, t), 0)
    qry = lax.broadcasted_iota(jnp.int32, (t, t), 1)
    out_ref[0, KIND_FAR] = jnp.zeros((t, t), F32)
    out_ref[0, KIND_PREV] = jnp.zeros((t, t), F32)
    out_ref[0, KIND_DIAG] = jnp.where(key <= qry, 0.0, NEG)


def _bias_c_kernel(tab_ref, out_ref):
    h = pl.program_id(0)
    row = lax.broadcasted_iota(jnp.int32, (WINDOW, 2 * WINDOW), 0)
    col = lax.broadcasted_iota(jnp.int32, (WINDOW, 2 * WINDOW), 1)
    rel = row + WINDOW - col
    valid = (rel >= 0) & (rel < WINDOW)
    out_ref[0] = jnp.where(valid, _lookup(tab_ref, _bucket(rel), h + A_HEADS), NEG)


def _bias_tiles(table):
    smem = pl.BlockSpec(memory_space=pltpu.SMEM)
    bias_a = pl.pallas_call(
        _bias_a_kernel,
        grid=(A_HEADS,),
        in_specs=[smem],
        out_specs=pl.BlockSpec((1, 3, T_ATT, T_ATT), lambda h: (h, 0, 0, 0)),
        out_shape=jax.ShapeDtypeStruct((A_HEADS, 3, T_ATT, T_ATT), F32),
        compiler_params=_params(("parallel",)),
        name="bias_a",
    )(table)
    mask_b = pl.pallas_call(
        _mask_kernel,
        out_shape=jax.ShapeDtypeStruct((1, 3, T_ATT, T_ATT), F32),
        compiler_params=pltpu.CompilerParams(vmem_limit_bytes=VMEM_LIMIT),
        name="mask_b",
    )()
    bias_c = pl.pallas_call(
        _bias_c_kernel,
        grid=(C_HEADS,),
        in_specs=[smem],
        out_specs=pl.BlockSpec((1, WINDOW, 2 * WINDOW), lambda h: (h, 0, 0)),
        out_shape=jax.ShapeDtypeStruct((C_HEADS, WINDOW, 2 * WINDOW), F32),
        compiler_params=_params(("parallel",)),
        name="bias_c",
    )(table)
    return bias_a, mask_b, bias_c


def _rms(t, width):
    return lax.rsqrt(jnp.sum(t * t, axis=-1, keepdims=True) / width + EPS)


def _in_kernel(x_ref, cos_ref, sin_ref, gmix_ref, ga_ref, glat_ref, gqb_ref, gkb_ref, gc_ref,
               w1_ref, wqc_ref, ws_ref, wuq_ref, wukv_ref, wvat_ref, wvbt_ref,
               qa_ref, ka_ref, va_ref, qb_ref, kb_ref, vb_ref, qc_ref, kc_ref, vc_ref):
    x = x_ref[...]
    h = (x * _rms(x, D_MODEL) * gmix_ref[...]).astype(BF16)
    lo = _lane_lo()
    w_refs = (w1_ref, wqc_ref, ws_ref)

    def proj(name):
        which, a, b = _SEG[name]
        return jnp.dot(h, w_refs[which][:, a:b], preferred_element_type=F32)

    def norm_halves(t, g, out_ref):
        for j in range(t.shape[1] // LANES):
            sl = slice(j * LANES, (j + 1) * LANES)
            tj = t[:, sl]
            sq = tj * tj
            s_lo = jnp.sum(jnp.where(lo, sq, 0.0), axis=-1, keepdims=True)
            s_hi = jnp.sum(jnp.where(lo, 0.0, sq), axis=-1, keepdims=True)
            r = jnp.where(lo, lax.rsqrt(s_lo / HALF + EPS), lax.rsqrt(s_hi / HALF + EPS))
            out_ref[:, sl] = (tj * r * g[:, sl]).astype(out_ref.dtype)

    glat = glat_ref[...]
    cq = proj("cq")
    ckv = proj("ckv")
    kpe = proj("kpe")
    kpe_sw = proj("kpe_sw")
    cqn = (cq * _rms(cq, B_Q_RANK) * glat[:, :B_Q_RANK]).astype(BF16)
    ckvn = (ckv * _rms(ckv, B_KV_RANK) * glat[:, B_Q_RANK:]).astype(BF16)
    uq = jnp.dot(cqn, wuq_ref[...], preferred_element_type=F32)
    ukv = jnp.dot(ckvn, wukv_ref[...], preferred_element_type=F32)
    qa = proj("qa")
    ka = proj("ka")
    qc = proj("qc")
    kc = proj("kc")
    vc = proj("vc")
    nt = (((1,), (1,)), ((), ()))
    va_ref[...] = lax.dot_general(wvat_ref[...], h, nt, preferred_element_type=F32).astype(BF16)
    vb_ref[...] = lax.dot_general(wvbt_ref[...], ckvn, nt, preferred_element_type=F32).astype(BF16)

    cos = cos_ref[...]
    sin = sin_ref[...]
    hw = B_HEADS * LANES
    for j in range(B_HEADS):
        sl = slice(j * LANES, (j + 1) * LANES)
        raw = uq[:, sl]
        raw_sw = uq[:, hw + j * LANES: hw + (j + 1) * LANES]
        sq = raw * raw
        s_n = jnp.sum(jnp.where(lo, sq, 0.0), axis=-1, keepdims=True)
        s_r = jnp.sum(jnp.where(lo, 0.0, sq), axis=-1, keepdims=True)
        r = jnp.where(lo, lax.rsqrt(s_n / B_NOPE + EPS), lax.rsqrt(s_r / B_ROPE + EPS))
        out = r * (raw * gqb_ref[0:1, sl] * cos + raw_sw * gqb_ref[1:2, sl] * sin)
        qb_ref[:, sl] = out.astype(BF16)

    kpe_out = _rms(kpe, B_ROPE) * (kpe * gkb_ref[1:2, :] * cos + kpe_sw * gkb_ref[2:3, :] * sin)
    for j in range(B_HEADS):
        sl = slice(j * LANES, (j + 1) * LANES)
        raw = ukv[:, sl]
        kb_ref[:, sl] = (raw * _rms(raw, B_NOPE) * gkb_ref[0:1, :] + kpe_out).astype(BF16)

    norm_halves(qa, ga_ref[0:1, :], qa_ref)
    norm_halves(ka, ga_ref[1:2, :], ka_ref)
    gc = gc_ref[...]
    norm_halves(qc, gc[:, :C_HEADS * C_HD], qc_ref)
    norm_halves(kc, gc[:, C_HEADS * C_HD:], kc_ref)
    vc_ref[...] = vc.astype(BF16)


def _in_proj(x2, cos, sin, gmix, ga, glat, gqb, gkb, gc, w1, wqc, ws, wuq, wukv, wvat, wvbt, batch, seq):
    n = x2.shape[0]
    tm = TM_IN
    nps = seq // tm
    row = lambda w: pl.BlockSpec((tm, w), lambda i: (i, 0))
    outs = (("qa", 512), ("ka", 512), ("va", None), ("qb", 1024), ("kb", 1024), ("vb", None),
            ("qc", 512), ("kc", 256), ("vc", 256))
    vt_spec = pl.BlockSpec((None, 512, tm), lambda i: (i // nps, 0, i % nps))
    vt_shape = jax.ShapeDtypeStruct((batch, 512, seq), BF16)
    return pl.pallas_call(
        _in_kernel,
        grid=(n // tm,),
        in_specs=[row(D_MODEL), row(LANES), row(LANES),
                  _const_spec(gmix.shape), _const_spec(ga.shape), _const_spec(glat.shape),
                  _const_spec(gqb.shape), _const_spec(gkb.shape), _const_spec(gc.shape),
                  _const_spec(w1.shape), _const_spec(wqc.shape), _const_spec(ws.shape),
                  _const_spec(wuq.shape), _const_spec(wukv.shape),
                  _const_spec(wvat.shape), _const_spec(wvbt.shape)],
        out_specs=[vt_spec if w is None else row(w) for _, w in outs],
        out_shape=[vt_shape if w is None else jax.ShapeDtypeStruct((n, w), BF16) for _, w in outs],
        compiler_params=_params(("parallel",)),
        name="in_proj",
    )(x2, cos, sin, gmix, ga, glat, gqb, gkb, gc, w1, wqc, ws, wuq, wukv, wvat, wvbt)


def _attn_items(nq):
    items = [(qi, ki, KIND_DIAG if ki == qi else KIND_PREV if ki == qi - 1 else KIND_FAR)
             for qi in range(nq) for ki in range(qi + 1)]
    items += [items[-1]] * 2
    return np.asarray(items, np.int32).T


def _attn_kernel(*refs, mode, lambda_init, n_items, nq):
    if mode == "diff":
        (tab_ref, q_ref, k_ref, vt_ref, bias_ref, lam_ref, subg_ref, o_ref,
         s_scr, p_scr, acc_scr, m_scr, mt_scr, al_scr) = refs
    else:
        tab_ref, q_ref, k_ref, vt_ref, bias_ref, o_ref, s_scr, p_scr, acc_scr, m_scr, mt_scr, al_scr = refs
    t = T_ATT
    lo = _lane_lo()
    m_scr[...] = jnp.full(m_scr.shape, NEG, F32)
    acc_scr[...] = jnp.zeros_like(acc_scr)
    if mode == "diff":
        lv = lam_ref[...]
        lam = (jnp.exp(jnp.sum(lv[0:1] * lv[1:2], axis=-1, keepdims=True))
               - jnp.exp(jnp.sum(lv[2:3] * lv[3:4], axis=-1, keepdims=True)) + lambda_init)
        out_gain = subg_ref[...] * (1.0 - lambda_init)

    def rows(idx):
        return pl.ds(pl.multiple_of(idx * t, t), t)

    def scores(n, slot):
        q = q_ref[rows(tab_ref[0, n]), :]
        k = k_ref[rows(tab_ref[1, n]), :]
        kind = tab_ref[2, n]
        for j in range(2):
            if mode == "diff":
                zero = jnp.zeros_like(q)
                qj = jnp.where(lo, q, zero) if j == 0 else jnp.where(lo, zero, q)
                kj = k
            else:
                qj = q[:, j * LANES:(j + 1) * LANES]
                kj = k[:, j * LANES:(j + 1) * LANES]
            s = lax.dot_general(kj, qj, (((1,), (1,)), ((), ())), preferred_element_type=F32)
            s = s + bias_ref[kind]
            s_scr[2 * slot + j] = s
            mt_scr[2 * slot + j] = jnp.max(s, axis=0, keepdims=True)

    def softmax(n, slot, pslot):
        restart = jnp.where(tab_ref[1, n] == 0, NEG, 0.0)
        for j in range(2):
            c = 2 * slot + j
            pc = 2 * pslot + j
            m_old = m_scr[j] + restart
            m_new = jnp.maximum(m_old, mt_scr[c])
            al_scr[pc] = jnp.exp2(m_old - m_new)
            m_scr[j] = m_new
            for r0 in range(0, t, ROW_CHUNK):
                d = s_scr[c, r0:r0 + ROW_CHUNK, :] - m_new
                p_scr[pc, r0:r0 + ROW_CHUNK, :] = jnp.exp2(d.astype(BF16))

    def values(n, pslot):
        qi = tab_ref[0, n]
        vt = jnp.concatenate([vt_ref[:, rows(tab_ref[1, n])], jnp.ones((DEN_ROWS, t), BF16)], axis=0)
        for j in range(2):
            pc = 2 * pslot + j
            acc_scr[qi, j] = al_scr[pc] * acc_scr[qi, j] + jnp.dot(vt, p_scr[pc], preferred_element_type=F32)

    def finalize(qi):
        outs = [acc_scr[qi, j, :LANES, :] / acc_scr[qi, j, LANES:LANES + 1, :] for j in range(2)]
        if mode == "diff":
            o = outs[0] - lam * outs[1]
            o = o * lax.rsqrt(jnp.sum(o * o, axis=0, keepdims=True) / A_VD + EPS)
            o = o.T * out_gain
        else:
            upper = lax.broadcasted_iota(jnp.int32, (LANES, 1), 0) < HALF
            o = jnp.where(upper, outs[0], outs[1]).T
        o_ref[qi * t:(qi + 1) * t, :] = o.astype(o_ref.dtype)

    scores(0, 0)
    scores(1, 1)

    def body(i, carry):
        for u in range(ITEMS_PER_TRIP):
            n = ITEMS_PER_TRIP * i + u
            s = u % S_SLOTS
            scores(n + 2, (s + 2) % S_SLOTS)
            softmax(n, s, s % P_SLOTS)
            if u > 0:
                values(n - 1, (s + 1) % P_SLOTS)
        values(ITEMS_PER_TRIP * i + ITEMS_PER_TRIP - 1, (ITEMS_PER_TRIP - 1) % P_SLOTS)
        return carry

    lax.fori_loop(0, n_items // ITEMS_PER_TRIP, body, 0)
    for qi in range(nq):
        finalize(qi)


def _attn(q, k, vt, bias, batch, seq, mode, extra=(), lambda_init=0.0):
    n = q.shape[0]
    t = T_ATT
    nq = seq // t
    tab = _attn_items(nq)
    n_items = tab.shape[1] - 2
    assert n_items % ITEMS_PER_TRIP == 0 and ITEMS_PER_TRIP % S_SLOTS == 0 and S_SLOTS % P_SLOTS == 0
    qw = LANES if mode == "diff" else 2 * LANES
    groups = q.shape[1] // qw
    per_head = bias.shape[0] > 1
    in_specs = [pl.BlockSpec(memory_space=pltpu.SMEM),
                pl.BlockSpec((seq, qw), lambda b, g: (b, g)),
                pl.BlockSpec((seq, qw), lambda b, g: (b, g)),
                pl.BlockSpec((None, LANES, seq), lambda b, g: (b, g, 0)),
                pl.BlockSpec((None, 3, t, t), lambda b, g: (g if per_head else 0, 0, 0, 0))]
    in_specs += [pl.BlockSpec(e.shape, lambda b, g: (0, 0)) for e in extra]
    return pl.pallas_call(
        functools.partial(_attn_kernel, mode=mode, lambda_init=lambda_init, n_items=n_items, nq=nq),
        grid=(batch, groups),
        in_specs=in_specs,
        out_specs=pl.BlockSpec((seq, LANES), lambda b, g: (b, g)),
        out_shape=jax.ShapeDtypeStruct((n, groups * LANES), BF16),
        scratch_shapes=[pltpu.VMEM((2 * S_SLOTS, t, t), F32), pltpu.VMEM((2 * P_SLOTS, t, t), BF16),
                        pltpu.VMEM((nq, 2, LANES + DEN_ROWS, t), F32), pltpu.VMEM((2, 1, t), F32),
                        pltpu.VMEM((2 * S_SLOTS, 1, t), F32), pltpu.VMEM((2 * P_SLOTS, 1, t), F32)],
        compiler_params=_params(("parallel", "parallel")),
        name="attn_" + mode,
    )(jnp.asarray(tab), q, k, vt, bias, *extra)


def _swa_kernel(sink_ref, q_ref, kp_ref, kc_ref, vp_ref, vc_ref, bias_ref, o_ref):
    nb = pl.program_id(1)
    lo = _lane_lo()
    col = lax.broadcasted_iota(jnp.int32, (WINDOW, 2 * WINDOW), 1)
    keep = (col >= WINDOW) | (nb > 0)
    grp = C_HEADS // C_KV_HEADS
    scores = []
    for head in range(C_HEADS):
        hp, j = divmod(head, 2)
        ksl = slice((head // grp) * LANES, (head // grp + 1) * LANES)
        q = q_ref[:, hp * LANES:(hp + 1) * LANES]
        zero = jnp.zeros_like(q)
        k = jnp.concatenate([kp_ref[:, ksl], kc_ref[:, ksl]], axis=0)
        qj = jnp.where(lo, q, zero) if j == 0 else jnp.where(lo, zero, q)
        scores.append(lax.dot_general(qj, k, (((1,), (1,)), ((), ())), preferred_element_type=F32))
    probs = []
    for head in range(C_HEADS):
        s = jnp.where(keep, scores[head] + bias_ref[head], NEG)
        sink = sink_ref[head]
        m = jnp.maximum(jnp.max(s, axis=-1, keepdims=True), sink)
        p = jnp.exp(s - m)
        den = jnp.sum(p, axis=-1, keepdims=True) + jnp.exp(sink - m)
        probs.append((p.astype(BF16), den))
    for hp in range(C_HEADS // 2):
        ksl = slice(((2 * hp) // grp) * LANES, ((2 * hp) // grp + 1) * LANES)
        v = jnp.concatenate([vp_ref[:, ksl], vc_ref[:, ksl]], axis=0)
        outs = [jnp.dot(p, v, preferred_element_type=F32) / den for p, den in probs[2 * hp:2 * hp + 2]]
        o_ref[:, hp * LANES:(hp + 1) * LANES] = jnp.where(lo, outs[0], outs[1]).astype(o_ref.dtype)


def _swa(sinks, q, k, v, bias, batch, seq):
    n = q.shape[0]
    nb = seq // WINDOW
    cur = lambda b, i: (b * nb + i, 0)
    prev = lambda b, i: (b * nb + jnp.maximum(i - 1, 0), 0)
    kw = k.shape[1]
    return pl.pallas_call(
        _swa_kernel,
        grid=(batch, nb),
        in_specs=[pl.BlockSpec(memory_space=pltpu.SMEM),
                  pl.BlockSpec((WINDOW, q.shape[1]), cur),
                  pl.BlockSpec((WINDOW, kw), prev), pl.BlockSpec((WINDOW, kw), cur),
                  pl.BlockSpec((WINDOW, kw), prev), pl.BlockSpec((WINDOW, kw), cur),
                  pl.BlockSpec(bias.shape, lambda b, i: (0, 0, 0))],
        out_specs=pl.BlockSpec((WINDOW, q.shape[1]), cur),
        out_shape=jax.ShapeDtypeStruct((n, q.shape[1]), BF16),
        compiler_params=_params(("parallel", "arbitrary")),
        name="swa",
    )(sinks, q, k, k, v, v, bias)


def _merge_kernel(x_ref, ya_ref, yb_ref, yc_ref, gmix_ref, wg_ref, pa_ref, pb_ref, pc_ref, wo_ref, o_ref):
    x = x_ref[...]
    h = (x * _rms(x, D_MODEL) * gmix_ref[...]).astype(BF16)
    merged = None
    for j, (y_ref, p_ref) in enumerate(((ya_ref, pa_ref), (yb_ref, pb_ref), (yc_ref, pc_ref))):
        gate = jnp.dot(h, wg_ref[:, j * D_MODEL:(j + 1) * D_MODEL], preferred_element_type=F32)
        term = jax.nn.sigmoid(gate) * jnp.dot(y_ref[...], p_ref[...], preferred_element_type=F32)
        merged = term if merged is None else merged + term
    o_ref[...] = x + jnp.dot(merged.astype(BF16), wo_ref[...], preferred_element_type=F32)


def _merge(x2, ya, yb, yc, gmix, wg, pa, pb, pc, wo):
    n = x2.shape[0]
    tm = TM_MERGE
    row = lambda w: pl.BlockSpec((tm, w), lambda i: (i, 0))
    return pl.pallas_call(
        _merge_kernel,
        grid=(n // tm,),
        in_specs=[row(D_MODEL), row(ya.shape[1]), row(yb.shape[1]), row(yc.shape[1]),
                  _const_spec(gmix.shape), _const_spec(wg.shape), _const_spec(pa.shape),
                  _const_spec(pb.shape), _const_spec(pc.shape), _const_spec(wo.shape)],
        out_specs=row(D_MODEL),
        out_shape=jax.ShapeDtypeStruct((n, D_MODEL), F32),
        compiler_params=_params(("parallel",)),
        name="merge",
    )(x2, ya, yb, yc, gmix, wg, pa, pb, pc, wo)


def _ffn_kernel(x_ref, g_ref, wup_ref, cw_ref, cb_ref, wdn_ref, o_ref, ubuf, act, carry, *, tiles_per_seq):
    tm = x_ref.shape[0]
    fc = FF_CHUNK

    @pl.when(pl.program_id(0) % tiles_per_seq == 0)
    def _():
        carry[...] = jnp.zeros_like(carry)

    x = x_ref[...]
    h = (x * _rms(x, D_MODEL) * g_ref[...]).astype(BF16)

    def cols(ref, rows, j):
        return jnp.concatenate([ref[rows, j * fc:(j + 1) * fc], ref[rows, D_FF + j * fc:D_FF + (j + 1) * fc]],
                               axis=1)

    def up(j):
        for half, base in enumerate((j * fc, D_FF + j * fc)):
            ubuf[j, HALO:HALO + tm, half * fc:(half + 1) * fc] = jnp.dot(
                h, wup_ref[:, base:base + fc], preferred_element_type=F32)

    def conv_act(j):
        ubuf[j, 0:HALO, :] = carry[j]
        carry[j] = ubuf[j, tm:tm + HALO, :]
        y = cols(cb_ref, slice(0, 1), j)
        for tap in range(CONV_W):
            shift = CONV_W - 1 - tap
            y = y + cols(cw_ref, slice(tap, tap + 1), j) * ubuf[j, HALO - shift:HALO - shift + tm, :]
        gate = y[:, :fc]
        act[:, j * fc:(j + 1) * fc] = (gate * jax.nn.sigmoid(gate) * y[:, fc:]).astype(BF16)

    up(0)
    for j in range(N_FF_CHUNKS):
        if j + 1 < N_FF_CHUNKS:
            up(j + 1)
        conv_act(j)
    o_ref[...] = x + jnp.dot(act[...], wdn_ref[...], preferred_element_type=F32)


def _ffn(x2, g, wup, cw, cb, wdn, seq):
    n = x2.shape[0]
    tm = TM_FFN
    row = pl.BlockSpec((tm, D_MODEL), lambda i: (i, 0))
    return pl.pallas_call(
        functools.partial(_ffn_kernel, tiles_per_seq=seq // tm),
        grid=(n // tm,),
        in_specs=[row, _const_spec(g.shape), _const_spec(wup.shape), _const_spec(cw.shape),
                  _const_spec(cb.shape), _const_spec(wdn.shape)],
        out_specs=row,
        out_shape=jax.ShapeDtypeStruct((n, D_MODEL), F32),
        scratch_shapes=[pltpu.VMEM((N_FF_CHUNKS, HALO + tm, 2 * FF_CHUNK), F32),
                        pltpu.VMEM((tm, D_FF), BF16),
                        pltpu.VMEM((N_FF_CHUNKS, HALO, 2 * FF_CHUNK), F32)],
        compiler_params=_params(("arbitrary",)),
        name="ffn",
    )(x2, g, wup, cw, cb, wdn)


def _layer_params(l, w_in, a_q_g, a_k_g, b_q_a_g, b_kv_a_g, b_w_uq, b_w_ukv, b_qn_g, b_qr_g, b_kn_g,
                  b_kr_g, c_q_g, c_k_g, w_up, conv_w, conv_b):
    w = w_in[l]
    o_kpe = 3 * 512 + B_Q_RANK + B_KV_RANK
    o_qc = o_kpe + B_ROPE
    o_kc = o_qc + C_HEADS * C_HD
    o_vc = o_kc + C_KV_HEADS * C_HD
    o_g = o_vc + C_KV_HEADS * C_HD
    w1 = w[:, :o_kpe].astype(BF16)
    wqc = w[:, o_qc:o_kc].astype(BF16)
    wg = w[:, o_g:].astype(BF16)
    kpe_w = w[:, o_kpe:o_qc].astype(BF16)
    kc_w = w[:, o_kc:o_vc].astype(BF16)
    vc_w = w[:, o_vc:o_g].astype(BF16)
    z = lambda c: jnp.zeros((w.shape[0], c), BF16)
    r = B_ROPE // 2
    pad = LANES - B_NOPE - B_ROPE
    dup = lambda t: jnp.concatenate([t[:, :C_HD], t[:, :C_HD], t[:, C_HD:], t[:, C_HD:]], axis=1)
    ws = jnp.concatenate([
        z(B_NOPE), kpe_w, z(pad),
        z(B_NOPE), kpe_w[:, r:], kpe_w[:, :r], z(pad),
        dup(kc_w), dup(vc_w)], axis=1)

    uq = b_w_uq[l].reshape(B_Q_RANK, B_HEADS, B_NOPE + B_ROPE)
    nope, pe = uq[..., :B_NOPE], uq[..., B_NOPE:]
    zq = lambda c: jnp.zeros((B_Q_RANK, B_HEADS, c), uq.dtype)
    wuq = jnp.concatenate([
        jnp.concatenate([nope, pe, zq(pad)], axis=-1).reshape(B_Q_RANK, B_HEADS * LANES),
        jnp.concatenate([zq(B_NOPE), pe[..., r:], pe[..., :r], zq(pad)], axis=-1).reshape(B_Q_RANK, B_HEADS * LANES),
    ], axis=1).astype(BF16)

    ukv = b_w_ukv[l].reshape(B_KV_RANK, B_HEADS, B_NOPE + B_VD)
    zk = jnp.zeros((B_KV_RANK, B_HEADS, LANES - B_NOPE), ukv.dtype)
    wukv = jnp.concatenate([ukv[..., :B_NOPE], zk], axis=-1).reshape(B_KV_RANK, B_HEADS * LANES).astype(BF16)
    wvbt = ukv[..., B_NOPE:].reshape(B_KV_RANK, B_HEADS * B_VD).T.astype(BF16)
    wvat = w[:, 2 * 512:3 * 512].T.astype(BF16)

    scale_a = A_HD ** -0.5 * LOG2E
    scale_b = (B_NOPE + B_ROPE) ** -0.5 * LOG2E
    scale_c = C_HD ** -0.5
    ga = jnp.stack([jnp.tile(a_q_g[l], 2 * A_HEADS) * scale_a, jnp.tile(a_k_g[l], 2 * A_HEADS)])
    glat = jnp.concatenate([b_q_a_g[l], b_kv_a_g[l]])[None, :]
    zl = lambda c: jnp.zeros((c,), F32)
    qr, kr = b_qr_g[l], b_kr_g[l]
    gqb = jnp.stack([
        jnp.tile(jnp.concatenate([b_qn_g[l], qr, zl(pad)]), B_HEADS),
        jnp.tile(jnp.concatenate([zl(B_NOPE), qr[r:], qr[:r], zl(pad)]), B_HEADS)]) * scale_b
    gkb = jnp.stack([
        jnp.concatenate([b_kn_g[l], zl(LANES - B_NOPE)]),
        jnp.concatenate([zl(B_NOPE), kr, zl(pad)]),
        jnp.concatenate([zl(B_NOPE), kr[r:], kr[:r], zl(pad)])])
    gc = jnp.concatenate([jnp.tile(c_q_g[l], C_HEADS) * scale_c, jnp.tile(c_k_g[l], 2 * C_KV_HEADS)])[None, :]

    return dict(w1=w1, wqc=wqc, ws=ws, wg=wg, wuq=wuq, wukv=wukv, wvat=wvat, wvbt=wvbt, ga=ga, glat=glat,
                gqb=gqb, gkb=gkb, gc=gc, wup=_to_bf16(w_up, l, 256), cw=conv_w[l, :, 0, :], cb=conv_b[l][None, :])


def kernel(x, positions, rel_bias_table, ln_mix_g, w_in, a_q_g, a_k_g, a_lam_q1, a_lam_k1, a_lam_q2, a_lam_k2, a_subln_g, b_q_a_g, b_kv_a_g, b_w_uq, b_w_ukv, b_qn_g, b_qr_g, b_kn_g, b_kr_g, c_q_g, c_k_g, c_sinks, p_a, p_b, p_c, w_o, ln_ffn_g, w_up, conv_w, conv_b, w_down):
    batch, seq, d = x.shape
    n = batch * seq
    assert d == D_MODEL and seq % T_ATT == 0 and n % TM_IN == 0 and seq % TM_FFN == 0
    x2 = x.reshape(n, d)
    cos, sin = _rope_tables(positions.reshape(n, 1))
    bias_a, mask_b, bias_c = _bias_tiles(rel_bias_table)
    for l in range(DEPTH):
        lambda_init = 0.8 - 0.6 * math.exp(-0.3 * l)
        p = _layer_params(l, w_in, a_q_g, a_k_g, b_q_a_g, b_kv_a_g, b_w_uq, b_w_ukv, b_qn_g, b_qr_g,
                          b_kn_g, b_kr_g, c_q_g, c_k_g, w_up, conv_w, conv_b)
        gmix = ln_mix_g[l][None, :]
        qa, ka, va, qb, kb, vb, qc, kc, vc = _in_proj(
            x2, cos, sin, gmix, p["ga"], p["glat"], p["gqb"], p["gkb"], p["gc"],
            p["w1"], p["wqc"], p["ws"], p["wuq"], p["wukv"], p["wvat"], p["wvbt"], batch, seq)
        lam = jnp.stack([a_lam_q1[l], a_lam_k1[l], a_lam_q2[l], a_lam_k2[l]])
        ya = _attn(qa, ka, va, bias_a, batch, seq, "diff", extra=(lam, a_subln_g[l][None, :]),
                   lambda_init=lambda_init)
        yb = _attn(qb, kb, vb, mask_b, batch, seq, "mla")
        yc = _swa(c_sinks[l], qc, kc, vc, bias_c, batch, seq)
        x2 = _merge(x2, ya, yb, yc, gmix, p["wg"], _to_bf16(p_a, l, 256), _to_bf16(p_b, l, 256),
                    _to_bf16(p_c, l, 256), _to_bf16(w_o, l, 256))
        x2 = _ffn(x2, ln_ffn_g[l][None, :], p["wup"], p["cw"], p["cb"], _to_bf16(w_down, l, 704), seq)
    return x2.reshape(batch, seq, d)
```

```python
import functools
import math

import jax
import jax.numpy as jnp
import numpy as np
from jax import lax
from jax.experimental import pallas as pl
from jax.experimental.pallas import tpu as pltpu

F32 = jnp.float32
BF16 = jnp.bfloat16

D_MODEL = 1024
DEPTH = 2
EPS = 1e-6
A_HEADS = 4
A_HD = 64
A_VD = 2 * A_HD
B_HEADS = 8
B_Q_RANK = 256
B_KV_RANK = 128
B_NOPE = 64
B_ROPE = 32
B_VD = 64
ROPE_THETA = 10000.0
C_HEADS = 8
C_KV_HEADS = 2
C_HD = 64
WINDOW = 128
N_BUCKETS = 32
MAX_DIST = 128
D_FF = 2816
CONV_W = 3

LANES = 128
HALF = LANES // 2
NEG = -1e30
LOG2E = math.log2(math.e)
ROW_CHUNK = 32
DEN_ROWS = 16
KIND_FAR, KIND_PREV, KIND_DIAG = 0, 1, 2
S_SLOTS = 4
P_SLOTS = 2
ITEMS_PER_TRIP = 12
VMEM_LIMIT = 56 * 1024 * 1024

T_ATT = 512
TM_IN = 512
TM_MERGE = 512
TM_FFN = 512
FF_CHUNK = 256
N_FF_CHUNKS = D_FF // FF_CHUNK
HALO = 8
SWA_WINDOWS = 4

_SEG = {"qa": (0, 0, 512), "ka": (0, 512, 1024), "cq": (0, 1536, 1792),
        "ckv": (0, 1792, 1920), "qc": (1, 0, 512),
        "kpe": (2, 0, 128), "kpe_sw": (2, 128, 256), "kc": (2, 256, 512), "vc": (2, 512, 768)}


def _params(sem, vmem=VMEM_LIMIT):
    return pltpu.CompilerParams(dimension_semantics=sem, vmem_limit_bytes=vmem)


def _const_spec(shape):
    nd = len(shape)
    return pl.BlockSpec(shape, lambda *_: (0,) * nd, pipeline_mode=pl.Buffered(1))


def _lane_lo():
    return lax.broadcasted_iota(jnp.int32, (1, LANES), 1) < HALF


def _cast_kernel(w_ref, o_ref):
    o_ref[...] = w_ref[...].astype(o_ref.dtype)


def _to_bf16(w, layer, row_block):
    _, rows, cols = w.shape
    assert rows % row_block == 0
    return pl.pallas_call(
        _cast_kernel,
        grid=(rows // row_block,),
        in_specs=[pl.BlockSpec((None, row_block, cols), lambda i: (layer, i, 0))],
        out_specs=pl.BlockSpec((row_block, cols), lambda i: (i, 0)),
        out_shape=jax.ShapeDtypeStruct((rows, cols), BF16),
        compiler_params=_params(("parallel",)),
        name="to_bf16",
    )(w)


def _rope_kernel(pos_ref, inv_ref, sign_ref, cos_ref, sin_ref):
    ang = pos_ref[...].astype(F32) * inv_ref[...]
    cos_ref[...] = jnp.cos(ang)
    sin_ref[...] = jnp.sin(ang) * sign_ref[...]


def _rope_tables(pos_col):
    n = pos_col.shape[0]
    inv = 1.0 / (ROPE_THETA ** (jnp.arange(0, B_ROPE, 2, dtype=F32) / B_ROPE))
    z = jnp.zeros((B_NOPE,), F32)
    zp = jnp.zeros((LANES - B_NOPE - B_ROPE,), F32)
    inv_pat = jnp.concatenate([z, inv, inv, zp])[None, :]
    ones = jnp.ones((B_ROPE // 2,), F32)
    sign_pat = jnp.concatenate([z, -ones, ones, zp])[None, :]
    tm = 2048
    return pl.pallas_call(
        _rope_kernel,
        grid=(n // tm,),
        in_specs=[pl.BlockSpec((tm, 1), lambda i: (i, 0)),
                  pl.BlockSpec((1, LANES), lambda i: (0, 0)),
                  pl.BlockSpec((1, LANES), lambda i: (0, 0))],
        out_specs=[pl.BlockSpec((tm, LANES), lambda i: (i, 0))] * 2,
        out_shape=[jax.ShapeDtypeStruct((n, LANES), F32)] * 2,
        compiler_params=_params(("parallel",)),
        name="rope_tables",
    )(pos_col, inv_pat, sign_pat)


def _bucket(rel):
    n = jnp.maximum(rel, 0)
    max_exact = N_BUCKETS // 2
    nf = jnp.maximum(n, 1).astype(F32)
    large = max_exact + (jnp.log(nf / max_exact) / math.log(MAX_DIST / max_exact)
                         * (N_BUCKETS - max_exact)).astype(jnp.int32)
    large = jnp.minimum(large, N_BUCKETS - 1)
    return jnp.where(n < max_exact, n, large)


def _lookup(tab_ref, bucket, col):
    out = jnp.zeros(bucket.shape, F32)
    for k in range(N_BUCKETS):
        out = jnp.where(bucket == k, tab_ref[k, col], out)
    return out


def _bias_a_kernel(tab_ref, out_ref):
    h = pl.program_id(0)
    t = out_ref.shape[-1]
    key = lax.broadcasted_iota(jnp.int32, (t, t), 0)
    qry = lax.broadcasted_iota(jnp.int32, (t, t), 1)
    far = tab_ref[N_BUCKETS - 1, h]
    rel = qry - key
    out_ref[0, KIND_FAR] = jnp.zeros((t, t), F32)
    out_ref[0, KIND_PREV] = (_lookup(tab_ref, _bucket(rel + t), h) - far) * LOG2E
    out_ref[0, KIND_DIAG] = jnp.where(rel >= 0, (_lookup(tab_ref, _bucket(rel), h) - far) * LOG2E, NEG)


def _mask_kernel(out_ref):
    t = out_ref.shape[-1]
    key = lax.broadcasted_iota(jnp.int32, (t, t), 0)
    qry = lax.broadcasted_iota(jnp.int32, (t, t), 1)
    out_ref[0, KIND_FAR] = jnp.zeros((t, t), F32)
    out_ref[0, KIND_PREV] = jnp.zeros((t, t), F32)
    out_ref[0, KIND_DIAG] = jnp.where(key <= qry, 0.0, NEG)


def _bias_c_kernel(tab_ref, out_ref):
    h = pl.program_id(0)
    row = lax.broadcasted_iota(jnp.int32, (WINDOW, 2 * WINDOW), 0)
    col = lax.broadcasted_iota(jnp.int32, (WINDOW, 2 * WINDOW), 1)
    rel = row + WINDOW - col
    valid = (rel >= 0) & (rel < WINDOW)
    out_ref[0] = jnp.where(valid, _lookup(tab_ref, _bucket(rel), h + A_HEADS), NEG)


def _bias_tiles(table):
    smem = pl.BlockSpec(memory_space=pltpu.SMEM)
    bias_a = pl.pallas_call(
        _bias_a_kernel,
        grid=(A_HEADS,),
        in_specs=[smem],
        out_specs=pl.BlockSpec((1, 3, T_ATT, T_ATT), lambda h: (h, 0, 0, 0)),
        out_shape=jax.ShapeDtypeStruct((A_HEADS, 3, T_ATT, T_ATT), F32),
        compiler_params=_params(("parallel",)),
        name="bias_a",
    )(table)
    mask_b = pl.pallas_call(
        _mask_kernel,
        out_shape=jax.ShapeDtypeStruct((1, 3, T_ATT, T_ATT), F32),
        compiler_params=pltpu.CompilerParams(vmem_limit_bytes=VMEM_LIMIT),
        name="mask_b",
    )()
    bias_c = pl.pallas_call(
        _bias_c_kernel,
        grid=(C_HEADS,),
        in_specs=[smem],
        out_specs=pl.BlockSpec((1, WINDOW, 2 * WINDOW), lambda h: (h, 0, 0)),
        out_shape=jax.ShapeDtypeStruct((C_HEADS, WINDOW, 2 * WINDOW), F32),
        compiler_params=_params(("parallel",)),
        name="bias_c",
    )(table)
    return bias_a, mask_b, bias_c


def _rms(t, width):
    return lax.rsqrt(jnp.sum(t * t, axis=-1, keepdims=True) / width + EPS)


def _in_kernel(x_ref, cos_ref, sin_ref, gmix_ref, ga_ref, glat_ref, gqb_ref, gkb_ref, gc_ref,
               w1_ref, wqc_ref, ws_ref, wuq_ref, wukv_ref, wvat_ref, wvbt_ref,
               qa_ref, ka_ref, va_ref, qb_ref, kb_ref, vb_ref, qc_ref, kc_ref, vc_ref):
    x = x_ref[...]
    h = (x * _rms(x, D_MODEL) * gmix_ref[...]).astype(BF16)
    lo = _lane_lo()
    w_refs = (w1_ref, wqc_ref, ws_ref)

    def proj(name):
        which, a, b = _SEG[name]
        return jnp.dot(h, w_refs[which][:, a:b], preferred_element_type=F32)

    def norm_halves(t, g, out_ref):
        for j in range(t.shape[1] // LANES):
            sl = slice(j * LANES, (j + 1) * LANES)
            tj = t[:, sl]
            sq = tj * tj
            s_lo = jnp.sum(jnp.where(lo, sq, 0.0), axis=-1, keepdims=True)
            s_hi = jnp.sum(jnp.where(lo, 0.0, sq), axis=-1, keepdims=True)
            r = jnp.where(lo, lax.rsqrt(s_lo / HALF + EPS), lax.rsqrt(s_hi / HALF + EPS))
            out_ref[:, sl] = (tj * r * g[:, sl]).astype(out_ref.dtype)

    glat = glat_ref[...]
    cq = proj("cq")
    ckv = proj("ckv")
    kpe = proj("kpe")
    kpe_sw = proj("kpe_sw")
    cqn = (cq * _rms(cq, B_Q_RANK) * glat[:, :B_Q_RANK]).astype(BF16)
    ckvn = (ckv * _rms(ckv, B_KV_RANK) * glat[:, B_Q_RANK:]).astype(BF16)
    uq = jnp.dot(cqn, wuq_ref[...], preferred_element_type=F32)
    ukv = jnp.dot(ckvn, wukv_ref[...], preferred_element_type=F32)
    qa = proj("qa")
    ka = proj("ka")
    qc = proj("qc")
    kc = proj("kc")
    vc = proj("vc")
    nt = (((1,), (1,)), ((), ()))
    va_ref[...] = lax.dot_general(wvat_ref[...], h, nt, preferred_element_type=F32).astype(BF16)
    vb_ref[...] = lax.dot_general(wvbt_ref[...], ckvn, nt, preferred_element_type=F32).astype(BF16)

    cos = cos_ref[...]
    sin = sin_ref[...]
    hw = B_HEADS * LANES
    for j in range(B_HEADS):
        sl = slice(j * LANES, (j + 1) * LANES)
        raw = uq[:, sl]
        raw_sw = uq[:, hw + j * LANES: hw + (j + 1) * LANES]
        sq = raw * raw
        s_n = jnp.sum(jnp.where(lo, sq, 0.0), axis=-1, keepdims=True)
        s_r = jnp.sum(jnp.where(lo, 0.0, sq), axis=-1, keepdims=True)
        r = jnp.where(lo, lax.rsqrt(s_n / B_NOPE + EPS), lax.rsqrt(s_r / B_ROPE + EPS))
        out = r * (raw * gqb_ref[0:1, sl] * cos + raw_sw * gqb_ref[1:2, sl] * sin)
        qb_ref[:, sl] = out.astype(BF16)

    kpe_out = _rms(kpe, B_ROPE) * (kpe * gkb_ref[1:2, :] * cos + kpe_sw * gkb_ref[2:3, :] * sin)
    for j in range(B_HEADS):
        sl = slice(j * LANES, (j + 1) * LANES)
        raw = ukv[:, sl]
        kb_ref[:, sl] = (raw * _rms(raw, B_NOPE) * gkb_ref[0:1, :] + kpe_out).astype(BF16)

    norm_halves(qa, ga_ref[0:1, :], qa_ref)
    norm_halves(ka, ga_ref[1:2, :], ka_ref)
    gc = gc_ref[...]
    norm_halves(qc, gc[:, :C_HEADS * C_HD], qc_ref)
    norm_halves(kc, gc[:, C_HEADS * C_HD:], kc_ref)
    vc_ref[...] = vc.astype(BF16)


def _in_proj(x2, cos, sin, gmix, ga, glat, gqb, gkb, gc, w1, wqc, ws, wuq, wukv, wvat, wvbt, batch, seq):
    n = x2.shape[0]
    tm = TM_IN
    nps = seq // tm
    row = lambda w: pl.BlockSpec((tm, w), lambda i: (i, 0))
    outs = (("qa", 512), ("ka", 512), ("va", None), ("qb", 1024), ("kb", 1024), ("vb", None),
            ("qc", 512), ("kc", 256), ("vc", 256))
    vt_spec = pl.BlockSpec((None, 512, tm), lambda i: (i // nps, 0, i % nps))
    vt_shape = jax.ShapeDtypeStruct((batch, 512, seq), BF16)
    return pl.pallas_call(
        _in_kernel,
        grid=(n // tm,),
        in_specs=[row(D_MODEL), row(LANES), row(LANES),
                  _const_spec(gmix.shape), _const_spec(ga.shape), _const_spec(glat.shape),
                  _const_spec(gqb.shape), _const_spec(gkb.shape), _const_spec(gc.shape),
                  _const_spec(w1.shape), _const_spec(wqc.shape), _const_spec(ws.shape),
                  _const_spec(wuq.shape), _const_spec(wukv.shape),
                  _const_spec(wvat.shape), _const_spec(wvbt.shape)],
        out_specs=[vt_spec if w is None else row(w) for _, w in outs],
        out_shape=[vt_shape if w is None else jax.ShapeDtypeStruct((n, w), BF16) for _, w in outs],
        compiler_params=_params(("parallel",)),
        name="in_proj",
    )(x2, cos, sin, gmix, ga, glat, gqb, gkb, gc, w1, wqc, ws, wuq, wukv, wvat, wvbt)


def _attn_items(nq):
    items = [(qi, ki, KIND_DIAG if ki == qi else KIND_PREV if ki == qi - 1 else KIND_FAR)
             for qi in range(nq) for ki in range(qi + 1)]
    items += [items[-1]] * 2
    return np.asarray(items, np.int32).T


def _attn_kernel(*refs, mode, lambda_init, n_items, nq):
    if mode == "diff":
        (tab_ref, q_ref, k_ref, vt_ref, bias_ref, lam_ref, subg_ref, o_ref,
         s_scr, p_scr, acc_scr, m_scr, mt_scr, al_scr) = refs
    else:
        tab_ref, q_ref, k_ref, vt_ref, bias_ref, o_ref, s_scr, p_scr, acc_scr, m_scr, mt_scr, al_scr = refs
    t = T_ATT
    dv = acc_scr.shape[2] - DEN_ROWS
    lo = _lane_lo()
    m_scr[...] = jnp.full(m_scr.shape, NEG, F32)
    acc_scr[...] = jnp.zeros_like(acc_scr)
    if mode == "diff":
        lv = lam_ref[...]
        lam = (jnp.exp(jnp.sum(lv[0:1] * lv[1:2], axis=-1, keepdims=True))
               - jnp.exp(jnp.sum(lv[2:3] * lv[3:4], axis=-1, keepdims=True)) + lambda_init)
        out_gain = subg_ref[...] * (1.0 - lambda_init)

    def rows(idx):
        return pl.ds(pl.multiple_of(idx * t, t), t)

    def scores(n, slot):
        q = q_ref[rows(tab_ref[0, n]), :]
        k = k_ref[rows(tab_ref[1, n]), :]
        kind = tab_ref[2, n]
        for j in range(2):
            if mode == "diff":
                zero = jnp.zeros_like(q)
                qj = jnp.where(lo, q, zero) if j == 0 else jnp.where(lo, zero, q)
                kj = k
            else:
                qj = q[:, j * LANES:(j + 1) * LANES]
                kj = k[:, j * LANES:(j + 1) * LANES]
            s = lax.dot_general(kj, qj, (((1,), (1,)), ((), ())), preferred_element_type=F32)
            s = s + bias_ref[kind]
            s_scr[2 * slot + j] = s
            mt_scr[2 * slot + j] = jnp.max(s, axis=0, keepdims=True)

    def softmax(n, slot, pslot):
        restart = jnp.where(tab_ref[1, n] == 0, NEG, 0.0)
        for j in range(2):
            c = 2 * slot + j
            pc = 2 * pslot + j
            m_old = m_scr[j] + restart
            m_new = jnp.maximum(m_old, mt_scr[c])
            al_scr[pc] = jnp.exp2(m_old - m_new)
            m_scr[j] = m_new
            for r0 in range(0, t, ROW_CHUNK):
                d = s_scr[c, r0:r0 + ROW_CHUNK, :] - m_new
                p_scr[pc, r0:r0 + ROW_CHUNK, :] = jnp.exp2(d.astype(BF16))

    def values(n, pslot):
        qi = tab_ref[0, n]
        v = vt_ref[:, rows(tab_ref[1, n])]
        ones = jnp.ones((DEN_ROWS, t), BF16)
        for j in range(2):
            pc = 2 * pslot + j
            vt = jnp.concatenate([v if mode == "diff" else v[j * dv:(j + 1) * dv, :], ones], axis=0)
            acc_scr[qi, j] = al_scr[pc] * acc_scr[qi, j] + jnp.dot(vt, p_scr[pc], preferred_element_type=F32)

    def finalize(qi):
        outs = [acc_scr[qi, j, :dv, :] / acc_scr[qi, j, dv:dv + 1, :] for j in range(2)]
        if mode == "diff":
            o = outs[0] - lam * outs[1]
            o = o * lax.rsqrt(jnp.sum(o * o, axis=0, keepdims=True) / A_VD + EPS)
            o = o.T * out_gain
        else:
            o = jnp.concatenate(outs, axis=0).T
        o_ref[qi * t:(qi + 1) * t, :] = o.astype(o_ref.dtype)

    scores(0, 0)
    scores(1, 1)

    def body(i, carry):
        for u in range(ITEMS_PER_TRIP):
            n = ITEMS_PER_TRIP * i + u
            s = u % S_SLOTS
            scores(n + 2, (s + 2) % S_SLOTS)
            softmax(n, s, s % P_SLOTS)
            if u > 0:
                values(n - 1, (s + 1) % P_SLOTS)
        values(ITEMS_PER_TRIP * i + ITEMS_PER_TRIP - 1, (ITEMS_PER_TRIP - 1) % P_SLOTS)
        return carry

    lax.fori_loop(0, n_items // ITEMS_PER_TRIP, body, 0)
    for qi in range(nq):
        finalize(qi)


def _attn(q, k, vt, bias, batch, seq, mode, extra=(), lambda_init=0.0):
    n = q.shape[0]
    t = T_ATT
    nq = seq // t
    tab = _attn_items(nq)
    n_items = tab.shape[1] - 2
    assert n_items % ITEMS_PER_TRIP == 0 and ITEMS_PER_TRIP % S_SLOTS == 0 and S_SLOTS % P_SLOTS == 0
    qw = LANES if mode == "diff" else 2 * LANES
    groups = q.shape[1] // qw
    per_head = bias.shape[0] > 1
    in_specs = [pl.BlockSpec(memory_space=pltpu.SMEM),
                pl.BlockSpec((seq, qw), lambda b, g: (b, g)),
                pl.BlockSpec((seq, qw), lambda b, g: (b, g)),
                pl.BlockSpec((None, LANES, seq), lambda b, g: (b, g, 0)),
                pl.BlockSpec((None, 3, t, t), lambda b, g: (g if per_head else 0, 0, 0, 0))]
    in_specs += [pl.BlockSpec(e.shape, lambda b, g: (0, 0)) for e in extra]
    return pl.pallas_call(
        functools.partial(_attn_kernel, mode=mode, lambda_init=lambda_init, n_items=n_items, nq=nq),
        grid=(batch, groups),
        in_specs=in_specs,
        out_specs=pl.BlockSpec((seq, LANES), lambda b, g: (b, g)),
        out_shape=jax.ShapeDtypeStruct((n, groups * LANES), BF16),
        scratch_shapes=[pltpu.VMEM((2 * S_SLOTS, t, t), F32), pltpu.VMEM((2 * P_SLOTS, t, t), BF16),
                        pltpu.VMEM((nq, 2, (A_VD if mode == "diff" else B_VD) + DEN_ROWS, t), F32), pltpu.VMEM((2, 1, t), F32),
                        pltpu.VMEM((2 * S_SLOTS, 1, t), F32), pltpu.VMEM((2 * P_SLOTS, 1, t), F32)],
        compiler_params=_params(("parallel", "parallel")),
        name="attn_" + mode,
    )(jnp.asarray(tab), q, k, vt, bias, *extra)


def _swa_kernel(sink_ref, q_ref, kp_ref, kc_ref, vp_ref, vc_ref, bias_ref, o_ref):
    first_step = pl.program_id(1) == 0
    lo = _lane_lo()
    col = lax.broadcasted_iota(jnp.int32, (WINDOW, 2 * WINDOW), 1)
    keep = (col >= WINDOW) | jnp.logical_not(first_step)
    grp = C_HEADS // C_KV_HEADS
    k_all = jnp.concatenate([kp_ref[...], kc_ref[...]], axis=0)
    v_all = jnp.concatenate([vp_ref[...], vc_ref[...]], axis=0)
    work = [(w, head) for w in range(SWA_WINDOWS) for head in range(C_HEADS)]
    scores = []
    for w, head in work:
        hp, j = divmod(head, 2)
        ksl = slice((head // grp) * LANES, (head // grp + 1) * LANES)
        q = q_ref[w * WINDOW:(w + 1) * WINDOW, hp * LANES:(hp + 1) * LANES]
        zero = jnp.zeros_like(q)
        qj = jnp.where(lo, q, zero) if j == 0 else jnp.where(lo, zero, q)
        k = k_all[w * WINDOW:(w + 2) * WINDOW, ksl]
        scores.append(lax.dot_general(qj, k, (((1,), (1,)), ((), ())), preferred_element_type=F32))
    probs = []
    for (w, head), s in zip(work, scores):
        s = s + bias_ref[head]
        if w == 0:
            s = jnp.where(keep, s, NEG)
        sink = sink_ref[head]
        m = jnp.maximum(jnp.max(s, axis=-1, keepdims=True), sink)
        p = jnp.exp(s - m)
        den = jnp.sum(p, axis=-1, keepdims=True) + jnp.exp(sink - m)
        probs.append((p.astype(BF16), den))
    for w in range(SWA_WINDOWS):
        for hp in range(C_HEADS // 2):
            ksl = slice(((2 * hp) // grp) * LANES, ((2 * hp) // grp + 1) * LANES)
            v = v_all[w * WINDOW:(w + 2) * WINDOW, ksl]
            pair = probs[w * C_HEADS + 2 * hp:w * C_HEADS + 2 * hp + 2]
            outs = [jnp.dot(p, v, preferred_element_type=F32) / den for p, den in pair]
            o_ref[w * WINDOW:(w + 1) * WINDOW, hp * LANES:(hp + 1) * LANES] = (
                jnp.where(lo, outs[0], outs[1]).astype(o_ref.dtype))


def _swa(sinks, q, k, v, bias, batch, seq):
    n = q.shape[0]
    rows = SWA_WINDOWS * WINDOW
    steps = seq // rows
    cur = lambda b, i: (b * steps + i, 0)
    prev = lambda b, i: ((b * steps + i) * SWA_WINDOWS - jnp.minimum(i, 1), 0)
    kw = k.shape[1]
    return pl.pallas_call(
        _swa_kernel,
        grid=(batch, steps),
        in_specs=[pl.BlockSpec(memory_space=pltpu.SMEM),
                  pl.BlockSpec((rows, q.shape[1]), cur),
                  pl.BlockSpec((WINDOW, kw), prev), pl.BlockSpec((rows, kw), cur),
                  pl.BlockSpec((WINDOW, kw), prev), pl.BlockSpec((rows, kw), cur),
                  pl.BlockSpec(bias.shape, lambda b, i: (0, 0, 0))],
        out_specs=pl.BlockSpec((rows, q.shape[1]), cur),
        out_shape=jax.ShapeDtypeStruct((n, q.shape[1]), BF16),
        compiler_params=_params(("parallel", "arbitrary")),
        name="swa",
    )(sinks, q, k, k, v, v, bias)


def _merge_kernel(x_ref, ya_ref, yb_ref, yc_ref, gmix_ref, wg_ref, pa_ref, pb_ref, pc_ref, wo_ref, o_ref):
    x = x_ref[...]
    h = (x * _rms(x, D_MODEL) * gmix_ref[...]).astype(BF16)
    merged = None
    for j, (y_ref, p_ref) in enumerate(((ya_ref, pa_ref), (yb_ref, pb_ref), (yc_ref, pc_ref))):
        gate = jnp.dot(h, wg_ref[:, j * D_MODEL:(j + 1) * D_MODEL], preferred_element_type=F32)
        term = jax.nn.sigmoid(gate) * jnp.dot(y_ref[...], p_ref[...], preferred_element_type=F32)
        merged = term if merged is None else merged + term
    o_ref[...] = x + jnp.dot(merged.astype(BF16), wo_ref[...], preferred_element_type=F32)


def _merge(x2, ya, yb, yc, gmix, wg, pa, pb, pc, wo):
    n = x2.shape[0]
    tm = TM_MERGE
    row = lambda w: pl.BlockSpec((tm, w), lambda i: (i, 0))
    return pl.pallas_call(
        _merge_kernel,
        grid=(n // tm,),
        in_specs=[row(D_MODEL), row(ya.shape[1]), row(yb.shape[1]), row(yc.shape[1]),
                  _const_spec(gmix.shape), _const_spec(wg.shape), _const_spec(pa.shape),
                  _const_spec(pb.shape), _const_spec(pc.shape), _const_spec(wo.shape)],
        out_specs=row(D_MODEL),
        out_shape=jax.ShapeDtypeStruct((n, D_MODEL), F32),
        compiler_params=_params(("parallel",)),
        name="merge",
    )(x2, ya, yb, yc, gmix, wg, pa, pb, pc, wo)


def _ffn_kernel(x_ref, g_ref, wup_ref, cw_ref, cb_ref, wdn_ref, o_ref, ubuf, act, carry, *, tiles_per_seq):
    tm = x_ref.shape[0]
    fc = FF_CHUNK

    @pl.when(pl.program_id(0) % tiles_per_seq == 0)
    def _():
        carry[...] = jnp.zeros_like(carry)

    x = x_ref[...]
    h = (x * _rms(x, D_MODEL) * g_ref[...]).astype(BF16)

    def cols(ref, rows, j):
        return jnp.concatenate([ref[rows, j * fc:(j + 1) * fc], ref[rows, D_FF + j * fc:D_FF + (j + 1) * fc]],
                               axis=1)

    def up(j):
        for half, base in enumerate((j * fc, D_FF + j * fc)):
            ubuf[j, HALO:HALO + tm, half * fc:(half + 1) * fc] = jnp.dot(
                h, wup_ref[:, base:base + fc], preferred_element_type=F32)

    def conv_act(j):
        ubuf[j, 0:HALO, :] = carry[j]
        carry[j] = ubuf[j, tm:tm + HALO, :]
        y = cols(cb_ref, slice(0, 1), j)
        for tap in range(CONV_W):
            shift = CONV_W - 1 - tap
            y = y + cols(cw_ref, slice(tap, tap + 1), j) * ubuf[j, HALO - shift:HALO - shift + tm, :]
        gate = y[:, :fc]
        act[:, j * fc:(j + 1) * fc] = (gate * jax.nn.sigmoid(gate) * y[:, fc:]).astype(BF16)

    up(0)
    for j in range(N_FF_CHUNKS):
        if j + 1 < N_FF_CHUNKS:
            up(j + 1)
        conv_act(j)
    o_ref[...] = x + jnp.dot(act[...], wdn_ref[...], preferred_element_type=F32)


def _ffn(x2, g, wup, cw, cb, wdn, seq):
    n = x2.shape[0]
    tm = TM_FFN
    row = pl.BlockSpec((tm, D_MODEL), lambda i: (i, 0))
    return pl.pallas_call(
        functools.partial(_ffn_kernel, tiles_per_seq=seq // tm),
        grid=(n // tm,),
        in_specs=[row, _const_spec(g.shape), _const_spec(wup.shape), _const_spec(cw.shape),
                  _const_spec(cb.shape), _const_spec(wdn.shape)],
        out_specs=row,
        out_shape=jax.ShapeDtypeStruct((n, D_MODEL), F32),
        scratch_shapes=[pltpu.VMEM((N_FF_CHUNKS, HALO + tm, 2 * FF_CHUNK), F32),
                        pltpu.VMEM((tm, D_FF), BF16),
                        pltpu.VMEM((N_FF_CHUNKS, HALO, 2 * FF_CHUNK), F32)],
        compiler_params=_params(("arbitrary",)),
        name="ffn",
    )(x2, g, wup, cw, cb, wdn)


def _layer_params(l, w_in, a_q_g, a_k_g, b_q_a_g, b_kv_a_g, b_w_uq, b_w_ukv, b_qn_g, b_qr_g, b_kn_g,
                  b_kr_g, c_q_g, c_k_g, w_up, conv_w, conv_b):
    w = w_in[l]
    o_kpe = 3 * 512 + B_Q_RANK + B_KV_RANK
    o_qc = o_kpe + B_ROPE
    o_kc = o_qc + C_HEADS * C_HD
    o_vc = o_kc + C_KV_HEADS * C_HD
    o_g = o_vc + C_KV_HEADS * C_HD
    w1 = w[:, :o_kpe].astype(BF16)
    wqc = w[:, o_qc:o_kc].astype(BF16)
    wg = w[:, o_g:].astype(BF16)
    kpe_w = w[:, o_kpe:o_qc].astype(BF16)
    kc_w = w[:, o_kc:o_vc].astype(BF16)
    vc_w = w[:, o_vc:o_g].astype(BF16)
    z = lambda c: jnp.zeros((w.shape[0], c), BF16)
    r = B_ROPE // 2
    pad = LANES - B_NOPE - B_ROPE
    dup = lambda t: jnp.concatenate([t[:, :C_HD], t[:, :C_HD], t[:, C_HD:], t[:, C_HD:]], axis=1)
    ws = jnp.concatenate([
        z(B_NOPE), kpe_w, z(pad),
        z(B_NOPE), kpe_w[:, r:], kpe_w[:, :r], z(pad),
        dup(kc_w), dup(vc_w)], axis=1)

    uq = b_w_uq[l].reshape(B_Q_RANK, B_HEADS, B_NOPE + B_ROPE)
    nope, pe = uq[..., :B_NOPE], uq[..., B_NOPE:]
    zq = lambda c: jnp.zeros((B_Q_RANK, B_HEADS, c), uq.dtype)
    wuq = jnp.concatenate([
        jnp.concatenate([nope, pe, zq(pad)], axis=-1).reshape(B_Q_RANK, B_HEADS * LANES),
        jnp.concatenate([zq(B_NOPE), pe[..., r:], pe[..., :r], zq(pad)], axis=-1).reshape(B_Q_RANK, B_HEADS * LANES),
    ], axis=1).astype(BF16)

    ukv = b_w_ukv[l].reshape(B_KV_RANK, B_HEADS, B_NOPE + B_VD)
    zk = jnp.zeros((B_KV_RANK, B_HEADS, LANES - B_NOPE), ukv.dtype)
    wukv = jnp.concatenate([ukv[..., :B_NOPE], zk], axis=-1).reshape(B_KV_RANK, B_HEADS * LANES).astype(BF16)
    wvbt = ukv[..., B_NOPE:].reshape(B_KV_RANK, B_HEADS * B_VD).T.astype(BF16)
    wvat = w[:, 2 * 512:3 * 512].T.astype(BF16)

    scale_a = A_HD ** -0.5 * LOG2E
    scale_b = (B_NOPE + B_ROPE) ** -0.5 * LOG2E
    scale_c = C_HD ** -0.5
    ga = jnp.stack([jnp.tile(a_q_g[l], 2 * A_HEADS) * scale_a, jnp.tile(a_k_g[l], 2 * A_HEADS)])
    glat = jnp.concatenate([b_q_a_g[l], b_kv_a_g[l]])[None, :]
    zl = lambda c: jnp.zeros((c,), F32)
    qr, kr = b_qr_g[l], b_kr_g[l]
    gqb = jnp.stack([
        jnp.tile(jnp.concatenate([b_qn_g[l], qr, zl(pad)]), B_HEADS),
        jnp.tile(jnp.concatenate([zl(B_NOPE), qr[r:], qr[:r], zl(pad)]), B_HEADS)]) * scale_b
    gkb = jnp.stack([
        jnp.concatenate([b_kn_g[l], zl(LANES - B_NOPE)]),
        jnp.concatenate([zl(B_NOPE), kr, zl(pad)]),
        jnp.concatenate([zl(B_NOPE), kr[r:], kr[:r], zl(pad)])])
    gc = jnp.concatenate([jnp.tile(c_q_g[l], C_HEADS) * scale_c, jnp.tile(c_k_g[l], 2 * C_KV_HEADS)])[None, :]

    return dict(w1=w1, wqc=wqc, ws=ws, wg=wg, wuq=wuq, wukv=wukv, wvat=wvat, wvbt=wvbt, ga=ga, glat=glat,
                gqb=gqb, gkb=gkb, gc=gc, wup=_to_bf16(w_up, l, 256), cw=conv_w[l, :, 0, :], cb=conv_b[l][None, :])


def kernel(x, positions, rel_bias_table, ln_mix_g, w_in, a_q_g, a_k_g, a_lam_q1, a_lam_k1, a_lam_q2, a_lam_k2, a_subln_g, b_q_a_g, b_kv_a_g, b_w_uq, b_w_ukv, b_qn_g, b_qr_g, b_kn_g, b_kr_g, c_q_g, c_k_g, c_sinks, p_a, p_b, p_c, w_o, ln_ffn_g, w_up, conv_w, conv_b, w_down):
    batch, seq, d = x.shape
    n = batch * seq
    assert d == D_MODEL and seq % T_ATT == 0 and n % TM_IN == 0 and seq % TM_FFN == 0
    x2 = x.reshape(n, d)
    cos, sin = _rope_tables(positions.reshape(n, 1))
    bias_a, mask_b, bias_c = _bias_tiles(rel_bias_table)
    for l in range(DEPTH):
        lambda_init = 0.8 - 0.6 * math.exp(-0.3 * l)
        p = _layer_params(l, w_in, a_q_g, a_k_g, b_q_a_g, b_kv_a_g, b_w_uq, b_w_ukv, b_qn_g, b_qr_g,
                          b_kn_g, b_kr_g, c_q_g, c_k_g, w_up, conv_w, conv_b)
        gmix = ln_mix_g[l][None, :]
        qa, ka, va, qb, kb, vb, qc, kc, vc = _in_proj(
            x2, cos, sin, gmix, p["ga"], p["glat"], p["gqb"], p["gkb"], p["gc"],
            p["w1"], p["wqc"], p["ws"], p["wuq"], p["wukv"], p["wvat"], p["wvbt"], batch, seq)
        lam = jnp.stack([a_lam_q1[l], a_lam_k1[l], a_lam_q2[l], a_lam_k2[l]])
        ya = _attn(qa, ka, va, bias_a, batch, seq, "diff", extra=(lam, a_subln_g[l][None, :]),
                   lambda_init=lambda_init)
        yb = _attn(qb, kb, vb, mask_b, batch, seq, "mla")
        yc = _swa(c_sinks[l], qc, kc, vc, bias_c, batch, seq)
        x2 = _merge(x2, ya, yb, yc, gmix, p["wg"], _to_bf16(p_a, l, 256), _to_bf16(p_b, l, 256),
                    _to_bf16(p_c, l, 256), _to_bf16(w_o, l, 256))
        x2 = _ffn(x2, ln_ffn_g[l][None, :], p["wup"], p["cw"], p["cb"], _to_bf16(w_down, l, 704), seq)
    return x2.reshape(batch, seq, d)
```

```python
import functools
import math

import jax
import jax.numpy as jnp
import numpy as np
from jax import lax
from jax.experimental import pallas as pl
from jax.experimental.pallas import tpu as pltpu

F32 = jnp.float32
BF16 = jnp.bfloat16

D_MODEL = 1024
DEPTH = 2
EPS = 1e-6
A_HEADS = 4
A_HD = 64
A_VD = 2 * A_HD
B_HEADS = 8
B_Q_RANK = 256
B_KV_RANK = 128
B_NOPE = 64
B_ROPE = 32
B_VD = 64
ROPE_THETA = 10000.0
C_HEADS = 8
C_KV_HEADS = 2
C_HD = 64
WINDOW = 128
N_BUCKETS = 32
MAX_DIST = 128
D_FF = 2816
CONV_W = 3

LANES = 128
HALF = LANES // 2
NEG = -1e30
LOG2E = math.log2(math.e)
ROW_CHUNK = 32
DEN_ROWS = 16
KIND_FAR, KIND_PREV, KIND_DIAG = 0, 1, 2
S_SLOTS = 4
P_SLOTS = 2
ITEMS_PER_TRIP = 12
VMEM_LIMIT = 56 * 1024 * 1024

T_ATT = 512
TM_IN = 512
TM_MERGE = 512
TM_FFN = 512
FF_CHUNK = 256
N_FF_CHUNKS = D_FF // FF_CHUNK
HALO = 8
SWA_WINDOWS = 4

_SEG = {"qa": (0, 0, 512), "ka": (0, 512, 1024), "cq": (0, 1536, 1792),
        "ckv": (0, 1792, 1920), "qc": (1, 0, 512),
        "kpe_both": (2, 0, 256), "kc": (2, 256, 512), "vc": (2, 512, 768)}


def _params(sem, vmem=VMEM_LIMIT):
    return pltpu.CompilerParams(dimension_semantics=sem, vmem_limit_bytes=vmem)


def _const_spec(shape):
    nd = len(shape)
    return pl.BlockSpec(shape, lambda *_: (0,) * nd, pipeline_mode=pl.Buffered(1))


def _lane_lo():
    return lax.broadcasted_iota(jnp.int32, (1, LANES), 1) < HALF


def _cast_kernel(w_ref, o_ref):
    o_ref[...] = w_ref[...].astype(o_ref.dtype)


def _to_bf16(w, layer, row_block):
    _, rows, cols = w.shape
    assert rows % row_block == 0
    return pl.pallas_call(
        _cast_kernel,
        grid=(rows // row_block,),
        in_specs=[pl.BlockSpec((None, row_block, cols), lambda i: (layer, i, 0))],
        out_specs=pl.BlockSpec((row_block, cols), lambda i: (i, 0)),
        out_shape=jax.ShapeDtypeStruct((rows, cols), BF16),
        compiler_params=_params(("parallel",)),
        name="to_bf16",
    )(w)


def _rope_kernel(pos_ref, inv_ref, sign_ref, cos_ref, sin_ref):
    ang = pos_ref[...].astype(F32) * inv_ref[...]
    cos_ref[...] = jnp.cos(ang)
    sin_ref[...] = jnp.sin(ang) * sign_ref[...]


def _rope_tables(pos_col):
    n = pos_col.shape[0]
    inv = 1.0 / (ROPE_THETA ** (jnp.arange(0, B_ROPE, 2, dtype=F32) / B_ROPE))
    z = jnp.zeros((B_NOPE,), F32)
    zp = jnp.zeros((LANES - B_NOPE - B_ROPE,), F32)
    inv_pat = jnp.concatenate([z, inv, inv, zp])[None, :]
    ones = jnp.ones((B_ROPE // 2,), F32)
    sign_pat = jnp.concatenate([z, -ones, ones, zp])[None, :]
    tm = 2048
    return pl.pallas_call(
        _rope_kernel,
        grid=(n // tm,),
        in_specs=[pl.BlockSpec((tm, 1), lambda i: (i, 0)),
                  pl.BlockSpec((1, LANES), lambda i: (0, 0)),
                  pl.BlockSpec((1, LANES), lambda i: (0, 0))],
        out_specs=[pl.BlockSpec((tm, LANES), lambda i: (i, 0))] * 2,
        out_shape=[jax.ShapeDtypeStruct((n, LANES), F32)] * 2,
        compiler_params=_params(("parallel",)),
        name="rope_tables",
    )(pos_col, inv_pat, sign_pat)


def _bucket(rel):
    n = jnp.maximum(rel, 0)
    max_exact = N_BUCKETS // 2
    nf = jnp.maximum(n, 1).astype(F32)
    large = max_exact + (jnp.log(nf / max_exact) / math.log(MAX_DIST / max_exact)
                         * (N_BUCKETS - max_exact)).astype(jnp.int32)
    large = jnp.minimum(large, N_BUCKETS - 1)
    return jnp.where(n < max_exact, n, large)


def _lookup(tab_ref, bucket, col):
    out = jnp.zeros(bucket.shape, F32)
    for k in range(N_BUCKETS):
        out = jnp.where(bucket == k, tab_ref[k, col], out)
    return out


def _bias_a_kernel(tab_ref, out_ref):
    h = pl.program_id(0)
    t = out_ref.shape[-1]
    key = lax.broadcasted_iota(jnp.int32, (t, t), 0)
    qry = lax.broadcasted_iota(jnp.int32, (t, t), 1)
    far = tab_ref[N_BUCKETS - 1, h]
    rel = qry - key
    out_ref[0, KIND_FAR] = jnp.zeros((t, t), F32)
    out_ref[0, KIND_PREV] = (_lookup(tab_ref, _bucket(rel + t), h) - far) * LOG2E
    out_ref[0, KIND_DIAG] = jnp.where(rel >= 0, (_lookup(tab_ref, _bucket(rel), h) - far) * LOG2E, NEG)


def _mask_kernel(out_ref):
    t = out_ref.shape[-1]
    key = lax.broadcasted_iota(jnp.int32, (t, t), 0)
    qry = lax.broadcasted_iota(jnp.int32, (t, t), 1)
    out_ref[0, KIND_FAR] = jnp.zeros((t, t), F32)
    out_ref[0, KIND_PREV] = jnp.zeros((t, t), F32)
    out_ref[0, KIND_DIAG] = jnp.where(key <= qry, 0.0, NEG)


def _bias_c_kernel(tab_ref, out_ref):
    h = pl.program_id(0)
    row = lax.broadcasted_iota(jnp.int32, (WINDOW, 2 * WINDOW), 0)
    col = lax.broadcasted_iota(jnp.int32, (WINDOW, 2 * WINDOW), 1)
    rel = row + WINDOW - col
    valid = (rel >= 0) & (rel < WINDOW)
    out_ref[0] = jnp.where(valid, _lookup(tab_ref, _bucket(rel), h + A_HEADS), NEG)


def _bias_tiles(table):
    smem = pl.BlockSpec(memory_space=pltpu.SMEM)
    bias_a = pl.pallas_call(
        _bias_a_kernel,
        grid=(A_HEADS,),
        in_specs=[smem],
        out_specs=pl.BlockSpec((1, 3, T_ATT, T_ATT), lambda h: (h, 0, 0, 0)),
        out_shape=jax.ShapeDtypeStruct((A_HEADS, 3, T_ATT, T_ATT), F32),
        compiler_params=_params(("parallel",)),
        name="bias_a",
    )(table)
    mask_b = pl.pallas_call(
        _mask_kernel,
        out_shape=jax.ShapeDtypeStruct((1, 3, T_ATT, T_ATT), F32),
        compiler_params=pltpu.CompilerParams(vmem_limit_bytes=VMEM_LIMIT),
        name="mask_b",
    )()
    bias_c = pl.pallas_call(
        _bias_c_kernel,
        grid=(C_HEADS,),
        in_specs=[smem],
        out_specs=pl.BlockSpec((1, WINDOW, 2 * WINDOW), lambda h: (h, 0, 0)),
        out_shape=jax.ShapeDtypeStruct((C_HEADS, WINDOW, 2 * WINDOW), F32),
        compiler_params=_params(("parallel",)),
        name="bias_c",
    )(table)
    return bias_a, mask_b, bias_c


def _rms(t, width):
    return lax.rsqrt(jnp.sum(t * t, axis=-1, keepdims=True) / width + EPS)


def _in_kernel(x_ref, cos_ref, sin_ref, gmix_ref, ga_ref, glat_ref, gqb_ref, gkb_ref, gc_ref,
               w1_ref, wqc_ref, ws_ref, wuq_ref, wukv_ref, wvat_ref, wvbt_ref,
               qa_ref, ka_ref, va_ref, qb_ref, kb_ref, vb_ref, qc_ref, kc_ref, vc_ref):
    x = x_ref[...]
    h = (x * _rms(x, D_MODEL) * gmix_ref[...]).astype(BF16)
    lo = _lane_lo()
    w_refs = (w1_ref, wqc_ref, ws_ref)

    def proj(name):
        which, a, b = _SEG[name]
        return jnp.dot(h, w_refs[which][:, a:b], preferred_element_type=F32)

    def norm_halves(t, g, out_ref):
        for j in range(t.shape[1] // LANES):
            sl = slice(j * LANES, (j + 1) * LANES)
            tj = t[:, sl]
            sq = tj * tj
            s_lo = jnp.sum(jnp.where(lo, sq, 0.0), axis=-1, keepdims=True)
            s_hi = jnp.sum(jnp.where(lo, 0.0, sq), axis=-1, keepdims=True)
            r = jnp.where(lo, lax.rsqrt(s_lo / HALF + EPS), lax.rsqrt(s_hi / HALF + EPS))
            out_ref[:, sl] = (tj * r * g[:, sl]).astype(out_ref.dtype)

    glat = glat_ref[...]
    cq = proj("cq")
    ckv = proj("ckv")
    kpe_both = proj("kpe_both")
    kpe = kpe_both[:, :LANES]
    kpe_sw = kpe_both[:, LANES:]
    cqn = (cq * _rms(cq, B_Q_RANK) * glat[:, :B_Q_RANK]).astype(BF16)
    ckvn = (ckv * _rms(ckv, B_KV_RANK) * glat[:, B_Q_RANK:]).astype(BF16)
    uq = jnp.dot(cqn, wuq_ref[...], preferred_element_type=F32)
    ukv = jnp.dot(ckvn, wukv_ref[...], preferred_element_type=F32)
    qa = proj("qa")
    ka = proj("ka")
    qc = proj("qc")
    kc = proj("kc")
    vc = proj("vc")
    nt = (((1,), (1,)), ((), ()))
    va_ref[...] = lax.dot_general(wvat_ref[...], h, nt, preferred_element_type=F32).astype(BF16)
    vb_ref[...] = lax.dot_general(wvbt_ref[...], ckvn, nt, preferred_element_type=F32).astype(BF16)

    cos = cos_ref[...]
    sin = sin_ref[...]
    hw = B_HEADS * LANES
    for j in range(B_HEADS):
        sl = slice(j * LANES, (j + 1) * LANES)
        raw = uq[:, sl]
        raw_sw = uq[:, hw + j * LANES: hw + (j + 1) * LANES]
        sq = raw * raw
        s_n = jnp.sum(jnp.where(lo, sq, 0.0), axis=-1, keepdims=True)
        s_r = jnp.sum(jnp.where(lo, 0.0, sq), axis=-1, keepdims=True)
        r = jnp.where(lo, lax.rsqrt(s_n / B_NOPE + EPS), lax.rsqrt(s_r / B_ROPE + EPS))
        out = r * (raw * gqb_ref[0:1, sl] * cos + raw_sw * gqb_ref[1:2, sl] * sin)
        qb_ref[:, sl] = out.astype(BF16)

    kpe_out = _rms(kpe, B_ROPE) * (kpe * gkb_ref[1:2, :] * cos + kpe_sw * gkb_ref[2:3, :] * sin)
    for j in range(B_HEADS):
        sl = slice(j * LANES, (j + 1) * LANES)
        raw = ukv[:, sl]
        kb_ref[:, sl] = (raw * _rms(raw, B_NOPE) * gkb_ref[0:1, :] + kpe_out).astype(BF16)

    norm_halves(qa, ga_ref[0:1, :], qa_ref)
    norm_halves(ka, ga_ref[1:2, :], ka_ref)
    gc = gc_ref[...]
    norm_halves(qc, gc[:, :C_HEADS * C_HD], qc_ref)
    norm_halves(kc, gc[:, C_HEADS * C_HD:], kc_ref)
    vc_ref[...] = vc.astype(BF16)


def _in_proj(x2, cos, sin, gmix, ga, glat, gqb, gkb, gc, w1, wqc, ws, wuq, wukv, wvat, wvbt, batch, seq):
    n = x2.shape[0]
    tm = TM_IN
    nps = seq // tm
    row = lambda w: pl.BlockSpec((tm, w), lambda i: (i, 0))
    outs = (("qa", 512), ("ka", 512), ("va", None), ("qb", 1024), ("kb", 1024), ("vb", None),
            ("qc", 512), ("kc", 256), ("vc", 256))
    vt_spec = pl.BlockSpec((None, 512, tm), lambda i: (i // nps, 0, i % nps))
    vt_shape = jax.ShapeDtypeStruct((batch, 512, seq), BF16)
    return pl.pallas_call(
        _in_kernel,
        grid=(n // tm,),
        in_specs=[row(D_MODEL), row(LANES), row(LANES),
                  _const_spec(gmix.shape), _const_spec(ga.shape), _const_spec(glat.shape),
                  _const_spec(gqb.shape), _const_spec(gkb.shape), _const_spec(gc.shape),
                  _const_spec(w1.shape), _const_spec(wqc.shape), _const_spec(ws.shape),
                  _const_spec(wuq.shape), _const_spec(wukv.shape),
                  _const_spec(wvat.shape), _const_spec(wvbt.shape)],
        out_specs=[vt_spec if w is None else row(w) for _, w in outs],
        out_shape=[vt_shape if w is None else jax.ShapeDtypeStruct((n, w), BF16) for _, w in outs],
        compiler_params=_params(("parallel",)),
        name="in_proj",
    )(x2, cos, sin, gmix, ga, glat, gqb, gkb, gc, w1, wqc, ws, wuq, wukv, wvat, wvbt)


def _attn_items(nq):
    items = [(qi, ki, KIND_DIAG if ki == qi else KIND_PREV if ki == qi - 1 else KIND_FAR)
             for qi in range(nq) for ki in range(qi + 1)]
    items += [items[-1]] * 2
    return np.asarray(items, np.int32).T


def _attn_kernel(*refs, mode, lambda_init, n_items, nq):
    if mode == "diff":
        (tab_ref, q_ref, k_ref, vt_ref, bias_ref, lam_ref, subg_ref, o_ref,
         s_scr, p_scr, acc_scr, m_scr, mt_scr, al_scr) = refs
    else:
        tab_ref, q_ref, k_ref, vt_ref, bias_ref, o_ref, s_scr, p_scr, acc_scr, m_scr, mt_scr, al_scr = refs
    t = T_ATT
    dv = acc_scr.shape[2] - DEN_ROWS
    lo = _lane_lo()
    m_scr[...] = jnp.full(m_scr.shape, NEG, F32)
    acc_scr[...] = jnp.zeros_like(acc_scr)
    if mode == "diff":
        lv = lam_ref[...]
        lam = (jnp.exp(jnp.sum(lv[0:1] * lv[1:2], axis=-1, keepdims=True))
               - jnp.exp(jnp.sum(lv[2:3] * lv[3:4], axis=-1, keepdims=True)) + lambda_init)
        out_gain = subg_ref[...] * (1.0 - lambda_init)

    def rows(idx):
        return pl.ds(pl.multiple_of(idx * t, t), t)

    def scores(n, slot):
        q = q_ref[rows(tab_ref[0, n]), :]
        k = k_ref[rows(tab_ref[1, n]), :]
        kind = tab_ref[2, n]
        for j in range(2):
            if mode == "diff":
                zero = jnp.zeros_like(q)
                qj = jnp.where(lo, q, zero) if j == 0 else jnp.where(lo, zero, q)
                kj = k
            else:
                qj = q[:, j * LANES:(j + 1) * LANES]
                kj = k[:, j * LANES:(j + 1) * LANES]
            s = lax.dot_general(kj, qj, (((1,), (1,)), ((), ())), preferred_element_type=F32)
            s = s + bias_ref[kind]
            s_scr[2 * slot + j] = s
            mt_scr[2 * slot + j] = jnp.max(s, axis=0, keepdims=True)

    def softmax(n, slot, pslot):
        restart = jnp.where(tab_ref[1, n] == 0, NEG, 0.0)
        for j in range(2):
            c = 2 * slot + j
            pc = 2 * pslot + j
            m_old = m_scr[j] + restart
            m_new = jnp.maximum(m_old, mt_scr[c])
            al_scr[pc] = jnp.exp2(m_old - m_new)
            m_scr[j] = m_new
            for r0 in range(0, t, ROW_CHUNK):
                d = s_scr[c, r0:r0 + ROW_CHUNK, :] - m_new
                p_scr[pc, r0:r0 + ROW_CHUNK, :] = jnp.exp2(d.astype(BF16))

    def values(n, pslot):
        qi = tab_ref[0, n]
        v = vt_ref[:, rows(tab_ref[1, n])]
        ones = jnp.ones((DEN_ROWS, t), BF16)
        for j in range(2):
            pc = 2 * pslot + j
            vt = jnp.concatenate([v if mode == "diff" else v[j * dv:(j + 1) * dv, :], ones], axis=0)
            acc_scr[qi, j] = al_scr[pc] * acc_scr[qi, j] + jnp.dot(vt, p_scr[pc], preferred_element_type=F32)

    def finalize(qi):
        outs = [acc_scr[qi, j, :dv, :] / acc_scr[qi, j, dv:dv + 1, :] for j in range(2)]
        if mode == "diff":
            o = outs[0] - lam * outs[1]
            o = o * lax.rsqrt(jnp.sum(o * o, axis=0, keepdims=True) / A_VD + EPS)
            o = o.T * out_gain
        else:
            o = jnp.concatenate(outs, axis=0).T
        o_ref[qi * t:(qi + 1) * t, :] = o.astype(o_ref.dtype)

    scores(0, 0)
    scores(1, 1)

    def body(i, carry):
        for u in range(ITEMS_PER_TRIP):
            n = ITEMS_PER_TRIP * i + u
            s = u % S_SLOTS
            scores(n + 2, (s + 2) % S_SLOTS)
            softmax(n, s, s % P_SLOTS)
            if u > 0:
                values(n - 1, (s + 1) % P_SLOTS)
        values(ITEMS_PER_TRIP * i + ITEMS_PER_TRIP - 1, (ITEMS_PER_TRIP - 1) % P_SLOTS)
        return carry

    lax.fori_loop(0, n_items // ITEMS_PER_TRIP, body, 0)
    for qi in range(nq):
        finalize(qi)


def _attn(q, k, vt, bias, batch, seq, mode, extra=(), lambda_init=0.0):
    n = q.shape[0]
    t = T_ATT
    nq = seq // t
    tab = _attn_items(nq)
    n_items = tab.shape[1] - 2
    assert n_items % ITEMS_PER_TRIP == 0 and ITEMS_PER_TRIP % S_SLOTS == 0 and S_SLOTS % P_SLOTS == 0
    qw = LANES if mode == "diff" else 2 * LANES
    groups = q.shape[1] // qw
    per_head = bias.shape[0] > 1
    in_specs = [pl.BlockSpec(memory_space=pltpu.SMEM),
                pl.BlockSpec((seq, qw), lambda b, g: (b, g)),
                pl.BlockSpec((seq, qw), lambda b, g: (b, g)),
                pl.BlockSpec((None, LANES, seq), lambda b, g: (b, g, 0)),
                pl.BlockSpec((None, 3, t, t), lambda b, g: (g if per_head else 0, 0, 0, 0))]
    in_specs += [pl.BlockSpec(e.shape, lambda b, g: (0, 0)) for e in extra]
    return pl.pallas_call(
        functools.partial(_attn_kernel, mode=mode, lambda_init=lambda_init, n_items=n_items, nq=nq),
        grid=(batch, groups),
        in_specs=in_specs,
        out_specs=pl.BlockSpec((seq, LANES), lambda b, g: (b, g)),
        out_shape=jax.ShapeDtypeStruct((n, groups * LANES), BF16),
        scratch_shapes=[pltpu.VMEM((2 * S_SLOTS, t, t), F32), pltpu.VMEM((2 * P_SLOTS, t, t), BF16),
                        pltpu.VMEM((nq, 2, (A_VD if mode == "diff" else B_VD) + DEN_ROWS, t), F32), pltpu.VMEM((2, 1, t), F32),
                        pltpu.VMEM((2 * S_SLOTS, 1, t), F32), pltpu.VMEM((2 * P_SLOTS, 1, t), F32)],
        compiler_params=_params(("parallel", "parallel")),
        name="attn_" + mode,
    )(jnp.asarray(tab), q, k, vt, bias, *extra)


def _swa_kernel(sink_ref, q_ref, kp_ref, kc_ref, vp_ref, vc_ref, bias_ref, o_ref):
    first_step = pl.program_id(1) == 0
    lo = _lane_lo()
    col = lax.broadcasted_iota(jnp.int32, (WINDOW, 2 * WINDOW), 1)
    keep = (col >= WINDOW) | jnp.logical_not(first_step)
    grp = C_HEADS // C_KV_HEADS
    k_all = jnp.concatenate([kp_ref[...], kc_ref[...]], axis=0)
    v_all = jnp.concatenate([vp_ref[...], vc_ref[...]], axis=0)
    work = [(w, head) for w in range(SWA_WINDOWS) for head in range(C_HEADS)]
    scores = []
    for w, head in work:
        hp, j = divmod(head, 2)
        ksl = slice((head // grp) * LANES, (head // grp + 1) * LANES)
        q = q_ref[w * WINDOW:(w + 1) * WINDOW, hp * LANES:(hp + 1) * LANES]
        zero = jnp.zeros_like(q)
        qj = jnp.where(lo, q, zero) if j == 0 else jnp.where(lo, zero, q)
        k = k_all[w * WINDOW:(w + 2) * WINDOW, ksl]
        scores.append(lax.dot_general(qj, k, (((1,), (1,)), ((), ())), preferred_element_type=F32))
    probs = []
    for (w, head), s in zip(work, scores):
        s = s + bias_ref[head]
        if w == 0:
            s = jnp.where(keep, s, NEG)
        sink = sink_ref[head]
        m = jnp.maximum(jnp.max(s, axis=-1, keepdims=True), sink)
        p = jnp.exp(s - m)
        den = jnp.sum(p, axis=-1, keepdims=True) + jnp.exp(sink - m)
        probs.append((p.astype(BF16), den))
    for w in range(SWA_WINDOWS):
        for hp in range(C_HEADS // 2):
            ksl = slice(((2 * hp) // grp) * LANES, ((2 * hp) // grp + 1) * LANES)
            v = v_all[w * WINDOW:(w + 2) * WINDOW, ksl]
            pair = probs[w * C_HEADS + 2 * hp:w * C_HEADS + 2 * hp + 2]
            outs = [jnp.dot(p, v, preferred_element_type=F32) / den for p, den in pair]
            o_ref[w * WINDOW:(w + 1) * WINDOW, hp * LANES:(hp + 1) * LANES] = (
                jnp.where(lo, outs[0], outs[1]).astype(o_ref.dtype))


def _swa(sinks, q, k, v, bias, batch, seq):
    n = q.shape[0]
    rows = SWA_WINDOWS * WINDOW
    steps = seq // rows
    cur = lambda b, i: (b * steps + i, 0)
    prev = lambda b, i: ((b * steps + i) * SWA_WINDOWS - jnp.minimum(i, 1), 0)
    kw = k.shape[1]
    return pl.pallas_call(
        _swa_kernel,
        grid=(batch, steps),
        in_specs=[pl.BlockSpec(memory_space=pltpu.SMEM),
                  pl.BlockSpec((rows, q.shape[1]), cur),
                  pl.BlockSpec((WINDOW, kw), prev), pl.BlockSpec((rows, kw), cur),
                  pl.BlockSpec((WINDOW, kw), prev), pl.BlockSpec((rows, kw), cur),
                  pl.BlockSpec(bias.shape, lambda b, i: (0, 0, 0))],
        out_specs=pl.BlockSpec((rows, q.shape[1]), cur),
        out_shape=jax.ShapeDtypeStruct((n, q.shape[1]), BF16),
        compiler_params=_params(("parallel", "arbitrary")),
        name="swa",
    )(sinks, q, k, k, v, v, bias)


def _merge_kernel(x_ref, ya_ref, yb_ref, yc_ref, gmix_ref, wg_ref, pa_ref, pb_ref, pc_ref, wo_ref, o_ref):
    x = x_ref[...]
    h = (x * _rms(x, D_MODEL) * gmix_ref[...]).astype(BF16)
    merged = None
    for j, (y_ref, p_ref) in enumerate(((ya_ref, pa_ref), (yb_ref, pb_ref), (yc_ref, pc_ref))):
        gate = jnp.dot(h, wg_ref[:, j * D_MODEL:(j + 1) * D_MODEL], preferred_element_type=F32)
        term = jax.nn.sigmoid(gate) * jnp.dot(y_ref[...], p_ref[...], preferred_element_type=F32)
        merged = term if merged is None else merged + term
    o_ref[...] = x + jnp.dot(merged.astype(BF16), wo_ref[...], preferred_element_type=F32)


def _merge(x2, ya, yb, yc, gmix, wg, pa, pb, pc, wo):
    n = x2.shape[0]
    tm = TM_MERGE
    row = lambda w: pl.BlockSpec((tm, w), lambda i: (i, 0))
    return pl.pallas_call(
        _merge_kernel,
        grid=(n // tm,),
        in_specs=[row(D_MODEL), row(ya.shape[1]), row(yb.shape[1]), row(yc.shape[1]),
                  _const_spec(gmix.shape), _const_spec(wg.shape), _const_spec(pa.shape),
                  _const_spec(pb.shape), _const_spec(pc.shape), _const_spec(wo.shape)],
        out_specs=row(D_MODEL),
        out_shape=jax.ShapeDtypeStruct((n, D_MODEL), F32),
        compiler_params=_params(("parallel",)),
        name="merge",
    )(x2, ya, yb, yc, gmix, wg, pa, pb, pc, wo)


def _ffn_kernel(x_ref, g_ref, wup_ref, cw_ref, cb_ref, wdn_ref, o_ref, ubuf, act, carry, *, tiles_per_seq):
    tm = x_ref.shape[0]
    fc = FF_CHUNK

    @pl.when(pl.program_id(0) % tiles_per_seq == 0)
    def _():
        carry[...] = jnp.zeros_like(carry)

    x = x_ref[...]
    h = (x * _rms(x, D_MODEL) * g_ref[...]).astype(BF16)

    def cols(ref, rows, j):
        return jnp.concatenate([ref[rows, j * fc:(j + 1) * fc], ref[rows, D_FF + j * fc:D_FF + (j + 1) * fc]],
                               axis=1)

    def up(j):
        for half, base in enumerate((j * fc, D_FF + j * fc)):
            ubuf[j, :, half * fc:(half + 1) * fc] = jnp.dot(
                h, wup_ref[:, base:base + fc], preferred_element_type=F32)

    def conv_act(j):
        ext = jnp.concatenate([carry[j], ubuf[j]], axis=0)
        carry[j] = ubuf[j, tm - HALO:tm, :]
        y = cols(cb_ref, slice(0, 1), j)
        for tap in range(CONV_W):
            shift = CONV_W - 1 - tap
            tok = ext if shift == 0 else pltpu.roll(ext, shift, axis=0)
            y = y + cols(cw_ref, slice(tap, tap + 1), j) * tok[HALO:HALO + tm, :]
        gate = y[:, :fc]
        half = 0.5 * gate
        act[:, j * fc:(j + 1) * fc] = ((half + half * jnp.tanh(half)) * y[:, fc:]).astype(BF16)

    up(0)
    for j in range(N_FF_CHUNKS):
        if j + 1 < N_FF_CHUNKS:
            up(j + 1)
        conv_act(j)
    o_ref[...] = x + jnp.dot(act[...], wdn_ref[...], preferred_element_type=F32)


def _ffn(x2, g, wup, cw, cb, wdn, seq):
    n = x2.shape[0]
    tm = TM_FFN
    row = pl.BlockSpec((tm, D_MODEL), lambda i: (i, 0))
    return pl.pallas_call(
        functools.partial(_ffn_kernel, tiles_per_seq=seq // tm),
        grid=(n // tm,),
        in_specs=[row, _const_spec(g.shape), _const_spec(wup.shape), _const_spec(cw.shape),
                  _const_spec(cb.shape), _const_spec(wdn.shape)],
        out_specs=row,
        out_shape=jax.ShapeDtypeStruct((n, D_MODEL), F32),
        scratch_shapes=[pltpu.VMEM((N_FF_CHUNKS, tm, 2 * FF_CHUNK), F32),
                        pltpu.VMEM((tm, D_FF), BF16),
                        pltpu.VMEM((N_FF_CHUNKS, HALO, 2 * FF_CHUNK), F32)],
        compiler_params=_params(("arbitrary",)),
        name="ffn",
    )(x2, g, wup, cw, cb, wdn)


def _layer_params(l, w_in, a_q_g, a_k_g, b_q_a_g, b_kv_a_g, b_w_uq, b_w_ukv, b_qn_g, b_qr_g, b_kn_g,
                  b_kr_g, c_q_g, c_k_g, w_up, conv_w, conv_b):
    w = w_in[l]
    o_kpe = 3 * 512 + B_Q_RANK + B_KV_RANK
    o_qc = o_kpe + B_ROPE
    o_kc = o_qc + C_HEADS * C_HD
    o_vc = o_kc + C_KV_HEADS * C_HD
    o_g = o_vc + C_KV_HEADS * C_HD
    w1 = w[:, :o_kpe].astype(BF16)
    wqc = w[:, o_qc:o_kc].astype(BF16)
    wg = w[:, o_g:].astype(BF16)
    kpe_w = w[:, o_kpe:o_qc].astype(BF16)
    kc_w = w[:, o_kc:o_vc].astype(BF16)
    vc_w = w[:, o_vc:o_g].astype(BF16)
    z = lambda c: jnp.zeros((w.shape[0], c), BF16)
    r = B_ROPE // 2
    pad = LANES - B_NOPE - B_ROPE
    dup = lambda t: jnp.concatenate([t[:, :C_HD], t[:, :C_HD], t[:, C_HD:], t[:, C_HD:]], axis=1)
    ws = jnp.concatenate([
        z(B_NOPE), kpe_w, z(pad),
        z(B_NOPE), kpe_w[:, r:], kpe_w[:, :r], z(pad),
        dup(kc_w), dup(vc_w)], axis=1)

    uq = b_w_uq[l].reshape(B_Q_RANK, B_HEADS, B_NOPE + B_ROPE)
    nope, pe = uq[..., :B_NOPE], uq[..., B_NOPE:]
    zq = lambda c: jnp.zeros((B_Q_RANK, B_HEADS, c), uq.dtype)
    wuq = jnp.concatenate([
        jnp.concatenate([nope, pe, zq(pad)], axis=-1).reshape(B_Q_RANK, B_HEADS * LANES),
        jnp.concatenate([zq(B_NOPE), pe[..., r:], pe[..., :r], zq(pad)], axis=-1).reshape(B_Q_RANK, B_HEADS * LANES),
    ], axis=1).astype(BF16)

    ukv = b_w_ukv[l].reshape(B_KV_RANK, B_HEADS, B_NOPE + B_VD)
    zk = jnp.zeros((B_KV_RANK, B_HEADS, LANES - B_NOPE), ukv.dtype)
    wukv = jnp.concatenate([ukv[..., :B_NOPE], zk], axis=-1).reshape(B_KV_RANK, B_HEADS * LANES).astype(BF16)
    wvbt = ukv[..., B_NOPE:].reshape(B_KV_RANK, B_HEADS * B_VD).T.astype(BF16)
    wvat = w[:, 2 * 512:3 * 512].T.astype(BF16)

    scale_a = A_HD ** -0.5 * LOG2E
    scale_b = (B_NOPE + B_ROPE) ** -0.5 * LOG2E
    scale_c = C_HD ** -0.5
    ga = jnp.stack([jnp.tile(a_q_g[l], 2 * A_HEADS) * scale_a, jnp.tile(a_k_g[l], 2 * A_HEADS)])
    glat = jnp.concatenate([b_q_a_g[l], b_kv_a_g[l]])[None, :]
    zl = lambda c: jnp.zeros((c,), F32)
    qr, kr = b_qr_g[l], b_kr_g[l]
    gqb = jnp.stack([
        jnp.tile(jnp.concatenate([b_qn_g[l], qr, zl(pad)]), B_HEADS),
        jnp.tile(jnp.concatenate([zl(B_NOPE), qr[r:], qr[:r], zl(pad)]), B_HEADS)]) * scale_b
    gkb = jnp.stack([
        jnp.concatenate([b_kn_g[l], zl(LANES - B_NOPE)]),
        jnp.concatenate([zl(B_NOPE), kr, zl(pad)]),
        jnp.concatenate([zl(B_NOPE), kr[r:], kr[:r], zl(pad)])])
    gc = jnp.concatenate([jnp.tile(c_q_g[l], C_HEADS) * scale_c, jnp.tile(c_k_g[l], 2 * C_KV_HEADS)])[None, :]

    return dict(w1=w1, wqc=wqc, ws=ws, wg=wg, wuq=wuq, wukv=wukv, wvat=wvat, wvbt=wvbt, ga=ga, glat=glat,
                gqb=gqb, gkb=gkb, gc=gc, wup=_to_bf16(w_up, l, 256), cw=conv_w[l, :, 0, :], cb=conv_b[l][None, :])


def kernel(x, positions, rel_bias_table, ln_mix_g, w_in, a_q_g, a_k_g, a_lam_q1, a_lam_k1, a_lam_q2, a_lam_k2, a_subln_g, b_q_a_g, b_kv_a_g, b_w_uq, b_w_ukv, b_qn_g, b_qr_g, b_kn_g, b_kr_g, c_q_g, c_k_g, c_sinks, p_a, p_b, p_c, w_o, ln_ffn_g, w_up, conv_w, conv_b, w_down):
    batch, seq, d = x.shape
    n = batch * seq
    assert d == D_MODEL and seq % T_ATT == 0 and n % TM_IN == 0 and seq % TM_FFN == 0
    x2 = x.reshape(n, d)
    cos, sin = _rope_tables(positions.reshape(n, 1))
    bias_a, mask_b, bias_c = _bias_tiles(rel_bias_table)
    for l in range(DEPTH):
        lambda_init = 0.8 - 0.6 * math.exp(-0.3 * l)
        p = _layer_params(l, w_in, a_q_g, a_k_g, b_q_a_g, b_kv_a_g, b_w_uq, b_w_ukv, b_qn_g, b_qr_g,
                          b_kn_g, b_kr_g, c_q_g, c_k_g, w_up, conv_w, conv_b)
        gmix = ln_mix_g[l][None, :]
        qa, ka, va, qb, kb, vb, qc, kc, vc = _in_proj(
            x2, cos, sin, gmix, p["ga"], p["glat"], p["gqb"], p["gkb"], p["gc"],
            p["w1"], p["wqc"], p["ws"], p["wuq"], p["wukv"], p["wvat"], p["wvbt"], batch, seq)
        lam = jnp.stack([a_lam_q1[l], a_lam_k1[l], a_lam_q2[l], a_lam_k2[l]])
        ya = _attn(qa, ka, va, bias_a, batch, seq, "diff", extra=(lam, a_subln_g[l][None, :]),
                   lambda_init=lambda_init)
        yb = _attn(qb, kb, vb, mask_b, batch, seq, "mla")
        yc = _swa(c_sinks[l], qc, kc, vc, bias_c, batch, seq)
        x2 = _merge(x2, ya, yb, yc, gmix, p["wg"], _to_bf16(p_a, l, 256), _to_bf16(p_b, l, 256),
                    _to_bf16(p_c, l, 256), _to_bf16(w_o, l, 256))
        x2 = _ffn(x2, ln_ffn_g[l][None, :], p["wup"], p["cw"], p["cb"], _to_bf16(w_down, l, 704), seq)
    return x2.reshape(batch, seq, d)
```

```python
import functools
import math

import jax
import jax.numpy as jnp
import numpy as np
from jax import lax
from jax.experimental import pallas as pl
from jax.experimental.pallas import tpu as pltpu

F32 = jnp.float32
BF16 = jnp.bfloat16

D_MODEL = 1024
DEPTH = 2
EPS = 1e-6
A_HEADS = 4
A_HD = 64
A_VD = 2 * A_HD
B_HEADS = 8
B_Q_RANK = 256
B_KV_RANK = 128
B_NOPE = 64
B_ROPE = 32
B_VD = 64
ROPE_THETA = 10000.0
C_HEADS = 8
C_KV_HEADS = 2
C_HD = 64
WINDOW = 128
N_BUCKETS = 32
MAX_DIST = 128
D_FF = 2816
CONV_W = 3

LANES = 128
HALF = LANES // 2
NEG = -1e30
LOG2E = math.log2(math.e)
ROW_CHUNK = 32
DEN_ROWS = 16
KIND_FAR, KIND_PREV, KIND_DIAG = 0, 1, 2
S_SLOTS = 4
P_SLOTS = 2
ITEMS_PER_TRIP = 12
VMEM_LIMIT = 56 * 1024 * 1024

T_ATT = 512
TM_IN = 512
TM_MERGE = 512
TM_FFN = 512
FF_CHUNK = 256
N_FF_CHUNKS = D_FF // FF_CHUNK
HALO = 8
SWA_WINDOWS = 4
ROPE_GROUP = 8

_SEG = {"qa": (0, 0, 512), "ka": (0, 512, 1024), "cq": (0, 1536, 1792),
        "ckv": (0, 1792, 1920), "qc": (1, 0, 512),
        "kpe_both": (2, 0, 256), "kc": (2, 256, 512), "vc": (2, 512, 768)}


def _params(sem, vmem=VMEM_LIMIT):
    return pltpu.CompilerParams(dimension_semantics=sem, vmem_limit_bytes=vmem)


def _const_spec(shape):
    nd = len(shape)
    return pl.BlockSpec(shape, lambda *_: (0,) * nd, pipeline_mode=pl.Buffered(1))


def _lane_lo():
    return lax.broadcasted_iota(jnp.int32, (1, LANES), 1) < HALF


def _cast_kernel(w_ref, o_ref):
    o_ref[...] = w_ref[...].astype(o_ref.dtype)


def _to_bf16(w, layer, row_block):
    _, rows, cols = w.shape
    assert rows % row_block == 0
    return pl.pallas_call(
        _cast_kernel,
        grid=(rows // row_block,),
        in_specs=[pl.BlockSpec((None, row_block, cols), lambda i: (layer, i, 0))],
        out_specs=pl.BlockSpec((row_block, cols), lambda i: (i, 0)),
        out_shape=jax.ShapeDtypeStruct((rows, cols), BF16),
        compiler_params=_params(("parallel",)),
        name="to_bf16",
    )(w)


def _rope_kernel(pos_ref, inv_ref, sign_ref, cos_ref, sin_ref):
    groups = pos_ref.shape[0]
    inv = inv_ref[...]
    sign = sign_ref[...]
    base = pos_ref[...].astype(F32) * inv
    cos_b, sin_b = jnp.cos(base), jnp.sin(base)
    step = lax.broadcasted_iota(jnp.int32, (ROPE_GROUP, LANES), 0).astype(F32) * inv
    cos_s, sin_s = jnp.cos(step), jnp.sin(step)
    for i in range(groups):
        rows = slice(i * ROPE_GROUP, (i + 1) * ROPE_GROUP)
        cb, sb = cos_b[i:i + 1, :], sin_b[i:i + 1, :]
        cos_ref[rows, :] = cb * cos_s - sb * sin_s
        sin_ref[rows, :] = (sb * cos_s + cb * sin_s) * sign


def _rope_tables(positions):
    batch, seq = positions.shape
    n = batch * seq
    inv = 1.0 / (ROPE_THETA ** (jnp.arange(0, B_ROPE, 2, dtype=F32) / B_ROPE))
    z = jnp.zeros((B_NOPE,), F32)
    zp = jnp.zeros((LANES - B_NOPE - B_ROPE,), F32)
    inv_pat = jnp.concatenate([z, inv, inv, zp])[None, :]
    ones = jnp.ones((B_ROPE // 2,), F32)
    sign_pat = jnp.concatenate([z, -ones, ones, zp])[None, :]
    tm = 1024
    assert seq % tm == 0 and tm % ROPE_GROUP == 0
    pos_groups = positions.reshape(n // ROPE_GROUP, ROPE_GROUP)[:, :1]
    return pl.pallas_call(
        _rope_kernel,
        grid=(n // tm,),
        in_specs=[pl.BlockSpec((tm // ROPE_GROUP, 1), lambda i: (i, 0)),
                  pl.BlockSpec((1, LANES), lambda i: (0, 0)),
                  pl.BlockSpec((1, LANES), lambda i: (0, 0))],
        out_specs=[pl.BlockSpec((tm, LANES), lambda i: (i, 0))] * 2,
        out_shape=[jax.ShapeDtypeStruct((n, LANES), F32)] * 2,
        compiler_params=_params(("parallel",)),
        name="rope_tables",
    )(pos_groups, inv_pat, sign_pat)


def _bucket(rel):
    n = jnp.maximum(rel, 0)
    max_exact = N_BUCKETS // 2
    nf = jnp.maximum(n, 1).astype(F32)
    large = max_exact + (jnp.log(nf / max_exact) / math.log(MAX_DIST / max_exact)
                         * (N_BUCKETS - max_exact)).astype(jnp.int32)
    large = jnp.minimum(large, N_BUCKETS - 1)
    return jnp.where(n < max_exact, n, large)


def _lookup(tab_ref, bucket, col):
    out = jnp.zeros(bucket.shape, F32)
    for k in range(N_BUCKETS):
        out = jnp.where(bucket == k, tab_ref[k, col], out)
    return out


def _bias_a_kernel(tab_ref, out_ref):
    h = pl.program_id(0)
    t = out_ref.shape[-1]
    key = lax.broadcasted_iota(jnp.int32, (t, t), 0)
    qry = lax.broadcasted_iota(jnp.int32, (t, t), 1)
    far = tab_ref[N_BUCKETS - 1, h]
    rel = qry - key
    out_ref[0, KIND_FAR] = jnp.zeros((t, t), F32)
    out_ref[0, KIND_PREV] = (_lookup(tab_ref, _bucket(rel + t), h) - far) * LOG2E
    out_ref[0, KIND_DIAG] = jnp.where(rel >= 0, (_lookup(tab_ref, _bucket(rel), h) - far) * LOG2E, NEG)


def _mask_kernel(out_ref):
    t = out_ref.shape[-1]
    key = lax.broadcasted_iota(jnp.int32, (t, t), 0)
    qry = lax.broadcasted_iota(jnp.int32, (t, t), 1)
    out_ref[0, KIND_FAR] = jnp.zeros((t, t), F32)
    out_ref[0, KIND_PREV] = jnp.zeros((t, t), F32)
    out_ref[0, KIND_DIAG] = jnp.where(key <= qry, 0.0, NEG)


def _bias_c_kernel(tab_ref, out_ref):
    h = pl.program_id(0)
    row = lax.broadcasted_iota(jnp.int32, (WINDOW, 2 * WINDOW), 0)
    col = lax.broadcasted_iota(jnp.int32, (WINDOW, 2 * WINDOW), 1)
    rel = row + WINDOW - col
    valid = (rel >= 0) & (rel < WINDOW)
    out_ref[0] = jnp.where(valid, _lookup(tab_ref, _bucket(rel), h + A_HEADS), NEG)


def _bias_tiles(table):
    smem = pl.BlockSpec(memory_space=pltpu.SMEM)
    bias_a = pl.pallas_call(
        _bias_a_kernel,
        grid=(A_HEADS,),
        in_specs=[smem],
        out_specs=pl.BlockSpec((1, 3, T_ATT, T_ATT), lambda h: (h, 0, 0, 0)),
        out_shape=jax.ShapeDtypeStruct((A_HEADS, 3, T_ATT, T_ATT), F32),
        compiler_params=_params(("parallel",)),
        name="bias_a",
    )(table)
    mask_b = pl.pallas_call(
        _mask_kernel,
        out_shape=jax.ShapeDtypeStruct((1, 3, T_ATT, T_ATT), F32),
        compiler_params=pltpu.CompilerParams(vmem_limit_bytes=VMEM_LIMIT),
        name="mask_b",
    )()
    bias_c = pl.pallas_call(
        _bias_c_kernel,
        grid=(C_HEADS,),
        in_specs=[smem],
        out_specs=pl.BlockSpec((1, WINDOW, 2 * WINDOW), lambda h: (h, 0, 0)),
        out_shape=jax.ShapeDtypeStruct((C_HEADS, WINDOW, 2 * WINDOW), F32),
        compiler_params=_params(("parallel",)),
        name="bias_c",
    )(table)
    return bias_a, mask_b, bias_c


def _rms(t, width):
    return lax.rsqrt(jnp.sum(t * t, axis=-1, keepdims=True) / width + EPS)


def _in_kernel(x_ref, cos_ref, sin_ref, gmix_ref, ga_ref, glat_ref, gqb_ref, gkb_ref, gc_ref,
               w1_ref, wqc_ref, ws_ref, wuq_ref, wukv_ref, wvat_ref, wvbt_ref,
               qa_ref, ka_ref, va_ref, qb_ref, kb_ref, vb_ref, qc_ref, kc_ref, vc_ref):
    x = x_ref[...]
    h = (x * _rms(x, D_MODEL) * gmix_ref[...]).astype(BF16)
    lo = _lane_lo()
    w_refs = (w1_ref, wqc_ref, ws_ref)

    def proj(name):
        which, a, b = _SEG[name]
        return jnp.dot(h, w_refs[which][:, a:b], preferred_element_type=F32)

    def norm_halves(t, g, out_ref):
        for j in range(t.shape[1] // LANES):
            sl = slice(j * LANES, (j + 1) * LANES)
            tj = t[:, sl]
            sq = tj * tj
            s_lo = jnp.sum(jnp.where(lo, sq, 0.0), axis=-1, keepdims=True)
            s_hi = jnp.sum(jnp.where(lo, 0.0, sq), axis=-1, keepdims=True)
            r = jnp.where(lo, lax.rsqrt(s_lo / HALF + EPS), lax.rsqrt(s_hi / HALF + EPS))
            out_ref[:, sl] = (tj * r * g[:, sl]).astype(out_ref.dtype)

    glat = glat_ref[...]
    cq = proj("cq")
    ckv = proj("ckv")
    kpe_both = proj("kpe_both")
    kpe = kpe_both[:, :LANES]
    kpe_sw = kpe_both[:, LANES:]
    cqn = (cq * _rms(cq, B_Q_RANK) * glat[:, :B_Q_RANK]).astype(BF16)
    ckvn = (ckv * _rms(ckv, B_KV_RANK) * glat[:, B_Q_RANK:]).astype(BF16)
    uq = jnp.dot(cqn, wuq_ref[...], preferred_element_type=F32)
    ukv = jnp.dot(ckvn, wukv_ref[...], preferred_element_type=F32)
    qa = proj("qa")
    ka = proj("ka")
    qc = proj("qc")
    kc = proj("kc")
    vc = proj("vc")
    nt = (((1,), (1,)), ((), ()))
    va_ref[...] = lax.dot_general(wvat_ref[...], h, nt, preferred_element_type=F32).astype(BF16)
    vb_ref[...] = lax.dot_general(wvbt_ref[...], ckvn, nt, preferred_element_type=F32).astype(BF16)

    cos = cos_ref[...]
    sin = sin_ref[...]
    hw = B_HEADS * LANES
    for j in range(B_HEADS):
        sl = slice(j * LANES, (j + 1) * LANES)
        raw = uq[:, sl]
        raw_sw = uq[:, hw + j * LANES: hw + (j + 1) * LANES]
        sq = raw * raw
        s_n = jnp.sum(jnp.where(lo, sq, 0.0), axis=-1, keepdims=True)
        s_r = jnp.sum(jnp.where(lo, 0.0, sq), axis=-1, keepdims=True)
        r = jnp.where(lo, lax.rsqrt(s_n / B_NOPE + EPS), lax.rsqrt(s_r / B_ROPE + EPS))
        out = r * (raw * gqb_ref[0:1, sl] * cos + raw_sw * gqb_ref[1:2, sl] * sin)
        qb_ref[:, sl] = out.astype(BF16)

    kpe_out = _rms(kpe, B_ROPE) * (kpe * gkb_ref[1:2, :] * cos + kpe_sw * gkb_ref[2:3, :] * sin)
    for j in range(B_HEADS):
        sl = slice(j * LANES, (j + 1) * LANES)
        raw = ukv[:, sl]
        kb_ref[:, sl] = (raw * _rms(raw, B_NOPE) * gkb_ref[0:1, :] + kpe_out).astype(BF16)

    norm_halves(qa, ga_ref[0:1, :], qa_ref)
    norm_halves(ka, ga_ref[1:2, :], ka_ref)
    gc = gc_ref[...]
    norm_halves(qc, gc[:, :C_HEADS * C_HD], qc_ref)
    norm_halves(kc, gc[:, C_HEADS * C_HD:], kc_ref)
    vc_ref[...] = vc.astype(BF16)


def _in_proj(x2, cos, sin, gmix, ga, glat, gqb, gkb, gc, w1, wqc, ws, wuq, wukv, wvat, wvbt, batch, seq):
    n = x2.shape[0]
    tm = TM_IN
    nps = seq // tm
    row = lambda w: pl.BlockSpec((tm, w), lambda i: (i, 0))
    outs = (("qa", 512), ("ka", 512), ("va", None), ("qb", 1024), ("kb", 1024), ("vb", None),
            ("qc", 512), ("kc", 256), ("vc", 256))
    vt_spec = pl.BlockSpec((None, 512, tm), lambda i: (i // nps, 0, i % nps))
    vt_shape = jax.ShapeDtypeStruct((batch, 512, seq), BF16)
    return pl.pallas_call(
        _in_kernel,
        grid=(n // tm,),
        in_specs=[row(D_MODEL), row(LANES), row(LANES),
                  _const_spec(gmix.shape), _const_spec(ga.shape), _const_spec(glat.shape),
                  _const_spec(gqb.shape), _const_spec(gkb.shape), _const_spec(gc.shape),
                  _const_spec(w1.shape), _const_spec(wqc.shape), _const_spec(ws.shape),
                  _const_spec(wuq.shape), _const_spec(wukv.shape),
                  _const_spec(wvat.shape), _const_spec(wvbt.shape)],
        out_specs=[vt_spec if w is None else row(w) for _, w in outs],
        out_shape=[vt_shape if w is None else jax.ShapeDtypeStruct((n, w), BF16) for _, w in outs],
        compiler_params=_params(("parallel",)),
        name="in_proj",
    )(x2, cos, sin, gmix, ga, glat, gqb, gkb, gc, w1, wqc, ws, wuq, wukv, wvat, wvbt)


def _attn_items(nq):
    items = [(qi, ki, KIND_DIAG if ki == qi else KIND_PREV if ki == qi - 1 else KIND_FAR)
             for qi in range(nq) for ki in range(qi + 1)]
    items += [items[-1]] * 2
    return np.asarray(items, np.int32).T


def _attn_kernel(*refs, mode, lambda_init, n_items, nq):
    if mode == "diff":
        (tab_ref, q_ref, k_ref, vt_ref, bias_ref, lam_ref, subg_ref, o_ref,
         s_scr, p_scr, acc_scr, m_scr, mt_scr, al_scr) = refs
    else:
        tab_ref, q_ref, k_ref, vt_ref, bias_ref, o_ref, s_scr, p_scr, acc_scr, m_scr, mt_scr, al_scr = refs
    t = T_ATT
    dv = acc_scr.shape[2] - DEN_ROWS
    lo = _lane_lo()
    m_scr[...] = jnp.full(m_scr.shape, NEG, F32)
    acc_scr[...] = jnp.zeros_like(acc_scr)
    if mode == "diff":
        lv = lam_ref[...]
        lam = (jnp.exp(jnp.sum(lv[0:1] * lv[1:2], axis=-1, keepdims=True))
               - jnp.exp(jnp.sum(lv[2:3] * lv[3:4], axis=-1, keepdims=True)) + lambda_init)
        out_gain = subg_ref[...] * (1.0 - lambda_init)

    def rows(idx):
        return pl.ds(pl.multiple_of(idx * t, t), t)

    def scores(n, slot):
        q = q_ref[rows(tab_ref[0, n]), :]
        k = k_ref[rows(tab_ref[1, n]), :]
        kind = tab_ref[2, n]
        for j in range(2):
            if mode == "diff":
                zero = jnp.zeros_like(q)
                qj = jnp.where(lo, q, zero) if j == 0 else jnp.where(lo, zero, q)
                kj = k
            else:
                qj = q[:, j * LANES:(j + 1) * LANES]
                kj = k[:, j * LANES:(j + 1) * LANES]
            s = lax.dot_general(kj, qj, (((1,), (1,)), ((), ())), preferred_element_type=F32)
            s = s + bias_ref[kind]
            s_scr[2 * slot + j] = s
            mt_scr[2 * slot + j] = jnp.max(s, axis=0, keepdims=True)

    def softmax(n, slot, pslot):
        restart = jnp.where(tab_ref[1, n] == 0, NEG, 0.0)
        for j in range(2):
            c = 2 * slot + j
            pc = 2 * pslot + j
            m_old = m_scr[j] + restart
            m_new = jnp.maximum(m_old, mt_scr[c])
            al_scr[pc] = jnp.exp2(m_old - m_new)
            m_scr[j] = m_new
            for r0 in range(0, t, ROW_CHUNK):
                d = s_scr[c, r0:r0 + ROW_CHUNK, :] - m_new
                p_scr[pc, r0:r0 + ROW_CHUNK, :] = jnp.exp2(d.astype(BF16))

    def values(n, pslot):
        qi = tab_ref[0, n]
        v = vt_ref[:, rows(tab_ref[1, n])]
        ones = jnp.ones((DEN_ROWS, t), BF16)
        for j in range(2):
            pc = 2 * pslot + j
            vt = jnp.concatenate([v if mode == "diff" else v[j * dv:(j + 1) * dv, :], ones], axis=0)
            acc_scr[qi, j] = al_scr[pc] * acc_scr[qi, j] + jnp.dot(vt, p_scr[pc], preferred_element_type=F32)

    def finalize(qi):
        outs = [acc_scr[qi, j, :dv, :] / acc_scr[qi, j, dv:dv + 1, :] for j in range(2)]
        if mode == "diff":
            o = outs[0] - lam * outs[1]
            o = o * lax.rsqrt(jnp.sum(o * o, axis=0, keepdims=True) / A_VD + EPS)
            o = o.T * out_gain
        else:
            o = jnp.concatenate(outs, axis=0).T
        o_ref[qi * t:(qi + 1) * t, :] = o.astype(o_ref.dtype)

    scores(0, 0)
    scores(1, 1)

    def body(i, carry):
        for u in range(ITEMS_PER_TRIP):
            n = ITEMS_PER_TRIP * i + u
            s = u % S_SLOTS
            scores(n + 2, (s + 2) % S_SLOTS)
            softmax(n, s, s % P_SLOTS)
            if u > 0:
                values(n - 1, (s + 1) % P_SLOTS)
        values(ITEMS_PER_TRIP * i + ITEMS_PER_TRIP - 1, (ITEMS_PER_TRIP - 1) % P_SLOTS)
        return carry

    lax.fori_loop(0, n_items // ITEMS_PER_TRIP, body, 0)
    for qi in range(nq):
        finalize(qi)


def _attn(q, k, vt, bias, batch, seq, mode, extra=(), lambda_init=0.0):
    n = q.shape[0]
    t = T_ATT
    nq = seq // t
    tab = _attn_items(nq)
    n_items = tab.shape[1] - 2
    assert n_items % ITEMS_PER_TRIP == 0 and ITEMS_PER_TRIP % S_SLOTS == 0 and S_SLOTS % P_SLOTS == 0
    qw = LANES if mode == "diff" else 2 * LANES
    groups = q.shape[1] // qw
    per_head = bias.shape[0] > 1
    in_specs = [pl.BlockSpec(memory_space=pltpu.SMEM),
                pl.BlockSpec((seq, qw), lambda b, g: (b, g)),
                pl.BlockSpec((seq, qw), lambda b, g: (b, g)),
                pl.BlockSpec((None, LANES, seq), lambda b, g: (b, g, 0)),
                pl.BlockSpec((None, 3, t, t), lambda b, g: (g if per_head else 0, 0, 0, 0))]
    in_specs += [pl.BlockSpec(e.shape, lambda b, g: (0, 0)) for e in extra]
    return pl.pallas_call(
        functools.partial(_attn_kernel, mode=mode, lambda_init=lambda_init, n_items=n_items, nq=nq),
        grid=(batch, groups),
        in_specs=in_specs,
        out_specs=pl.BlockSpec((seq, LANES), lambda b, g: (b, g)),
        out_shape=jax.ShapeDtypeStruct((n, groups * LANES), BF16),
        scratch_shapes=[pltpu.VMEM((2 * S_SLOTS, t, t), F32), pltpu.VMEM((2 * P_SLOTS, t, t), BF16),
                        pltpu.VMEM((nq, 2, (A_VD if mode == "diff" else B_VD) + DEN_ROWS, t), F32), pltpu.VMEM((2, 1, t), F32),
                        pltpu.VMEM((2 * S_SLOTS, 1, t), F32), pltpu.VMEM((2 * P_SLOTS, 1, t), F32)],
        compiler_params=_params(("parallel", "parallel")),
        name="attn_" + mode,
    )(jnp.asarray(tab), q, k, vt, bias, *extra)


def _swa_kernel(sink_ref, q_ref, kp_ref, kc_ref, vp_ref, vc_ref, bias_ref, o_ref):
    first_step = pl.program_id(1) == 0
    lo = _lane_lo()
    col = lax.broadcasted_iota(jnp.int32, (WINDOW, 2 * WINDOW), 1)
    keep = (col >= WINDOW) | jnp.logical_not(first_step)
    grp = C_HEADS // C_KV_HEADS
    k_all = jnp.concatenate([kp_ref[...], kc_ref[...]], axis=0)
    v_all = jnp.concatenate([vp_ref[...], vc_ref[...]], axis=0)
    work = [(w, head) for w in range(SWA_WINDOWS) for head in range(C_HEADS)]
    scores = []
    for w, head in work:
        hp, j = divmod(head, 2)
        ksl = slice((head // grp) * LANES, (head // grp + 1) * LANES)
        q = q_ref[w * WINDOW:(w + 1) * WINDOW, hp * LANES:(hp + 1) * LANES]
        zero = jnp.zeros_like(q)
        qj = jnp.where(lo, q, zero) if j == 0 else jnp.where(lo, zero, q)
        k = k_all[w * WINDOW:(w + 2) * WINDOW, ksl]
        scores.append(lax.dot_general(qj, k, (((1,), (1,)), ((), ())), preferred_element_type=F32))
    probs = []
    for (w, head), s in zip(work, scores):
        s = s + bias_ref[head]
        if w == 0:
            s = jnp.where(keep, s, NEG)
        sink = sink_ref[head]
        m = jnp.maximum(jnp.max(s, axis=-1, keepdims=True), sink)
        p = jnp.exp(s - m)
        den = jnp.sum(p, axis=-1, keepdims=True) + jnp.exp(sink - m)
        probs.append((p.astype(BF16), den))
    for w in range(SWA_WINDOWS):
        for hp in range(C_HEADS // 2):
            ksl = slice(((2 * hp) // grp) * LANES, ((2 * hp) // grp + 1) * LANES)
            v = v_all[w * WINDOW:(w + 2) * WINDOW, ksl]
            pair = probs[w * C_HEADS + 2 * hp:w * C_HEADS + 2 * hp + 2]
            outs = [jnp.dot(p, v, preferred_element_type=F32) / den for p, den in pair]
            o_ref[w * WINDOW:(w + 1) * WINDOW, hp * LANES:(hp + 1) * LANES] = (
                jnp.where(lo, outs[0], outs[1]).astype(o_ref.dtype))


def _swa(sinks, q, k, v, bias, batch, seq):
    n = q.shape[0]
    rows = SWA_WINDOWS * WINDOW
    steps = seq // rows
    cur = lambda b, i: (b * steps + i, 0)
    prev = lambda b, i: ((b * steps + i) * SWA_WINDOWS - jnp.minimum(i, 1), 0)
    kw = k.shape[1]
    return pl.pallas_call(
        _swa_kernel,
        grid=(batch, steps),
        in_specs=[pl.BlockSpec(memory_space=pltpu.SMEM),
                  pl.BlockSpec((rows, q.shape[1]), cur),
                  pl.BlockSpec((WINDOW, kw), prev), pl.BlockSpec((rows, kw), cur),
                  pl.BlockSpec((WINDOW, kw), prev), pl.BlockSpec((rows, kw), cur),
                  pl.BlockSpec(bias.shape, lambda b, i: (0, 0, 0))],
        out_specs=pl.BlockSpec((rows, q.shape[1]), cur),
        out_shape=jax.ShapeDtypeStruct((n, q.shape[1]), BF16),
        compiler_params=_params(("parallel", "arbitrary")),
        name="swa",
    )(sinks, q, k, k, v, v, bias)


def _merge_kernel(x_ref, ya_ref, yb_ref, yc_ref, gmix_ref, wg_ref, pa_ref, pb_ref, pc_ref, wo_ref, o_ref):
    x = x_ref[...]
    h = (x * _rms(x, D_MODEL) * gmix_ref[...]).astype(BF16)
    merged = None
    for j, (y_ref, p_ref) in enumerate(((ya_ref, pa_ref), (yb_ref, pb_ref), (yc_ref, pc_ref))):
        gate = jnp.dot(h, wg_ref[:, j * D_MODEL:(j + 1) * D_MODEL], preferred_element_type=F32)
        term = jax.nn.sigmoid(gate) * jnp.dot(y_ref[...], p_ref[...], preferred_element_type=F32)
        merged = term if merged is None else merged + term
    o_ref[...] = x + jnp.dot(merged.astype(BF16), wo_ref[...], preferred_element_type=F32)


def _merge(x2, ya, yb, yc, gmix, wg, pa, pb, pc, wo):
    n = x2.shape[0]
    tm = TM_MERGE
    row = lambda w: pl.BlockSpec((tm, w), lambda i: (i, 0))
    return pl.pallas_call(
        _merge_kernel,
        grid=(n // tm,),
        in_specs=[row(D_MODEL), row(ya.shape[1]), row(yb.shape[1]), row(yc.shape[1]),
                  _const_spec(gmix.shape), _const_spec(wg.shape), _const_spec(pa.shape),
                  _const_spec(pb.shape), _const_spec(pc.shape), _const_spec(wo.shape)],
        out_specs=row(D_MODEL),
        out_shape=jax.ShapeDtypeStruct((n, D_MODEL), F32),
        compiler_params=_params(("parallel",)),
        name="merge",
    )(x2, ya, yb, yc, gmix, wg, pa, pb, pc, wo)


def _ffn_kernel(x_ref, g_ref, wup_ref, cw_ref, cb_ref, wdn_ref, o_ref, ubuf, act, carry, *, tiles_per_seq):
    tm = x_ref.shape[0]
    fc = FF_CHUNK

    @pl.when(pl.program_id(0) % tiles_per_seq == 0)
    def _():
        carry[...] = jnp.zeros_like(carry)

    x = x_ref[...]
    h = (x * _rms(x, D_MODEL) * g_ref[...]).astype(BF16)

    def cols(ref, rows, j):
        return jnp.concatenate([ref[rows, j * fc:(j + 1) * fc], ref[rows, D_FF + j * fc:D_FF + (j + 1) * fc]],
                               axis=1)

    def up(j):
        for half, base in enumerate((j * fc, D_FF + j * fc)):
            ubuf[j, :, half * fc:(half + 1) * fc] = jnp.dot(
                h, wup_ref[:, base:base + fc], preferred_element_type=F32)

    def conv_act(j):
        ext = jnp.concatenate([carry[j], ubuf[j]], axis=0)
        carry[j] = ubuf[j, tm - HALO:tm, :]
        y = cols(cb_ref, slice(0, 1), j)
        for tap in range(CONV_W):
            shift = CONV_W - 1 - tap
            tok = ext if shift == 0 else pltpu.roll(ext, shift, axis=0)
            y = y + cols(cw_ref, slice(tap, tap + 1), j) * tok[HALO:HALO + tm, :]
        gate = y[:, :fc]
        half = 0.5 * gate
        act[:, j * fc:(j + 1) * fc] = ((half + half * jnp.tanh(half)) * y[:, fc:]).astype(BF16)

    up(0)
    for j in range(N_FF_CHUNKS):
        if j + 1 < N_FF_CHUNKS:
            up(j + 1)
        conv_act(j)
    o_ref[...] = x + jnp.dot(act[...], wdn_ref[...], preferred_element_type=F32)


def _ffn(x2, g, wup, cw, cb, wdn, seq):
    n = x2.shape[0]
    tm = TM_FFN
    row = pl.BlockSpec((tm, D_MODEL), lambda i: (i, 0))
    return pl.pallas_call(
        functools.partial(_ffn_kernel, tiles_per_seq=seq // tm),
        grid=(n // tm,),
        in_specs=[row, _const_spec(g.shape), _const_spec(wup.shape), _const_spec(cw.shape),
                  _const_spec(cb.shape), _const_spec(wdn.shape)],
        out_specs=row,
        out_shape=jax.ShapeDtypeStruct((n, D_MODEL), F32),
        scratch_shapes=[pltpu.VMEM((N_FF_CHUNKS, tm, 2 * FF_CHUNK), F32),
                        pltpu.VMEM((tm, D_FF), BF16),
                        pltpu.VMEM((N_FF_CHUNKS, HALO, 2 * FF_CHUNK), F32)],
        compiler_params=_params(("arbitrary",)),
        name="ffn",
    )(x2, g, wup, cw, cb, wdn)


def _layer_params(l, w_in, a_q_g, a_k_g, b_q_a_g, b_kv_a_g, b_w_uq, b_w_ukv, b_qn_g, b_qr_g, b_kn_g,
                  b_kr_g, c_q_g, c_k_g, w_up, conv_w, conv_b):
    w = w_in[l]
    o_kpe = 3 * 512 + B_Q_RANK + B_KV_RANK
    o_qc = o_kpe + B_ROPE
    o_kc = o_qc + C_HEADS * C_HD
    o_vc = o_kc + C_KV_HEADS * C_HD
    o_g = o_vc + C_KV_HEADS * C_HD
    w1 = w[:, :o_kpe].astype(BF16)
    wqc = w[:, o_qc:o_kc].astype(BF16)
    wg = w[:, o_g:].astype(BF16)
    kpe_w = w[:, o_kpe:o_qc].astype(BF16)
    kc_w = w[:, o_kc:o_vc].astype(BF16)
    vc_w = w[:, o_vc:o_g].astype(BF16)
    z = lambda c: jnp.zeros((w.shape[0], c), BF16)
    r = B_ROPE // 2
    pad = LANES - B_NOPE - B_ROPE
    dup = lambda t: jnp.concatenate([t[:, :C_HD], t[:, :C_HD], t[:, C_HD:], t[:, C_HD:]], axis=1)
    ws = jnp.concatenate([
        z(B_NOPE), kpe_w, z(pad),
        z(B_NOPE), kpe_w[:, r:], kpe_w[:, :r], z(pad),
        dup(kc_w), dup(vc_w)], axis=1)

    uq = b_w_uq[l].reshape(B_Q_RANK, B_HEADS, B_NOPE + B_ROPE)
    nope, pe = uq[..., :B_NOPE], uq[..., B_NOPE:]
    zq = lambda c: jnp.zeros((B_Q_RANK, B_HEADS, c), uq.dtype)
    wuq = jnp.concatenate([
        jnp.concatenate([nope, pe, zq(pad)], axis=-1).reshape(B_Q_RANK, B_HEADS * LANES),
        jnp.concatenate([zq(B_NOPE), pe[..., r:], pe[..., :r], zq(pad)], axis=-1).reshape(B_Q_RANK, B_HEADS * LANES),
    ], axis=1).astype(BF16)

    ukv = b_w_ukv[l].reshape(B_KV_RANK, B_HEADS, B_NOPE + B_VD)
    zk = jnp.zeros((B_KV_RANK, B_HEADS, LANES - B_NOPE), ukv.dtype)
    wukv = jnp.concatenate([ukv[..., :B_NOPE], zk], axis=-1).reshape(B_KV_RANK, B_HEADS * LANES).astype(BF16)
    wvbt = ukv[..., B_NOPE:].reshape(B_KV_RANK, B_HEADS * B_VD).T.astype(BF16)
    wvat = w[:, 2 * 512:3 * 512].T.astype(BF16)

    scale_a = A_HD ** -0.5 * LOG2E
    scale_b = (B_NOPE + B_ROPE) ** -0.5 * LOG2E
    scale_c = C_HD ** -0.5
    ga = jnp.stack([jnp.tile(a_q_g[l], 2 * A_HEADS) * scale_a, jnp.tile(a_k_g[l], 2 * A_HEADS)])
    glat = jnp.concatenate([b_q_a_g[l], b_kv_a_g[l]])[None, :]
    zl = lambda c: jnp.zeros((c,), F32)
    qr, kr = b_qr_g[l], b_kr_g[l]
    gqb = jnp.stack([
        jnp.tile(jnp.concatenate([b_qn_g[l], qr, zl(pad)]), B_HEADS),
        jnp.tile(jnp.concatenate([zl(B_NOPE), qr[r:], qr[:r], zl(pad)]), B_HEADS)]) * scale_b
    gkb = jnp.stack([
        jnp.concatenate([b_kn_g[l], zl(LANES - B_NOPE)]),
        jnp.concatenate([zl(B_NOPE), kr, zl(pad)]),
        jnp.concatenate([zl(B_NOPE), kr[r:], kr[:r], zl(pad)])])
    gc = jnp.concatenate([jnp.tile(c_q_g[l], C_HEADS) * scale_c, jnp.tile(c_k_g[l], 2 * C_KV_HEADS)])[None, :]

    return dict(w1=w1, wqc=wqc, ws=ws, wg=wg, wuq=wuq, wukv=wukv, wvat=wvat, wvbt=wvbt, ga=ga, glat=glat,
                gqb=gqb, gkb=gkb, gc=gc, wup=_to_bf16(w_up, l, 256), cw=conv_w[l, :, 0, :], cb=conv_b[l][None, :])


def kernel(x, positions, rel_bias_table, ln_mix_g, w_in, a_q_g, a_k_g, a_lam_q1, a_lam_k1, a_lam_q2, a_lam_k2, a_subln_g, b_q_a_g, b_kv_a_g, b_w_uq, b_w_ukv, b_qn_g, b_qr_g, b_kn_g, b_kr_g, c_q_g, c_k_g, c_sinks, p_a, p_b, p_c, w_o, ln_ffn_g, w_up, conv_w, conv_b, w_down):
    batch, seq, d = x.shape
    n = batch * seq
    assert d == D_MODEL and seq % T_ATT == 0 and n % TM_IN == 0 and seq % TM_FFN == 0
    x2 = x.reshape(n, d)
    cos, sin = _rope_tables(positions)
    bias_a, mask_b, bias_c = _bias_tiles(rel_bias_table)
    for l in range(DEPTH):
        lambda_init = 0.8 - 0.6 * math.exp(-0.3 * l)
        p = _layer_params(l, w_in, a_q_g, a_k_g, b_q_a_g, b_kv_a_g, b_w_uq, b_w_ukv, b_qn_g, b_qr_g,
                          b_kn_g, b_kr_g, c_q_g, c_k_g, w_up, conv_w, conv_b)
        gmix = ln_mix_g[l][None, :]
        qa, ka, va, qb, kb, vb, qc, kc, vc = _in_proj(
            x2, cos, sin, gmix, p["ga"], p["glat"], p["gqb"], p["gkb"], p["gc"],
            p["w1"], p["wqc"], p["ws"], p["wuq"], p["wukv"], p["wvat"], p["wvbt"], batch, seq)
        lam = jnp.stack([a_lam_q1[l], a_lam_k1[l], a_lam_q2[l], a_lam_k2[l]])
        ya = _attn(qa, ka, va, bias_a, batch, seq, "diff", extra=(lam, a_subln_g[l][None, :]),
                   lambda_init=lambda_init)
        yb = _attn(qb, kb, vb, mask_b, batch, seq, "mla")
        yc = _swa(c_sinks[l], qc, kc, vc, bias_c, batch, seq)
        x2 = _merge(x2, ya, yb, yc, gmix, p["wg"], _to_bf16(p_a, l, 256), _to_bf16(p_b, l, 256),
                    _to_bf16(p_c, l, 256), _to_bf16(w_o, l, 256))
        x2 = _ffn(x2, ln_ffn_g[l][None, :], p["wup"], p["cw"], p["cb"], _to_bf16(w_down, l, 704), seq)
    return x2.reshape(batch, seq, d)
```

```python
import functools
import math

import jax
import jax.numpy as jnp
import numpy as np
from jax import lax
from jax.experimental import pallas as pl
from jax.experimental.pallas import tpu as pltpu

F32 = jnp.float32
BF16 = jnp.bfloat16

D_MODEL = 1024
DEPTH = 2
EPS = 1e-6
A_HEADS = 4
A_HD = 64
A_VD = 2 * A_HD
B_HEADS = 8
B_Q_RANK = 256
B_KV_RANK = 128
B_NOPE = 64
B_ROPE = 32
B_VD = 64
ROPE_THETA = 10000.0
C_HEADS = 8
C_KV_HEADS = 2
C_HD = 64
WINDOW = 128
N_BUCKETS = 32
MAX_DIST = 128
D_FF = 2816
CONV_W = 3

LANES = 128
HALF = LANES // 2
NEG = -1e30
LOG2E = math.log2(math.e)
ROW_CHUNK = 32
DEN_ROWS = 16
KIND_FAR, KIND_PREV, KIND_DIAG = 0, 1, 2
S_SLOTS = 4
P_SLOTS = 2
ITEMS_PER_TRIP = 12
VMEM_LIMIT = 56 * 1024 * 1024

T_ATT = 512
TM_IN = 1024
TM_MERGE = 1024
TM_FFN = 512
FF_CHUNK = 256
N_FF_CHUNKS = D_FF // FF_CHUNK
HALO = 8
SWA_WINDOWS = 4
ROPE_GROUP = 8

_SEG = {"qa": (0, 0, 512), "ka": (0, 512, 1024), "cq": (0, 1536, 1792),
        "ckv": (0, 1792, 1920), "qc": (1, 0, 512),
        "kpe_both": (2, 0, 256), "kc": (2, 256, 512), "vc": (2, 512, 768)}


def _params(sem, vmem=VMEM_LIMIT):
    return pltpu.CompilerParams(dimension_semantics=sem, vmem_limit_bytes=vmem)


def _const_spec(shape):
    nd = len(shape)
    return pl.BlockSpec(shape, lambda *_: (0,) * nd, pipeline_mode=pl.Buffered(1))


def _lane_lo():
    return lax.broadcasted_iota(jnp.int32, (1, LANES), 1) < HALF


def _cast_kernel(w_ref, o_ref):
    o_ref[...] = w_ref[...].astype(o_ref.dtype)


def _to_bf16(w, layer, row_block):
    _, rows, cols = w.shape
    assert rows % row_block == 0
    return pl.pallas_call(
        _cast_kernel,
        grid=(rows // row_block,),
        in_specs=[pl.BlockSpec((None, row_block, cols), lambda i: (layer, i, 0))],
        out_specs=pl.BlockSpec((row_block, cols), lambda i: (i, 0)),
        out_shape=jax.ShapeDtypeStruct((rows, cols), BF16),
        compiler_params=_params(("parallel",)),
        name="to_bf16",
    )(w)


def _rope_kernel(pos_ref, inv_ref, sign_ref, cos_ref, sin_ref):
    groups = pos_ref.shape[0]
    inv = inv_ref[...]
    sign = sign_ref[...]
    base = pos_ref[...].astype(F32) * inv
    cos_b, sin_b = jnp.cos(base), jnp.sin(base)
    step = lax.broadcasted_iota(jnp.int32, (ROPE_GROUP, LANES), 0).astype(F32) * inv
    cos_s, sin_s = jnp.cos(step), jnp.sin(step)
    for i in range(groups):
        rows = slice(i * ROPE_GROUP, (i + 1) * ROPE_GROUP)
        cb, sb = cos_b[i:i + 1, :], sin_b[i:i + 1, :]
        cos_ref[rows, :] = cb * cos_s - sb * sin_s
        sin_ref[rows, :] = (sb * cos_s + cb * sin_s) * sign


def _rope_tables(positions):
    batch, seq = positions.shape
    n = batch * seq
    inv = 1.0 / (ROPE_THETA ** (jnp.arange(0, B_ROPE, 2, dtype=F32) / B_ROPE))
    z = jnp.zeros((B_NOPE,), F32)
    zp = jnp.zeros((LANES - B_NOPE - B_ROPE,), F32)
    inv_pat = jnp.concatenate([z, inv, inv, zp])[None, :]
    ones = jnp.ones((B_ROPE // 2,), F32)
    sign_pat = jnp.concatenate([z, -ones, ones, zp])[None, :]
    tm = 1024
    assert seq % tm == 0 and tm % ROPE_GROUP == 0
    pos_groups = positions.reshape(n // ROPE_GROUP, ROPE_GROUP)[:, :1]
    return pl.pallas_call(
        _rope_kernel,
        grid=(n // tm,),
        in_specs=[pl.BlockSpec((tm // ROPE_GROUP, 1), lambda i: (i, 0)),
                  pl.BlockSpec((1, LANES), lambda i: (0, 0)),
                  pl.BlockSpec((1, LANES), lambda i: (0, 0))],
        out_specs=[pl.BlockSpec((tm, LANES), lambda i: (i, 0))] * 2,
        out_shape=[jax.ShapeDtypeStruct((n, LANES), F32)] * 2,
        compiler_params=_params(("parallel",)),
        name="rope_tables",
    )(pos_groups, inv_pat, sign_pat)


def _bucket(rel):
    n = jnp.maximum(rel, 0)
    max_exact = N_BUCKETS // 2
    nf = jnp.maximum(n, 1).astype(F32)
    large = max_exact + (jnp.log(nf / max_exact) / math.log(MAX_DIST / max_exact)
                         * (N_BUCKETS - max_exact)).astype(jnp.int32)
    large = jnp.minimum(large, N_BUCKETS - 1)
    return jnp.where(n < max_exact, n, large)


def _lookup(tab_ref, bucket, col):
    out = jnp.zeros(bucket.shape, F32)
    for k in range(N_BUCKETS):
        out = jnp.where(bucket == k, tab_ref[k, col], out)
    return out


def _bias_a_kernel(tab_ref, out_ref):
    h = pl.program_id(0)
    t = out_ref.shape[-1]
    key = lax.broadcasted_iota(jnp.int32, (t, t), 0)
    qry = lax.broadcasted_iota(jnp.int32, (t, t), 1)
    far = tab_ref[N_BUCKETS - 1, h]
    rel = qry - key
    out_ref[0, KIND_FAR] = jnp.zeros((t, t), F32)
    out_ref[0, KIND_PREV] = (_lookup(tab_ref, _bucket(rel + t), h) - far) * LOG2E
    out_ref[0, KIND_DIAG] = jnp.where(rel >= 0, (_lookup(tab_ref, _bucket(rel), h) - far) * LOG2E, NEG)


def _mask_kernel(out_ref):
    t = out_ref.shape[-1]
    key = lax.broadcasted_iota(jnp.int32, (t, t), 0)
    qry = lax.broadcasted_iota(jnp.int32, (t, t), 1)
    out_ref[0, KIND_FAR] = jnp.zeros((t, t), F32)
    out_ref[0, KIND_PREV] = jnp.zeros((t, t), F32)
    out_ref[0, KIND_DIAG] = jnp.where(key <= qry, 0.0, NEG)


def _bias_c_kernel(tab_ref, out_ref):
    h = pl.program_id(0)
    row = lax.broadcasted_iota(jnp.int32, (WINDOW, 2 * WINDOW), 0)
    col = lax.broadcasted_iota(jnp.int32, (WINDOW, 2 * WINDOW), 1)
    rel = row + WINDOW - col
    valid = (rel >= 0) & (rel < WINDOW)
    out_ref[0] = jnp.where(valid, _lookup(tab_ref, _bucket(rel), h + A_HEADS), NEG)


def _bias_tiles(table):
    smem = pl.BlockSpec(memory_space=pltpu.SMEM)
    bias_a = pl.pallas_call(
        _bias_a_kernel,
        grid=(A_HEADS,),
        in_specs=[smem],
        out_specs=pl.BlockSpec((1, 3, T_ATT, T_ATT), lambda h: (h, 0, 0, 0)),
        out_shape=jax.ShapeDtypeStruct((A_HEADS, 3, T_ATT, T_ATT), F32),
        compiler_params=_params(("parallel",)),
        name="bias_a",
    )(table)
    mask_b = pl.pallas_call(
        _mask_kernel,
        out_shape=jax.ShapeDtypeStruct((1, 3, T_ATT, T_ATT), F32),
        compiler_params=pltpu.CompilerParams(vmem_limit_bytes=VMEM_LIMIT),
        name="mask_b",
    )()
    bias_c = pl.pallas_call(
        _bias_c_kernel,
        grid=(C_HEADS,),
        in_specs=[smem],
        out_specs=pl.BlockSpec((1, WINDOW, 2 * WINDOW), lambda h: (h, 0, 0)),
        out_shape=jax.ShapeDtypeStruct((C_HEADS, WINDOW, 2 * WINDOW), F32),
        compiler_params=_params(("parallel",)),
        name="bias_c",
    )(table)
    return bias_a, mask_b, bias_c


def _rms(t, width):
    return lax.rsqrt(jnp.sum(t * t, axis=-1, keepdims=True) / width + EPS)


def _in_kernel(x_ref, cos_ref, sin_ref, gmix_ref, ga_ref, glat_ref, gqb_ref, gkb_ref, gc_ref,
               w1_ref, wqc_ref, ws_ref, wuq_ref, wukv_ref, wvat_ref, wvbt_ref,
               qa_ref, ka_ref, va_ref, qb_ref, kb_ref, vb_ref, qc_ref, kc_ref, vc_ref):
    x = x_ref[...]
    h = (x * _rms(x, D_MODEL) * gmix_ref[...]).astype(BF16)
    lo = _lane_lo()
    w_refs = (w1_ref, wqc_ref, ws_ref)

    def proj(name):
        which, a, b = _SEG[name]
        return jnp.dot(h, w_refs[which][:, a:b], preferred_element_type=F32)

    def norm_halves(t, g, out_ref):
        for j in range(t.shape[1] // LANES):
            sl = slice(j * LANES, (j + 1) * LANES)
            tj = t[:, sl]
            sq = tj * tj
            s_lo = jnp.sum(jnp.where(lo, sq, 0.0), axis=-1, keepdims=True)
            s_hi = jnp.sum(jnp.where(lo, 0.0, sq), axis=-1, keepdims=True)
            r = jnp.where(lo, lax.rsqrt(s_lo / HALF + EPS), lax.rsqrt(s_hi / HALF + EPS))
            out_ref[:, sl] = (tj * r * g[:, sl]).astype(out_ref.dtype)

    glat = glat_ref[...]
    cq = proj("cq")
    ckv = proj("ckv")
    kpe_both = proj("kpe_both")
    kpe = kpe_both[:, :LANES]
    kpe_sw = kpe_both[:, LANES:]
    cqn = (cq * _rms(cq, B_Q_RANK) * glat[:, :B_Q_RANK]).astype(BF16)
    ckvn = (ckv * _rms(ckv, B_KV_RANK) * glat[:, B_Q_RANK:]).astype(BF16)
    uq = jnp.dot(cqn, wuq_ref[...], preferred_element_type=F32)
    ukv = jnp.dot(ckvn, wukv_ref[...], preferred_element_type=F32)
    qa = proj("qa")
    ka = proj("ka")
    qc = proj("qc")
    kc = proj("kc")
    vc = proj("vc")
    nt = (((1,), (1,)), ((), ()))
    va_ref[...] = lax.dot_general(wvat_ref[...], h, nt, preferred_element_type=F32).astype(BF16)
    vb_ref[...] = lax.dot_general(wvbt_ref[...], ckvn, nt, preferred_element_type=F32).astype(BF16)

    cos = cos_ref[...]
    sin = sin_ref[...]
    hw = B_HEADS * LANES
    for j in range(B_HEADS):
        sl = slice(j * LANES, (j + 1) * LANES)
        raw = uq[:, sl]
        raw_sw = uq[:, hw + j * LANES: hw + (j + 1) * LANES]
        sq = raw * raw
        s_n = jnp.sum(jnp.where(lo, sq, 0.0), axis=-1, keepdims=True)
        s_r = jnp.sum(jnp.where(lo, 0.0, sq), axis=-1, keepdims=True)
        r = jnp.where(lo, lax.rsqrt(s_n / B_NOPE + EPS), lax.rsqrt(s_r / B_ROPE + EPS))
        out = r * (raw * gqb_ref[0:1, sl] * cos + raw_sw * gqb_ref[1:2, sl] * sin)
        qb_ref[:, sl] = out.astype(BF16)

    kpe_out = _rms(kpe, B_ROPE) * (kpe * gkb_ref[1:2, :] * cos + kpe_sw * gkb_ref[2:3, :] * sin)
    for j in range(B_HEADS):
        sl = slice(j * LANES, (j + 1) * LANES)
        raw = ukv[:, sl]
        kb_ref[:, sl] = (raw * _rms(raw, B_NOPE) * gkb_ref[0:1, :] + kpe_out).astype(BF16)

    norm_halves(qa, ga_ref[0:1, :], qa_ref)
    norm_halves(ka, ga_ref[1:2, :], ka_ref)
    gc = gc_ref[...]
    norm_halves(qc, gc[:, :C_HEADS * C_HD], qc_ref)
    norm_halves(kc, gc[:, C_HEADS * C_HD:], kc_ref)
    vc_ref[...] = vc.astype(BF16)


def _in_proj(x2, cos, sin, gmix, ga, glat, gqb, gkb, gc, w1, wqc, ws, wuq, wukv, wvat, wvbt, batch, seq):
    n = x2.shape[0]
    tm = TM_IN
    nps = seq // tm
    row = lambda w: pl.BlockSpec((tm, w), lambda i: (i, 0))
    outs = (("qa", 512), ("ka", 512), ("va", None), ("qb", 1024), ("kb", 1024), ("vb", None),
            ("qc", 512), ("kc", 256), ("vc", 256))
    vt_spec = pl.BlockSpec((None, 512, tm), lambda i: (i // nps, 0, i % nps))
    vt_shape = jax.ShapeDtypeStruct((batch, 512, seq), BF16)
    return pl.pallas_call(
        _in_kernel,
        grid=(n // tm,),
        in_specs=[row(D_MODEL), row(LANES), row(LANES),
                  _const_spec(gmix.shape), _const_spec(ga.shape), _const_spec(glat.shape),
                  _const_spec(gqb.shape), _const_spec(gkb.shape), _const_spec(gc.shape),
                  _const_spec(w1.shape), _const_spec(wqc.shape), _const_spec(ws.shape),
                  _const_spec(wuq.shape), _const_spec(wukv.shape),
                  _const_spec(wvat.shape), _const_spec(wvbt.shape)],
        out_specs=[vt_spec if w is None else row(w) for _, w in outs],
        out_shape=[vt_shape if w is None else jax.ShapeDtypeStruct((n, w), BF16) for _, w in outs],
        compiler_params=_params(("parallel",)),
        name="in_proj",
    )(x2, cos, sin, gmix, ga, glat, gqb, gkb, gc, w1, wqc, ws, wuq, wukv, wvat, wvbt)


def _attn_items(nq):
    items = [(qi, ki, KIND_DIAG if ki == qi else KIND_PREV if ki == qi - 1 else KIND_FAR)
             for qi in range(nq) for ki in range(qi + 1)]
    items += [items[-1]] * 2
    return np.asarray(items, np.int32).T


def _attn_kernel(*refs, mode, lambda_init, n_items, nq):
    if mode == "diff":
        (tab_ref, q_ref, k_ref, vt_ref, bias_ref, lam_ref, subg_ref, o_ref,
         s_scr, p_scr, acc_scr, m_scr, mt_scr, al_scr) = refs
    else:
        tab_ref, q_ref, k_ref, vt_ref, bias_ref, o_ref, s_scr, p_scr, acc_scr, m_scr, mt_scr, al_scr = refs
    t = T_ATT
    dv = acc_scr.shape[2] - DEN_ROWS
    lo = _lane_lo()
    m_scr[...] = jnp.full(m_scr.shape, NEG, F32)
    acc_scr[...] = jnp.zeros_like(acc_scr)
    if mode == "diff":
        lv = lam_ref[...]
        lam = (jnp.exp(jnp.sum(lv[0:1] * lv[1:2], axis=-1, keepdims=True))
               - jnp.exp(jnp.sum(lv[2:3] * lv[3:4], axis=-1, keepdims=True)) + lambda_init)
        out_gain = subg_ref[...] * (1.0 - lambda_init)

    def rows(idx):
        return pl.ds(pl.multiple_of(idx * t, t), t)

    def scores(n, slot):
        q = q_ref[rows(tab_ref[0, n]), :]
        k = k_ref[rows(tab_ref[1, n]), :]
        kind = tab_ref[2, n]
        for j in range(2):
            if mode == "diff":
                zero = jnp.zeros_like(q)
                qj = jnp.where(lo, q, zero) if j == 0 else jnp.where(lo, zero, q)
                kj = k
            else:
                qj = q[:, j * LANES:(j + 1) * LANES]
                kj = k[:, j * LANES:(j + 1) * LANES]
            s = lax.dot_general(kj, qj, (((1,), (1,)), ((), ())), preferred_element_type=F32)
            s = s + bias_ref[kind]
            s_scr[2 * slot + j] = s
            mt_scr[2 * slot + j] = jnp.max(s, axis=0, keepdims=True)

    def softmax(n, slot, pslot):
        restart = jnp.where(tab_ref[1, n] == 0, NEG, 0.0)
        for j in range(2):
            c = 2 * slot + j
            pc = 2 * pslot + j
            m_old = m_scr[j] + restart
            m_new = jnp.maximum(m_old, mt_scr[c])
            al_scr[pc] = jnp.exp2(m_old - m_new)
            m_scr[j] = m_new
            for r0 in range(0, t, ROW_CHUNK):
                d = s_scr[c, r0:r0 + ROW_CHUNK, :] - m_new
                p_scr[pc, r0:r0 + ROW_CHUNK, :] = jnp.exp2(d.astype(BF16))

    def values(n, pslot):
        qi = tab_ref[0, n]
        v = vt_ref[:, rows(tab_ref[1, n])]
        ones = jnp.ones((DEN_ROWS, t), BF16)
        for j in range(2):
            pc = 2 * pslot + j
            vt = jnp.concatenate([v if mode == "diff" else v[j * dv:(j + 1) * dv, :], ones], axis=0)
            acc_scr[qi, j] = al_scr[pc] * acc_scr[qi, j] + jnp.dot(vt, p_scr[pc], preferred_element_type=F32)

    def finalize(qi):
        outs = [acc_scr[qi, j, :dv, :] / acc_scr[qi, j, dv:dv + 1, :] for j in range(2)]
        if mode == "diff":
            o = outs[0] - lam * outs[1]
            o = o * lax.rsqrt(jnp.sum(o * o, axis=0, keepdims=True) / A_VD + EPS)
            o = o.T * out_gain
        else:
            o = jnp.concatenate(outs, axis=0).T
        o_ref[qi * t:(qi + 1) * t, :] = o.astype(o_ref.dtype)

    scores(0, 0)
    scores(1, 1)

    def body(i, carry):
        for u in range(ITEMS_PER_TRIP):
            n = ITEMS_PER_TRIP * i + u
            s = u % S_SLOTS
            scores(n + 2, (s + 2) % S_SLOTS)
            softmax(n, s, s % P_SLOTS)
            if u > 0:
                values(n - 1, (s + 1) % P_SLOTS)
        values(ITEMS_PER_TRIP * i + ITEMS_PER_TRIP - 1, (ITEMS_PER_TRIP - 1) % P_SLOTS)
        return carry

    lax.fori_loop(0, n_items // ITEMS_PER_TRIP, body, 0)
    for qi in range(nq):
        finalize(qi)


def _attn(q, k, vt, bias, batch, seq, mode, extra=(), lambda_init=0.0):
    n = q.shape[0]
    t = T_ATT
    nq = seq // t
    tab = _attn_items(nq)
    n_items = tab.shape[1] - 2
    assert n_items % ITEMS_PER_TRIP == 0 and ITEMS_PER_TRIP % S_SLOTS == 0 and S_SLOTS % P_SLOTS == 0
    qw = LANES if mode == "diff" else 2 * LANES
    groups = q.shape[1] // qw
    per_head = bias.shape[0] > 1
    in_specs = [pl.BlockSpec(memory_space=pltpu.SMEM),
                pl.BlockSpec((seq, qw), lambda b, g: (b, g)),
                pl.BlockSpec((seq, qw), lambda b, g: (b, g)),
                pl.BlockSpec((None, LANES, seq), lambda b, g: (b, g, 0)),
                pl.BlockSpec((None, 3, t, t), lambda b, g: (g if per_head else 0, 0, 0, 0))]
    in_specs += [pl.BlockSpec(e.shape, lambda b, g: (0, 0)) for e in extra]
    return pl.pallas_call(
        functools.partial(_attn_kernel, mode=mode, lambda_init=lambda_init, n_items=n_items, nq=nq),
        grid=(batch, groups),
        in_specs=in_specs,
        out_specs=pl.BlockSpec((seq, LANES), lambda b, g: (b, g)),
        out_shape=jax.ShapeDtypeStruct((n, groups * LANES), BF16),
        scratch_shapes=[pltpu.VMEM((2 * S_SLOTS, t, t), F32), pltpu.VMEM((2 * P_SLOTS, t, t), BF16),
                        pltpu.VMEM((nq, 2, (A_VD if mode == "diff" else B_VD) + DEN_ROWS, t), F32), pltpu.VMEM((2, 1, t), F32),
                        pltpu.VMEM((2 * S_SLOTS, 1, t), F32), pltpu.VMEM((2 * P_SLOTS, 1, t), F32)],
        compiler_params=_params(("parallel", "parallel")),
        name="attn_" + mode,
    )(jnp.asarray(tab), q, k, vt, bias, *extra)


def _swa_kernel(sink_ref, q_ref, kp_ref, kc_ref, vp_ref, vc_ref, bias_ref, o_ref):
    first_step = pl.program_id(1) == 0
    lo = _lane_lo()
    col = lax.broadcasted_iota(jnp.int32, (WINDOW, 2 * WINDOW), 1)
    keep = (col >= WINDOW) | jnp.logical_not(first_step)
    grp = C_HEADS // C_KV_HEADS
    k_all = jnp.concatenate([kp_ref[...], kc_ref[...]], axis=0)
    v_all = jnp.concatenate([vp_ref[...], vc_ref[...]], axis=0)
    work = [(w, head) for w in range(SWA_WINDOWS) for head in range(C_HEADS)]
    scores = []
    for w, head in work:
        hp, j = divmod(head, 2)
        ksl = slice((head // grp) * LANES, (head // grp + 1) * LANES)
        q = q_ref[w * WINDOW:(w + 1) * WINDOW, hp * LANES:(hp + 1) * LANES]
        zero = jnp.zeros_like(q)
        qj = jnp.where(lo, q, zero) if j == 0 else jnp.where(lo, zero, q)
        k = k_all[w * WINDOW:(w + 2) * WINDOW, ksl]
        scores.append(lax.dot_general(qj, k, (((1,), (1,)), ((), ())), preferred_element_type=F32))
    probs = []
    for (w, head), s in zip(work, scores):
        s = s + bias_ref[head]
        if w == 0:
            s = jnp.where(keep, s, NEG)
        sink = sink_ref[head]
        m = jnp.maximum(jnp.max(s, axis=-1, keepdims=True), sink)
        p = jnp.exp(s - m)
        den = jnp.sum(p, axis=-1, keepdims=True) + jnp.exp(sink - m)
        probs.append((p.astype(BF16), den))
    for w in range(SWA_WINDOWS):
        for hp in range(C_HEADS // 2):
            ksl = slice(((2 * hp) // grp) * LANES, ((2 * hp) // grp + 1) * LANES)
            v = v_all[w * WINDOW:(w + 2) * WINDOW, ksl]
            pair = probs[w * C_HEADS + 2 * hp:w * C_HEADS + 2 * hp + 2]
            outs = [jnp.dot(p, v, preferred_element_type=F32) / den for p, den in pair]
            o_ref[w * WINDOW:(w + 1) * WINDOW, hp * LANES:(hp + 1) * LANES] = (
                jnp.where(lo, outs[0], outs[1]).astype(o_ref.dtype))


def _swa(sinks, q, k, v, bias, batch, seq):
    n = q.shape[0]
    rows = SWA_WINDOWS * WINDOW
    steps = seq // rows
    cur = lambda b, i: (b * steps + i, 0)
    prev = lambda b, i: ((b * steps + i) * SWA_WINDOWS - jnp.minimum(i, 1), 0)
    kw = k.shape[1]
    return pl.pallas_call(
        _swa_kernel,
        grid=(batch, steps),
        in_specs=[pl.BlockSpec(memory_space=pltpu.SMEM),
                  pl.BlockSpec((rows, q.shape[1]), cur),
                  pl.BlockSpec((WINDOW, kw), prev), pl.BlockSpec((rows, kw), cur),
                  pl.BlockSpec((WINDOW, kw), prev), pl.BlockSpec((rows, kw), cur),
                  pl.BlockSpec(bias.shape, lambda b, i: (0, 0, 0))],
        out_specs=pl.BlockSpec((rows, q.shape[1]), cur),
        out_shape=jax.ShapeDtypeStruct((n, q.shape[1]), BF16),
        compiler_params=_params(("parallel", "arbitrary")),
        name="swa",
    )(sinks, q, k, k, v, v, bias)


def _merge_kernel(x_ref, ya_ref, yb_ref, yc_ref, gmix_ref, wg_ref, pa_ref, pb_ref, pc_ref, wo_ref, o_ref):
    x = x_ref[...]
    h = (x * _rms(x, D_MODEL) * gmix_ref[...]).astype(BF16)
    merged = None
    for j, (y_ref, p_ref) in enumerate(((ya_ref, pa_ref), (yb_ref, pb_ref), (yc_ref, pc_ref))):
        gate = jnp.dot(h, wg_ref[:, j * D_MODEL:(j + 1) * D_MODEL], preferred_element_type=F32)
        term = jax.nn.sigmoid(gate) * jnp.dot(y_ref[...], p_ref[...], preferred_element_type=F32)
        merged = term if merged is None else merged + term
    o_ref[...] = x + jnp.dot(merged.astype(BF16), wo_ref[...], preferred_element_type=F32)


def _merge(x2, ya, yb, yc, gmix, wg, pa, pb, pc, wo):
    n = x2.shape[0]
    tm = TM_MERGE
    row = lambda w: pl.BlockSpec((tm, w), lambda i: (i, 0))
    return pl.pallas_call(
        _merge_kernel,
        grid=(n // tm,),
        in_specs=[row(D_MODEL), row(ya.shape[1]), row(yb.shape[1]), row(yc.shape[1]),
                  _const_spec(gmix.shape), _const_spec(wg.shape), _const_spec(pa.shape),
                  _const_spec(pb.shape), _const_spec(pc.shape), _const_spec(wo.shape)],
        out_specs=row(D_MODEL),
        out_shape=jax.ShapeDtypeStruct((n, D_MODEL), F32),
        compiler_params=_params(("parallel",)),
        name="merge",
    )(x2, ya, yb, yc, gmix, wg, pa, pb, pc, wo)


def _ffn_kernel(x_ref, g_ref, wup_ref, cw_ref, cb_ref, wdn_ref, o_ref, ubuf, act, carry, *, tiles_per_seq):
    tm = x_ref.shape[0]
    fc = FF_CHUNK

    @pl.when(pl.program_id(0) % tiles_per_seq == 0)
    def _():
        carry[...] = jnp.zeros_like(carry)

    x = x_ref[...]
    h = (x * _rms(x, D_MODEL) * g_ref[...]).astype(BF16)

    def cols(ref, rows, j):
        return jnp.concatenate([ref[rows, j * fc:(j + 1) * fc], ref[rows, D_FF + j * fc:D_FF + (j + 1) * fc]],
                               axis=1)

    def up(j):
        for half, base in enumerate((j * fc, D_FF + j * fc)):
            ubuf[j, :, half * fc:(half + 1) * fc] = jnp.dot(
                h, wup_ref[:, base:base + fc], preferred_element_type=F32)

    def conv_act(j):
        ext = jnp.concatenate([carry[j], ubuf[j]], axis=0)
        carry[j] = ubuf[j, tm - HALO:tm, :]
        y = cols(cb_ref, slice(0, 1), j)
        for tap in range(CONV_W):
            shift = CONV_W - 1 - tap
            tok = ext if shift == 0 else pltpu.roll(ext, shift, axis=0)
            y = y + cols(cw_ref, slice(tap, tap + 1), j) * tok[HALO:HALO + tm, :]
        gate = y[:, :fc]
        half = 0.5 * gate
        act[:, j * fc:(j + 1) * fc] = ((half + half * jnp.tanh(half)) * y[:, fc:]).astype(BF16)

    up(0)
    for j in range(N_FF_CHUNKS):
        if j + 1 < N_FF_CHUNKS:
            up(j + 1)
        conv_act(j)
    o_ref[...] = x + jnp.dot(act[...], wdn_ref[...], preferred_element_type=F32)


def _ffn(x2, g, wup, cw, cb, wdn, seq):
    n = x2.shape[0]
    tm = TM_FFN
    row = pl.BlockSpec((tm, D_MODEL), lambda i: (i, 0))
    return pl.pallas_call(
        functools.partial(_ffn_kernel, tiles_per_seq=seq // tm),
        grid=(n // tm,),
        in_specs=[row, _const_spec(g.shape), _const_spec(wup.shape), _const_spec(cw.shape),
                  _const_spec(cb.shape), _const_spec(wdn.shape)],
        out_specs=row,
        out_shape=jax.ShapeDtypeStruct((n, D_MODEL), F32),
        scratch_shapes=[pltpu.VMEM((N_FF_CHUNKS, tm, 2 * FF_CHUNK), F32),
                        pltpu.VMEM((tm, D_FF), BF16),
                        pltpu.VMEM((N_FF_CHUNKS, HALO, 2 * FF_CHUNK), F32)],
        compiler_params=_params(("arbitrary",)),
        name="ffn",
    )(x2, g, wup, cw, cb, wdn)


def _layer_params(l, w_in, a_q_g, a_k_g, b_q_a_g, b_kv_a_g, b_w_uq, b_w_ukv, b_qn_g, b_qr_g, b_kn_g,
                  b_kr_g, c_q_g, c_k_g, w_up, conv_w, conv_b):
    w = w_in[l]
    o_kpe = 3 * 512 + B_Q_RANK + B_KV_RANK
    o_qc = o_kpe + B_ROPE
    o_kc = o_qc + C_HEADS * C_HD
    o_vc = o_kc + C_KV_HEADS * C_HD
    o_g = o_vc + C_KV_HEADS * C_HD
    w1 = w[:, :o_kpe].astype(BF16)
    wqc = w[:, o_qc:o_kc].astype(BF16)
    wg = w[:, o_g:].astype(BF16)
    kpe_w = w[:, o_kpe:o_qc].astype(BF16)
    kc_w = w[:, o_kc:o_vc].astype(BF16)
    vc_w = w[:, o_vc:o_g].astype(BF16)
    z = lambda c: jnp.zeros((w.shape[0], c), BF16)
    r = B_ROPE // 2
    pad = LANES - B_NOPE - B_ROPE
    dup = lambda t: jnp.concatenate([t[:, :C_HD], t[:, :C_HD], t[:, C_HD:], t[:, C_HD:]], axis=1)
    ws = jnp.concatenate([
        z(B_NOPE), kpe_w, z(pad),
        z(B_NOPE), kpe_w[:, r:], kpe_w[:, :r], z(pad),
        dup(kc_w), dup(vc_w)], axis=1)

    uq = b_w_uq[l].reshape(B_Q_RANK, B_HEADS, B_NOPE + B_ROPE)
    nope, pe = uq[..., :B_NOPE], uq[..., B_NOPE:]
    zq = lambda c: jnp.zeros((B_Q_RANK, B_HEADS, c), uq.dtype)
    wuq = jnp.concatenate([
        jnp.concatenate([nope, pe, zq(pad)], axis=-1).reshape(B_Q_RANK, B_HEADS * LANES),
        jnp.concatenate([zq(B_NOPE), pe[..., r:], pe[..., :r], zq(pad)], axis=-1).reshape(B_Q_RANK, B_HEADS * LANES),
    ], axis=1).astype(BF16)

    ukv = b_w_ukv[l].reshape(B_KV_RANK, B_HEADS, B_NOPE + B_VD)
    zk = jnp.zeros((B_KV_RANK, B_HEADS, LANES - B_NOPE), ukv.dtype)
    wukv = jnp.concatenate([ukv[..., :B_NOPE], zk], axis=-1).reshape(B_KV_RANK, B_HEADS * LANES).astype(BF16)
    wvbt = ukv[..., B_NOPE:].reshape(B_KV_RANK, B_HEADS * B_VD).T.astype(BF16)
    wvat = w[:, 2 * 512:3 * 512].T.astype(BF16)

    scale_a = A_HD ** -0.5 * LOG2E
    scale_b = (B_NOPE + B_ROPE) ** -0.5 * LOG2E
    scale_c = C_HD ** -0.5
    ga = jnp.stack([jnp.tile(a_q_g[l], 2 * A_HEADS) * scale_a, jnp.tile(a_k_g[l], 2 * A_HEADS)])
    glat = jnp.concatenate([b_q_a_g[l], b_kv_a_g[l]])[None, :]
    zl = lambda c: jnp.zeros((c,), F32)
    qr, kr = b_qr_g[l], b_kr_g[l]
    gqb = jnp.stack([
        jnp.tile(jnp.concatenate([b_qn_g[l], qr, zl(pad)]), B_HEADS),
        jnp.tile(jnp.concatenate([zl(B_NOPE), qr[r:], qr[:r], zl(pad)]), B_HEADS)]) * scale_b
    gkb = jnp.stack([
        jnp.concatenate([b_kn_g[l], zl(LANES - B_NOPE)]),
        jnp.concatenate([zl(B_NOPE), kr, zl(pad)]),
        jnp.concatenate([zl(B_NOPE), kr[r:], kr[:r], zl(pad)])])
    gc = jnp.concatenate([jnp.tile(c_q_g[l], C_HEADS) * scale_c, jnp.tile(c_k_g[l], 2 * C_KV_HEADS)])[None, :]

    return dict(w1=w1, wqc=wqc, ws=ws, wg=wg, wuq=wuq, wukv=wukv, wvat=wvat, wvbt=wvbt, ga=ga, glat=glat,
                gqb=gqb, gkb=gkb, gc=gc, wup=_to_bf16(w_up, l, 256), cw=conv_w[l, :, 0, :], cb=conv_b[l][None, :])


def kernel(x, positions, rel_bias_table, ln_mix_g, w_in, a_q_g, a_k_g, a_lam_q1, a_lam_k1, a_lam_q2, a_lam_k2, a_subln_g, b_q_a_g, b_kv_a_g, b_w_uq, b_w_ukv, b_qn_g, b_qr_g, b_kn_g, b_kr_g, c_q_g, c_k_g, c_sinks, p_a, p_b, p_c, w_o, ln_ffn_g, w_up, conv_w, conv_b, w_down):
    batch, seq, d = x.shape
    n = batch * seq
    assert d == D_MODEL and seq % T_ATT == 0 and n % TM_IN == 0 and seq % TM_FFN == 0
    x2 = x.reshape(n, d)
    cos, sin = _rope_tables(positions)
    bias_a, mask_b, bias_c = _bias_tiles(rel_bias_table)
    for l in range(DEPTH):
        lambda_init = 0.8 - 0.6 * math.exp(-0.3 * l)
        p = _layer_params(l, w_in, a_q_g, a_k_g, b_q_a_g, b_kv_a_g, b_w_uq, b_w_ukv, b_qn_g, b_qr_g,
                          b_kn_g, b_kr_g, c_q_g, c_k_g, w_up, conv_w, conv_b)
        gmix = ln_mix_g[l][None, :]
        qa, ka, va, qb, kb, vb, qc, kc, vc = _in_proj(
            x2, cos, sin, gmix, p["ga"], p["glat"], p["gqb"], p["gkb"], p["gc"],
            p["w1"], p["wqc"], p["ws"], p["wuq"], p["wukv"], p["wvat"], p["wvbt"], batch, seq)
        lam = jnp.stack([a_lam_q1[l], a_lam_k1[l], a_lam_q2[l], a_lam_k2[l]])
        ya = _attn(qa, ka, va, bias_a, batch, seq, "diff", extra=(lam, a_subln_g[l][None, :]),
                   lambda_init=lambda_init)
        yb = _attn(qb, kb, vb, mask_b, batch, seq, "mla")
        yc = _swa(c_sinks[l], qc, kc, vc, bias_c, batch, seq)
        x2 = _merge(x2, ya, yb, yc, gmix, p["wg"], _to_bf16(p_a, l, 256), _to_bf16(p_b, l, 256),
                    _to_bf16(p_c, l, 256), _to_bf16(w_o, l, 256))
        x2 = _ffn(x2, ln_ffn_g[l][None, :], p["wup"], p["cw"], p["cb"], _to_bf16(w_down, l, 704), seq)
    return x2.reshape(batch, seq, d)
```

```python
import functools
import math

import jax
import jax.numpy as jnp
import numpy as np
from jax import lax
from jax.experimental import pallas as pl
from jax.experimental.pallas import tpu as pltpu

F32 = jnp.float32
BF16 = jnp.bfloat16

D_MODEL = 1024
DEPTH = 2
EPS = 1e-6
A_HEADS = 4
A_HD = 64
A_VD = 2 * A_HD
B_HEADS = 8
B_Q_RANK = 256
B_KV_RANK = 128
B_NOPE = 64
B_ROPE = 32
B_VD = 64
ROPE_THETA = 10000.0
C_HEADS = 8
C_KV_HEADS = 2
C_HD = 64
WINDOW = 128
N_BUCKETS = 32
MAX_DIST = 128
D_FF = 2816
CONV_W = 3

LANES = 128
HALF = LANES // 2
NEG = -1e30
LOG2E = math.log2(math.e)
ROW_CHUNK = 32
DEN_ROWS = 16
KIND_FAR, KIND_PREV, KIND_DIAG = 0, 1, 2
S_SLOTS = 4
P_SLOTS = 2
ITEMS_PER_TRIP = 12
VMEM_LIMIT = 56 * 1024 * 1024

T_ATT = 512
TM_IN = 1024
TM_MERGE = 1024
TM_FFN = 512
FF_CHUNK = 256
N_FF_CHUNKS = D_FF // FF_CHUNK
HALO = 8
SWA_WINDOWS = 4
ROPE_GROUP = 8

_SEG = {"qa": (0, 0, 512), "ka": (0, 512, 1024), "cq": (0, 1536, 1792),
        "ckv": (0, 1792, 1920), "qc": (1, 0, 512),
        "kpe_both": (2, 0, 256), "kc": (2, 256, 512), "vc": (2, 512, 768)}


def _params(sem, vmem=VMEM_LIMIT):
    return pltpu.CompilerParams(dimension_semantics=sem, vmem_limit_bytes=vmem)


def _const_spec(shape):
    nd = len(shape)
    return pl.BlockSpec(shape, lambda *_: (0,) * nd, pipeline_mode=pl.Buffered(1))


def _lane_lo():
    return lax.broadcasted_iota(jnp.int32, (1, LANES), 1) < HALF


def _cast_kernel(w_ref, o_ref):
    o_ref[...] = w_ref[...].astype(o_ref.dtype)


def _to_bf16(w, row_block):
    depth, rows, cols = w.shape
    assert rows % row_block == 0
    spec = pl.BlockSpec((None, row_block, cols), lambda l, i: (l, i, 0))
    return pl.pallas_call(
        _cast_kernel,
        grid=(depth, rows // row_block),
        in_specs=[spec],
        out_specs=spec,
        out_shape=jax.ShapeDtypeStruct(w.shape, BF16),
        compiler_params=_params(("parallel", "parallel")),
        name="to_bf16",
    )(w)


def _layer_spec(stack, layer):
    _, rows, cols = stack.shape
    return pl.BlockSpec((None, rows, cols), lambda *_: (layer, 0, 0), pipeline_mode=pl.Buffered(1))


def _rope_kernel(pos_ref, inv_ref, sign_ref, cos_ref, sin_ref):
    groups = pos_ref.shape[0]
    inv = inv_ref[...]
    sign = sign_ref[...]
    base = pos_ref[...].astype(F32) * inv
    cos_b, sin_b = jnp.cos(base), jnp.sin(base)
    step = lax.broadcasted_iota(jnp.int32, (ROPE_GROUP, LANES), 0).astype(F32) * inv
    cos_s, sin_s = jnp.cos(step), jnp.sin(step)
    for i in range(groups):
        rows = slice(i * ROPE_GROUP, (i + 1) * ROPE_GROUP)
        cb, sb = cos_b[i:i + 1, :], sin_b[i:i + 1, :]
        cos_ref[rows, :] = cb * cos_s - sb * sin_s
        sin_ref[rows, :] = (sb * cos_s + cb * sin_s) * sign


def _rope_tables(positions):
    batch, seq = positions.shape
    n = batch * seq
    inv = 1.0 / (ROPE_THETA ** (jnp.arange(0, B_ROPE, 2, dtype=F32) / B_ROPE))
    z = jnp.zeros((B_NOPE,), F32)
    zp = jnp.zeros((LANES - B_NOPE - B_ROPE,), F32)
    inv_pat = jnp.concatenate([z, inv, inv, zp])[None, :]
    ones = jnp.ones((B_ROPE // 2,), F32)
    sign_pat = jnp.concatenate([z, -ones, ones, zp])[None, :]
    tm = 1024
    assert seq % tm == 0 and tm % ROPE_GROUP == 0
    pos_groups = positions.reshape(n // ROPE_GROUP, ROPE_GROUP)[:, :1]
    return pl.pallas_call(
        _rope_kernel,
        grid=(n // tm,),
        in_specs=[pl.BlockSpec((tm // ROPE_GROUP, 1), lambda i: (i, 0)),
                  pl.BlockSpec((1, LANES), lambda i: (0, 0)),
                  pl.BlockSpec((1, LANES), lambda i: (0, 0))],
        out_specs=[pl.BlockSpec((tm, LANES), lambda i: (i, 0))] * 2,
        out_shape=[jax.ShapeDtypeStruct((n, LANES), F32)] * 2,
        compiler_params=_params(("parallel",)),
        name="rope_tables",
    )(pos_groups, inv_pat, sign_pat)


def _bucket(rel):
    n = jnp.maximum(rel, 0)
    max_exact = N_BUCKETS // 2
    nf = jnp.maximum(n, 1).astype(F32)
    large = max_exact + (jnp.log(nf / max_exact) / math.log(MAX_DIST / max_exact)
                         * (N_BUCKETS - max_exact)).astype(jnp.int32)
    large = jnp.minimum(large, N_BUCKETS - 1)
    return jnp.where(n < max_exact, n, large)


def _lookup(tab_ref, bucket, col):
    out = jnp.zeros(bucket.shape, F32)
    for k in range(N_BUCKETS):
        out = jnp.where(bucket == k, tab_ref[k, col], out)
    return out


def _bias_a_kernel(tab_ref, out_ref):
    h = pl.program_id(0)
    t = out_ref.shape[-1]
    key = lax.broadcasted_iota(jnp.int32, (t, t), 0)
    qry = lax.broadcasted_iota(jnp.int32, (t, t), 1)
    far = tab_ref[N_BUCKETS - 1, h]
    rel = qry - key
    out_ref[0, KIND_FAR] = jnp.zeros((t, t), F32)
    out_ref[0, KIND_PREV] = (_lookup(tab_ref, _bucket(rel + t), h) - far) * LOG2E
    out_ref[0, KIND_DIAG] = jnp.where(rel >= 0, (_lookup(tab_ref, _bucket(rel), h) - far) * LOG2E, NEG)


def _mask_kernel(out_ref):
    t = out_ref.shape[-1]
    key = lax.broadcasted_iota(jnp.int32, (t, t), 0)
    qry = lax.broadcasted_iota(jnp.int32, (t, t), 1)
    out_ref[0, KIND_FAR] = jnp.zeros((t, t), F32)
    out_ref[0, KIND_PREV] = jnp.zeros((t, t), F32)
    out_ref[0, KIND_DIAG] = jnp.where(key <= qry, 0.0, NEG)


def _bias_c_kernel(tab_ref, out_ref):
    h = pl.program_id(0)
    row = lax.broadcasted_iota(jnp.int32, (WINDOW, 2 * WINDOW), 0)
    col = lax.broadcasted_iota(jnp.int32, (WINDOW, 2 * WINDOW), 1)
    rel = row + WINDOW - col
    valid = (rel >= 0) & (rel < WINDOW)
    out_ref[0] = jnp.where(valid, _lookup(tab_ref, _bucket(rel), h + A_HEADS), NEG)


def _bias_tiles(table):
    smem = pl.BlockSpec(memory_space=pltpu.SMEM)
    bias_a = pl.pallas_call(
        _bias_a_kernel,
        grid=(A_HEADS,),
        in_specs=[smem],
        out_specs=pl.BlockSpec((1, 3, T_ATT, T_ATT), lambda h: (h, 0, 0, 0)),
        out_shape=jax.ShapeDtypeStruct((A_HEADS, 3, T_ATT, T_ATT), F32),
        compiler_params=_params(("parallel",)),
        name="bias_a",
    )(table)
    mask_b = pl.pallas_call(
        _mask_kernel,
        out_shape=jax.ShapeDtypeStruct((1, 3, T_ATT, T_ATT), F32),
        compiler_params=pltpu.CompilerParams(vmem_limit_bytes=VMEM_LIMIT),
        name="mask_b",
    )()
    bias_c = pl.pallas_call(
        _bias_c_kernel,
        grid=(C_HEADS,),
        in_specs=[smem],
        out_specs=pl.BlockSpec((1, WINDOW, 2 * WINDOW), lambda h: (h, 0, 0)),
        out_shape=jax.ShapeDtypeStruct((C_HEADS, WINDOW, 2 * WINDOW), F32),
        compiler_params=_params(("parallel",)),
        name="bias_c",
    )(table)
    return bias_a, mask_b, bias_c


def _rms(t, width):
    return lax.rsqrt(jnp.sum(t * t, axis=-1, keepdims=True) / width + EPS)


def _in_kernel(x_ref, cos_ref, sin_ref, gmix_ref, ga_ref, glat_ref, gqb_ref, gkb_ref, gc_ref,
               w1_ref, wqc_ref, ws_ref, wuq_ref, wukv_ref, wvat_ref, wvbt_ref,
               qa_ref, ka_ref, va_ref, qb_ref, kb_ref, vb_ref, qc_ref, kc_ref, vc_ref):
    x = x_ref[...]
    h = (x * _rms(x, D_MODEL) * gmix_ref[...]).astype(BF16)
    lo = _lane_lo()
    w_refs = (w1_ref, wqc_ref, ws_ref)

    def proj(name):
        which, a, b = _SEG[name]
        return jnp.dot(h, w_refs[which][:, a:b], preferred_element_type=F32)

    def norm_halves(t, g, out_ref):
        for j in range(t.shape[1] // LANES):
            sl = slice(j * LANES, (j + 1) * LANES)
            tj = t[:, sl]
            sq = tj * tj
            s_lo = jnp.sum(jnp.where(lo, sq, 0.0), axis=-1, keepdims=True)
            s_hi = jnp.sum(jnp.where(lo, 0.0, sq), axis=-1, keepdims=True)
            r = jnp.where(lo, lax.rsqrt(s_lo / HALF + EPS), lax.rsqrt(s_hi / HALF + EPS))
            out_ref[:, sl] = (tj * r * g[:, sl]).astype(out_ref.dtype)

    glat = glat_ref[...]
    cq = proj("cq")
    ckv = proj("ckv")
    kpe_both = proj("kpe_both")
    kpe = kpe_both[:, :LANES]
    kpe_sw = kpe_both[:, LANES:]
    cqn = (cq * _rms(cq, B_Q_RANK) * glat[:, :B_Q_RANK]).astype(BF16)
    ckvn = (ckv * _rms(ckv, B_KV_RANK) * glat[:, B_Q_RANK:]).astype(BF16)
    uq = jnp.dot(cqn, wuq_ref[...], preferred_element_type=F32)
    ukv = jnp.dot(ckvn, wukv_ref[...], preferred_element_type=F32)
    qa = proj("qa")
    ka = proj("ka")
    qc = proj("qc")
    kc = proj("kc")
    vc = proj("vc")
    nt = (((1,), (1,)), ((), ()))
    va_ref[...] = lax.dot_general(wvat_ref[...], h, nt, preferred_element_type=F32).astype(BF16)
    vb_ref[...] = lax.dot_general(wvbt_ref[...], ckvn, nt, preferred_element_type=F32).astype(BF16)

    cos = cos_ref[...]
    sin = sin_ref[...]
    hw = B_HEADS * LANES
    for j in range(B_HEADS):
        sl = slice(j * LANES, (j + 1) * LANES)
        raw = uq[:, sl]
        raw_sw = uq[:, hw + j * LANES: hw + (j + 1) * LANES]
        sq = raw * raw
        s_n = jnp.sum(jnp.where(lo, sq, 0.0), axis=-1, keepdims=True)
        s_r = jnp.sum(jnp.where(lo, 0.0, sq), axis=-1, keepdims=True)
        r = jnp.where(lo, lax.rsqrt(s_n / B_NOPE + EPS), lax.rsqrt(s_r / B_ROPE + EPS))
        out = r * (raw * gqb_ref[0:1, sl] * cos + raw_sw * gqb_ref[1:2, sl] * sin)
        qb_ref[:, sl] = out.astype(BF16)

    kpe_out = _rms(kpe, B_ROPE) * (kpe * gkb_ref[1:2, :] * cos + kpe_sw * gkb_ref[2:3, :] * sin)
    for j in range(B_HEADS):
        sl = slice(j * LANES, (j + 1) * LANES)
        raw = ukv[:, sl]
        kb_ref[:, sl] = (raw * _rms(raw, B_NOPE) * gkb_ref[0:1, :] + kpe_out).astype(BF16)

    norm_halves(qa, ga_ref[0:1, :], qa_ref)
    norm_halves(ka, ga_ref[1:2, :], ka_ref)
    gc = gc_ref[...]
    norm_halves(qc, gc[:, :C_HEADS * C_HD], qc_ref)
    norm_halves(kc, gc[:, C_HEADS * C_HD:], kc_ref)
    vc_ref[...] = vc.astype(BF16)


def _in_proj(x2, cos, sin, gmix, ga, glat, gqb, gkb, gc, w1, wqc, ws, wuq, wukv, wvat, wvbt, batch, seq):
    n = x2.shape[0]
    tm = TM_IN
    nps = seq // tm
    row = lambda w: pl.BlockSpec((tm, w), lambda i: (i, 0))
    outs = (("qa", 512), ("ka", 512), ("va", None), ("qb", 1024), ("kb", 1024), ("vb", None),
            ("qc", 512), ("kc", 256), ("vc", 256))
    vt_spec = pl.BlockSpec((None, 512, tm), lambda i: (i // nps, 0, i % nps))
    vt_shape = jax.ShapeDtypeStruct((batch, 512, seq), BF16)
    return pl.pallas_call(
        _in_kernel,
        grid=(n // tm,),
        in_specs=[row(D_MODEL), row(LANES), row(LANES),
                  _const_spec(gmix.shape), _const_spec(ga.shape), _const_spec(glat.shape),
                  _const_spec(gqb.shape), _const_spec(gkb.shape), _const_spec(gc.shape),
                  _const_spec(w1.shape), _const_spec(wqc.shape), _const_spec(ws.shape),
                  _const_spec(wuq.shape), _const_spec(wukv.shape),
                  _const_spec(wvat.shape), _const_spec(wvbt.shape)],
        out_specs=[vt_spec if w is None else row(w) for _, w in outs],
        out_shape=[vt_shape if w is None else jax.ShapeDtypeStruct((n, w), BF16) for _, w in outs],
        compiler_params=_params(("parallel",)),
        name="in_proj",
    )(x2, cos, sin, gmix, ga, glat, gqb, gkb, gc, w1, wqc, ws, wuq, wukv, wvat, wvbt)


def _attn_items(nq):
    items = [(qi, ki, KIND_DIAG if ki == qi else KIND_PREV if ki == qi - 1 else KIND_FAR)
             for qi in range(nq) for ki in range(qi + 1)]
    items += [items[-1]] * 2
    return np.asarray(items, np.int32).T


def _attn_kernel(*refs, mode, lambda_init, n_items, nq):
    if mode == "diff":
        (tab_ref, q_ref, k_ref, vt_ref, bias_ref, lam_ref, subg_ref, o_ref,
         s_scr, p_scr, acc_scr, m_scr, mt_scr, al_scr) = refs
    else:
        tab_ref, q_ref, k_ref, vt_ref, bias_ref, o_ref, s_scr, p_scr, acc_scr, m_scr, mt_scr, al_scr = refs
    t = T_ATT
    dv = acc_scr.shape[2] - DEN_ROWS
    lo = _lane_lo()
    m_scr[...] = jnp.full(m_scr.shape, NEG, F32)
    acc_scr[...] = jnp.zeros_like(acc_scr)
    if mode == "diff":
        lv = lam_ref[...]
        lam = (jnp.exp(jnp.sum(lv[0:1] * lv[1:2], axis=-1, keepdims=True))
               - jnp.exp(jnp.sum(lv[2:3] * lv[3:4], axis=-1, keepdims=True)) + lambda_init)
        out_gain = subg_ref[...] * (1.0 - lambda_init)

    def rows(idx):
        return pl.ds(pl.multiple_of(idx * t, t), t)

    def scores(n, slot):
        q = q_ref[rows(tab_ref[0, n]), :]
        k = k_ref[rows(tab_ref[1, n]), :]
        kind = tab_ref[2, n]
        for j in range(2):
            if mode == "diff":
                zero = jnp.zeros_like(q)
                qj = jnp.where(lo, q, zero) if j == 0 else jnp.where(lo, zero, q)
                kj = k
            else:
                qj = q[:, j * LANES:(j + 1) * LANES]
                kj = k[:, j * LANES:(j + 1) * LANES]
            s = lax.dot_general(kj, qj, (((1,), (1,)), ((), ())), preferred_element_type=F32)
            s = s + bias_ref[kind]
            s_scr[2 * slot + j] = s
            mt_scr[2 * slot + j] = jnp.max(s, axis=0, keepdims=True)

    def softmax(n, slot, pslot):
        restart = jnp.where(tab_ref[1, n] == 0, NEG, 0.0)
        for j in range(2):
            c = 2 * slot + j
            pc = 2 * pslot + j
            m_old = m_scr[j] + restart
            m_new = jnp.maximum(m_old, mt_scr[c])
            al_scr[pc] = jnp.exp2(m_old - m_new)
            m_scr[j] = m_new
            for r0 in range(0, t, ROW_CHUNK):
                d = s_scr[c, r0:r0 + ROW_CHUNK, :] - m_new
                p_scr[pc, r0:r0 + ROW_CHUNK, :] = jnp.exp2(d.astype(BF16))

    def values(n, pslot):
        qi = tab_ref[0, n]
        v = vt_ref[:, rows(tab_ref[1, n])]
        ones = jnp.ones((DEN_ROWS, t), BF16)
        for j in range(2):
            pc = 2 * pslot + j
            vt = jnp.concatenate([v if mode == "diff" else v[j * dv:(j + 1) * dv, :], ones], axis=0)
            acc_scr[qi, j] = al_scr[pc] * acc_scr[qi, j] + jnp.dot(vt, p_scr[pc], preferred_element_type=F32)

    def finalize(qi):
        outs = [acc_scr[qi, j, :dv, :] / acc_scr[qi, j, dv:dv + 1, :] for j in range(2)]
        if mode == "diff":
            o = outs[0] - lam * outs[1]
            o = o * lax.rsqrt(jnp.sum(o * o, axis=0, keepdims=True) / A_VD + EPS)
            o = o.T * out_gain
        else:
            o = jnp.concatenate(outs, axis=0).T
        o_ref[qi * t:(qi + 1) * t, :] = o.astype(o_ref.dtype)

    scores(0, 0)
    scores(1, 1)

    def body(i, carry):
        for u in range(ITEMS_PER_TRIP):
            n = ITEMS_PER_TRIP * i + u
            s = u % S_SLOTS
            scores(n + 2, (s + 2) % S_SLOTS)
            softmax(n, s, s % P_SLOTS)
            if u > 0:
                values(n - 1, (s + 1) % P_SLOTS)
        values(ITEMS_PER_TRIP * i + ITEMS_PER_TRIP - 1, (ITEMS_PER_TRIP - 1) % P_SLOTS)
        return carry

    lax.fori_loop(0, n_items // ITEMS_PER_TRIP, body, 0)
    for qi in range(nq):
        finalize(qi)


def _attn(q, k, vt, bias, batch, seq, mode, extra=(), lambda_init=0.0):
    n = q.shape[0]
    t = T_ATT
    nq = seq // t
    tab = _attn_items(nq)
    n_items = tab.shape[1] - 2
    assert n_items % ITEMS_PER_TRIP == 0 and ITEMS_PER_TRIP % S_SLOTS == 0 and S_SLOTS % P_SLOTS == 0
    qw = LANES if mode == "diff" else 2 * LANES
    groups = q.shape[1] // qw
    per_head = bias.shape[0] > 1
    in_specs = [pl.BlockSpec(memory_space=pltpu.SMEM),
                pl.BlockSpec((seq, qw), lambda b, g: (b, g)),
                pl.BlockSpec((seq, qw), lambda b, g: (b, g)),
                pl.BlockSpec((None, LANES, seq), lambda b, g: (b, g, 0)),
                pl.BlockSpec((None, 3, t, t), lambda b, g: (g if per_head else 0, 0, 0, 0))]
    in_specs += [pl.BlockSpec(e.shape, lambda b, g: (0, 0)) for e in extra]
    return pl.pallas_call(
        functools.partial(_attn_kernel, mode=mode, lambda_init=lambda_init, n_items=n_items, nq=nq),
        grid=(batch, groups),
        in_specs=in_specs,
        out_specs=pl.BlockSpec((seq, LANES), lambda b, g: (b, g)),
        out_shape=jax.ShapeDtypeStruct((n, groups * LANES), BF16),
        scratch_shapes=[pltpu.VMEM((2 * S_SLOTS, t, t), F32), pltpu.VMEM((2 * P_SLOTS, t, t), BF16),
                        pltpu.VMEM((nq, 2, (A_VD if mode == "diff" else B_VD) + DEN_ROWS, t), F32), pltpu.VMEM((2, 1, t), F32),
                        pltpu.VMEM((2 * S_SLOTS, 1, t), F32), pltpu.VMEM((2 * P_SLOTS, 1, t), F32)],
        compiler_params=_params(("parallel", "parallel")),
        name="attn_" + mode,
    )(jnp.asarray(tab), q, k, vt, bias, *extra)


def _swa_kernel(sink_ref, q_ref, kp_ref, kc_ref, vp_ref, vc_ref, bias_ref, o_ref):
    first_step = pl.program_id(1) == 0
    lo = _lane_lo()
    col = lax.broadcasted_iota(jnp.int32, (WINDOW, 2 * WINDOW), 1)
    keep = (col >= WINDOW) | jnp.logical_not(first_step)
    grp = C_HEADS // C_KV_HEADS
    k_all = jnp.concatenate([kp_ref[...], kc_ref[...]], axis=0)
    v_all = jnp.concatenate([vp_ref[...], vc_ref[...]], axis=0)
    work = [(w, head) for w in range(SWA_WINDOWS) for head in range(C_HEADS)]
    scores = []
    for w, head in work:
        hp, j = divmod(head, 2)
        ksl = slice((head // grp) * LANES, (head // grp + 1) * LANES)
        q = q_ref[w * WINDOW:(w + 1) * WINDOW, hp * LANES:(hp + 1) * LANES]
        zero = jnp.zeros_like(q)
        qj = jnp.where(lo, q, zero) if j == 0 else jnp.where(lo, zero, q)
        k = k_all[w * WINDOW:(w + 2) * WINDOW, ksl]
        scores.append(lax.dot_general(qj, k, (((1,), (1,)), ((), ())), preferred_element_type=F32))
    probs = []
    for (w, head), s in zip(work, scores):
        s = s + bias_ref[head]
        if w == 0:
            s = jnp.where(keep, s, NEG)
        sink = sink_ref[head]
        m = jnp.maximum(jnp.max(s, axis=-1, keepdims=True), sink)
        p = jnp.exp(s - m)
        den = jnp.sum(p, axis=-1, keepdims=True) + jnp.exp(sink - m)
        probs.append((p.astype(BF16), den))
    for w in range(SWA_WINDOWS):
        for hp in range(C_HEADS // 2):
            ksl = slice(((2 * hp) // grp) * LANES, ((2 * hp) // grp + 1) * LANES)
            v = v_all[w * WINDOW:(w + 2) * WINDOW, ksl]
            pair = probs[w * C_HEADS + 2 * hp:w * C_HEADS + 2 * hp + 2]
            outs = [jnp.dot(p, v, preferred_element_type=F32) / den for p, den in pair]
            o_ref[w * WINDOW:(w + 1) * WINDOW, hp * LANES:(hp + 1) * LANES] = (
                jnp.where(lo, outs[0], outs[1]).astype(o_ref.dtype))


def _swa(sinks, q, k, v, bias, batch, seq):
    n = q.shape[0]
    rows = SWA_WINDOWS * WINDOW
    steps = seq // rows
    cur = lambda b, i: (b * steps + i, 0)
    prev = lambda b, i: ((b * steps + i) * SWA_WINDOWS - jnp.minimum(i, 1), 0)
    kw = k.shape[1]
    return pl.pallas_call(
        _swa_kernel,
        grid=(batch, steps),
        in_specs=[pl.BlockSpec(memory_space=pltpu.SMEM),
                  pl.BlockSpec((rows, q.shape[1]), cur),
                  pl.BlockSpec((WINDOW, kw), prev), pl.BlockSpec((rows, kw), cur),
                  pl.BlockSpec((WINDOW, kw), prev), pl.BlockSpec((rows, kw), cur),
                  pl.BlockSpec(bias.shape, lambda b, i: (0, 0, 0))],
        out_specs=pl.BlockSpec((rows, q.shape[1]), cur),
        out_shape=jax.ShapeDtypeStruct((n, q.shape[1]), BF16),
        compiler_params=_params(("parallel", "arbitrary")),
        name="swa",
    )(sinks, q, k, k, v, v, bias)


def _merge_kernel(x_ref, ya_ref, yb_ref, yc_ref, gmix_ref, wg_ref, pa_ref, pb_ref, pc_ref, wo_ref, o_ref):
    x = x_ref[...]
    h = (x * _rms(x, D_MODEL) * gmix_ref[...]).astype(BF16)
    merged = None
    for j, (y_ref, p_ref) in enumerate(((ya_ref, pa_ref), (yb_ref, pb_ref), (yc_ref, pc_ref))):
        gate = jnp.dot(h, wg_ref[:, j * D_MODEL:(j + 1) * D_MODEL], preferred_element_type=F32)
        term = jax.nn.sigmoid(gate) * jnp.dot(y_ref[...], p_ref[...], preferred_element_type=F32)
        merged = term if merged is None else merged + term
    o_ref[...] = x + jnp.dot(merged.astype(BF16), wo_ref[...], preferred_element_type=F32)


def _merge(x2, ya, yb, yc, gmix, wg, pa, pb, pc, wo, layer):
    n = x2.shape[0]
    tm = TM_MERGE
    row = lambda w: pl.BlockSpec((tm, w), lambda i: (i, 0))
    return pl.pallas_call(
        _merge_kernel,
        grid=(n // tm,),
        in_specs=[row(D_MODEL), row(ya.shape[1]), row(yb.shape[1]), row(yc.shape[1]),
                  _const_spec(gmix.shape), _const_spec(wg.shape), _layer_spec(pa, layer),
                  _layer_spec(pb, layer), _layer_spec(pc, layer), _layer_spec(wo, layer)],
        out_specs=row(D_MODEL),
        out_shape=jax.ShapeDtypeStruct((n, D_MODEL), F32),
        compiler_params=_params(("parallel",)),
        name="merge",
    )(x2, ya, yb, yc, gmix, wg, pa, pb, pc, wo)


def _ffn_kernel(x_ref, g_ref, wup_ref, cw_ref, cb_ref, wdn_ref, o_ref, ubuf, act, carry, *, tiles_per_seq):
    tm = x_ref.shape[0]
    fc = FF_CHUNK

    @pl.when(pl.program_id(0) % tiles_per_seq == 0)
    def _():
        carry[...] = jnp.zeros_like(carry)

    x = x_ref[...]
    h = (x * _rms(x, D_MODEL) * g_ref[...]).astype(BF16)

    def cols(ref, rows, j):
        return jnp.concatenate([ref[rows, j * fc:(j + 1) * fc], ref[rows, D_FF + j * fc:D_FF + (j + 1) * fc]],
                               axis=1)

    def up(j):
        for half, base in enumerate((j * fc, D_FF + j * fc)):
            ubuf[j, :, half * fc:(half + 1) * fc] = jnp.dot(
                h, wup_ref[:, base:base + fc], preferred_element_type=F32)

    def conv_act(j):
        ext = jnp.concatenate([carry[j], ubuf[j]], axis=0)
        carry[j] = ubuf[j, tm - HALO:tm, :]
        y = cols(cb_ref, slice(0, 1), j)
        for tap in range(CONV_W):
            shift = CONV_W - 1 - tap
            tok = ext if shift == 0 else pltpu.roll(ext, shift, axis=0)
            y = y + cols(cw_ref, slice(tap, tap + 1), j) * tok[HALO:HALO + tm, :]
        gate = y[:, :fc]
        half = 0.5 * gate
        act[:, j * fc:(j + 1) * fc] = ((half + half * jnp.tanh(half)) * y[:, fc:]).astype(BF16)

    up(0)
    for j in range(N_FF_CHUNKS):
        if j + 1 < N_FF_CHUNKS:
            up(j + 1)
        conv_act(j)
    o_ref[...] = x + jnp.dot(act[...], wdn_ref[...], preferred_element_type=F32)


def _ffn(x2, g, wup, cw, cb, wdn, seq, layer):
    n = x2.shape[0]
    tm = TM_FFN
    row = pl.BlockSpec((tm, D_MODEL), lambda i: (i, 0))
    return pl.pallas_call(
        functools.partial(_ffn_kernel, tiles_per_seq=seq // tm),
        grid=(n // tm,),
        in_specs=[row, _const_spec(g.shape), _layer_spec(wup, layer), _const_spec(cw.shape),
                  _const_spec(cb.shape), _layer_spec(wdn, layer)],
        out_specs=row,
        out_shape=jax.ShapeDtypeStruct((n, D_MODEL), F32),
        scratch_shapes=[pltpu.VMEM((N_FF_CHUNKS, tm, 2 * FF_CHUNK), F32),
                        pltpu.VMEM((tm, D_FF), BF16),
                        pltpu.VMEM((N_FF_CHUNKS, HALO, 2 * FF_CHUNK), F32)],
        compiler_params=_params(("arbitrary",)),
        name="ffn",
    )(x2, g, wup, cw, cb, wdn)


def _layer_params(l, w_in, a_q_g, a_k_g, b_q_a_g, b_kv_a_g, b_w_uq, b_w_ukv, b_qn_g, b_qr_g, b_kn_g,
                  b_kr_g, c_q_g, c_k_g, w_up, conv_w, conv_b):
    w = w_in[l]
    o_kpe = 3 * 512 + B_Q_RANK + B_KV_RANK
    o_qc = o_kpe + B_ROPE
    o_kc = o_qc + C_HEADS * C_HD
    o_vc = o_kc + C_KV_HEADS * C_HD
    o_g = o_vc + C_KV_HEADS * C_HD
    w1 = w[:, :o_kpe].astype(BF16)
    wqc = w[:, o_qc:o_kc].astype(BF16)
    wg = w[:, o_g:].astype(BF16)
    kpe_w = w[:, o_kpe:o_qc].astype(BF16)
    kc_w = w[:, o_kc:o_vc].astype(BF16)
    vc_w = w[:, o_vc:o_g].astype(BF16)
    z = lambda c: jnp.zeros((w.shape[0], c), BF16)
    r = B_ROPE // 2
    pad = LANES - B_NOPE - B_ROPE
    dup = lambda t: jnp.concatenate([t[:, :C_HD], t[:, :C_HD], t[:, C_HD:], t[:, C_HD:]], axis=1)
    ws = jnp.concatenate([
        z(B_NOPE), kpe_w, z(pad),
        z(B_NOPE), kpe_w[:, r:], kpe_w[:, :r], z(pad),
        dup(kc_w), dup(vc_w)], axis=1)

    uq = b_w_uq[l].reshape(B_Q_RANK, B_HEADS, B_NOPE + B_ROPE)
    nope, pe = uq[..., :B_NOPE], uq[..., B_NOPE:]
    zq = lambda c: jnp.zeros((B_Q_RANK, B_HEADS, c), uq.dtype)
    wuq = jnp.concatenate([
        jnp.concatenate([nope, pe, zq(pad)], axis=-1).reshape(B_Q_RANK, B_HEADS * LANES),
        jnp.concatenate([zq(B_NOPE), pe[..., r:], pe[..., :r], zq(pad)], axis=-1).reshape(B_Q_RANK, B_HEADS * LANES),
    ], axis=1).astype(BF16)

    ukv = b_w_ukv[l].reshape(B_KV_RANK, B_HEADS, B_NOPE + B_VD)
    zk = jnp.zeros((B_KV_RANK, B_HEADS, LANES - B_NOPE), ukv.dtype)
    wukv = jnp.concatenate([ukv[..., :B_NOPE], zk], axis=-1).reshape(B_KV_RANK, B_HEADS * LANES).astype(BF16)
    wvbt = ukv[..., B_NOPE:].reshape(B_KV_RANK, B_HEADS * B_VD).T.astype(BF16)
    wvat = w[:, 2 * 512:3 * 512].T.astype(BF16)

    scale_a = A_HD ** -0.5 * LOG2E
    scale_b = (B_NOPE + B_ROPE) ** -0.5 * LOG2E
    scale_c = C_HD ** -0.5
    ga = jnp.stack([jnp.tile(a_q_g[l], 2 * A_HEADS) * scale_a, jnp.tile(a_k_g[l], 2 * A_HEADS)])
    glat = jnp.concatenate([b_q_a_g[l], b_kv_a_g[l]])[None, :]
    zl = lambda c: jnp.zeros((c,), F32)
    qr, kr = b_qr_g[l], b_kr_g[l]
    gqb = jnp.stack([
        jnp.tile(jnp.concatenate([b_qn_g[l], qr, zl(pad)]), B_HEADS),
        jnp.tile(jnp.concatenate([zl(B_NOPE), qr[r:], qr[:r], zl(pad)]), B_HEADS)]) * scale_b
    gkb = jnp.stack([
        jnp.concatenate([b_kn_g[l], zl(LANES - B_NOPE)]),
        jnp.concatenate([zl(B_NOPE), kr, zl(pad)]),
        jnp.concatenate([zl(B_NOPE), kr[r:], kr[:r], zl(pad)])])
    gc = jnp.concatenate([jnp.tile(c_q_g[l], C_HEADS) * scale_c, jnp.tile(c_k_g[l], 2 * C_KV_HEADS)])[None, :]

    return dict(w1=w1, wqc=wqc, ws=ws, wg=wg, wuq=wuq, wukv=wukv, wvat=wvat, wvbt=wvbt, ga=ga, glat=glat,
                gqb=gqb, gkb=gkb, gc=gc, cw=conv_w[l, :, 0, :], cb=conv_b[l][None, :])


def kernel(x, positions, rel_bias_table, ln_mix_g, w_in, a_q_g, a_k_g, a_lam_q1, a_lam_k1, a_lam_q2, a_lam_k2, a_subln_g, b_q_a_g, b_kv_a_g, b_w_uq, b_w_ukv, b_qn_g, b_qr_g, b_kn_g, b_kr_g, c_q_g, c_k_g, c_sinks, p_a, p_b, p_c, w_o, ln_ffn_g, w_up, conv_w, conv_b, w_down):
    batch, seq, d = x.shape
    n = batch * seq
    assert d == D_MODEL and seq % T_ATT == 0 and n % TM_IN == 0 and seq % TM_FFN == 0
    x2 = x.reshape(n, d)
    cos, sin = _rope_tables(positions)
    bias_a, mask_b, bias_c = _bias_tiles(rel_bias_table)
    wup_bf, wdn_bf = _to_bf16(w_up, 256), _to_bf16(w_down, 704)
    pa_bf, pb_bf, pc_bf, wo_bf = (_to_bf16(w, 256) for w in (p_a, p_b, p_c, w_o))
    for l in range(DEPTH):
        lambda_init = 0.8 - 0.6 * math.exp(-0.3 * l)
        p = _layer_params(l, w_in, a_q_g, a_k_g, b_q_a_g, b_kv_a_g, b_w_uq, b_w_ukv, b_qn_g, b_qr_g,
                          b_kn_g, b_kr_g, c_q_g, c_k_g, w_up, conv_w, conv_b)
        gmix = ln_mix_g[l][None, :]
        qa, ka, va, qb, kb, vb, qc, kc, vc = _in_proj(
            x2, cos, sin, gmix, p["ga"], p["glat"], p["gqb"], p["gkb"], p["gc"],
            p["w1"], p["wqc"], p["ws"], p["wuq"], p["wukv"], p["wvat"], p["wvbt"], batch, seq)
        lam = jnp.stack([a_lam_q1[l], a_lam_k1[l], a_lam_q2[l], a_lam_k2[l]])
        ya = _attn(qa, ka, va, bias_a, batch, seq, "diff", extra=(lam, a_subln_g[l][None, :]),
                   lambda_init=lambda_init)
        yb = _attn(qb, kb, vb, mask_b, batch, seq, "mla")
        yc = _swa(c_sinks[l], qc, kc, vc, bias_c, batch, seq)
        x2 = _merge(x2, ya, yb, yc, gmix, p["wg"], pa_bf, pb_bf, pc_bf, wo_bf, l)
        x2 = _ffn(x2, ln_ffn_g[l][None, :], wup_bf, p["cw"], p["cb"], wdn_bf, seq, l)
    return x2.reshape(batch, seq, d)
```
